```python
import math
import jax, jax.numpy as jnp
from jax import lax
import numpy as np

D_MODEL = 2048
BATCH = 32
SEQ = 256
DEPTH = 2
DEC_BATCH = 2
DEC_SEQ = 1024
PAST_LEN = 512

GRID_W = 64
N_EVEN = (DEPTH + 1) // 2
N_ODD = DEPTH // 2
N_MOD = 9
NORM_EPS = 1e-6
D_FF = 5632
HEAD_DIM = 128
N_Q_HEADS = 8
N_KV_HEADS = 2
Q_PER_KV = N_Q_HEADS // N_KV_HEADS
ATTN_W = N_Q_HEADS * HEAD_DIM
KV_W = N_KV_HEADS * HEAD_DIM
Q_BLOCK = 128
ROPE_THETA = 10000.0
ROPE_AXIS_DIM = HEAD_DIM // 2
LRU_W = 1024
LRU_HEADS = 8
LRU_BLK = LRU_W // LRU_HEADS
LRU_CONV = 4
LRU_CONV_LEFT = 2
LRU_C = 8.0
MIX_W_AB = ATTN_W + LRU_W
IN_W_AB = ATTN_W + 2 * KV_W + 2 * LRU_W
HY_W = 1024
HY_ORDER = 2
HY_CONV = 3
HY_CONV_LEFT = 1
HY_EMB = 33
HY_BANDS = (HY_EMB - 1) // 2
HY_FH = 64
HY_DECAY_MIN = math.log(100.0) / 1.5
HY_DECAY_MAX = math.log(100.0) / 0.3
POOL_W = 1024
POOL_WINDOWS = (2, 4, 8, 16)
POOL_GROUPS = len(POOL_WINDOWS)
POOL_GW = POOL_W // POOL_GROUPS
MIX_W_CD = HY_W + POOL_W
IN_W_CD = (HY_ORDER + 1) * HY_W + POOL_W

kernel_name = 'hybrid_diffusion_prefix_trunk_step'

F32 = jnp.float32


def rmsnorm(x, g):
    xf = x.astype(F32)
    y = xf * lax.rsqrt(jnp.mean(xf * xf, axis=-1, keepdims=True) + NORM_EPS)
    return (y * g.astype(F32)).astype(x.dtype)


def modulation(cvec, w_mod, b_mod):
    m = jax.nn.silu(cvec) @ w_mod + b_mod
    return m.reshape(*cvec.shape[:-1], N_MOD, D_MODEL)


def modulate(x, g, m, j):
    return rmsnorm(x, g) * (1 + m[..., 3 * j + 1, :]) + m[..., 3 * j, :]


def swiglu(h, w13, w2):
    gate, up = jnp.split(h @ w13, 2, axis=-1)
    return (jax.nn.silu(gate) * up) @ w2


def ffn_sublayer(x, m, j, g, w13, w2):
    return x + 0.5 * m[..., 3 * j + 2, :] * swiglu(modulate(x, g, m, j), w13, w2)


def dwconv(x, w, b, left):
    k, L = w.shape[0], x.shape[1]
    xp = jnp.pad(x, ((0, 0), (left, k - 1 - left), (0, 0)))
    return sum(xp[:, j:j + L] * w[j] for j in range(k)) + b


def grid_rope(L):
    rows = L // GRID_W
    r = jnp.repeat(jnp.arange(rows), GRID_W).astype(F32)
    col = jnp.tile(jnp.arange(GRID_W), rows).astype(F32)
    inv = ROPE_THETA ** (-jnp.arange(0, ROPE_AXIS_DIM, 2, dtype=F32) / ROPE_AXIS_DIM)
    ang = jnp.concatenate([r[:, None] * inv, col[:, None] * inv], axis=-1)
    return jnp.cos(ang), jnp.sin(ang)


def apply_rope(x, cos, sin):
    xf = x.astype(F32).reshape(*x.shape[:-1], HEAD_DIM // 2, 2)
    x0, x1 = xf[..., 0], xf[..., 1]
    cs, sn = cos[None, :, None], sin[None, :, None]
    out = jnp.stack([x0 * cs - x1 * sn, x0 * sn + x1 * cs], axis=-1)
    return out.reshape(x.shape).astype(x.dtype)


def block_attention(q, k, v):
    B, T = q.shape[:2]
    nb = T // Q_BLOCK
    qb = jnp.moveaxis(q.reshape(B, nb, Q_BLOCK, N_KV_HEADS, Q_PER_KV, HEAD_DIM), 1, 0)
    scale = HEAD_DIM ** -0.5

    def one_block(qi):
        s = jnp.einsum('bqkgd,bskd->bkgqs', qi, k).astype(F32) * scale
        p = jax.nn.softmax(s, axis=-1).astype(v.dtype)
        return jnp.einsum('bkgqs,bskd->bqkgd', p, v)

    o = lax.map(one_block, qb)
    return jnp.moveaxis(o, 0, 1).reshape(B, T, ATTN_W)


def lru_combine(e1, e2):
    a1, b1 = e1
    a2, b2 = e2
    return a1 * a2, a2 * b1 + b2


def lru_scan(a, b, h0, reverse):
    edge = -1 if reverse else 0
    b = b.at[:, edge].add(a[:, edge] * h0)
    _, h = lax.associative_scan(lru_combine, (a, b), reverse=reverse, axis=1)
    return h


def rg_lru(xc, gate_w, gate_b, lam, h0_f, h0_b):
    B, L, _ = xc.shape
    xh = xc.reshape(B, L, LRU_HEADS, LRU_BLK)
    g = jnp.einsum('blhi,dghij->dgblhj', xh, gate_w).reshape(2, 2, B, L, LRU_W)
    g = jax.nn.sigmoid((g + gate_b[:, :, None, None]).astype(F32))
    r, i = g[:, 0], g[:, 1]
    log_a = -LRU_C * r * jax.nn.softplus(-lam.astype(F32))[:, None, None]
    a = jnp.exp(log_a)
    b = jnp.sqrt(-jnp.expm1(2.0 * log_a)) * (i * xc.astype(F32))
    hf = lru_scan(a[0], b[0], h0_f.astype(F32), False)
    hb = lru_scan(a[1], b[1], h0_b.astype(F32), True)
    return (hf + hb).astype(xc.dtype), hf[:, -1], hb[:, 0]


def mixer_ab(h, w_in, q_norm, k_norm, conv_w, conv_b, gate_w, gate_b, lam, w_out, ctx):
    B, L, _ = h.shape
    q, k, v, lx, lg = jnp.split(
        h @ w_in, [ATTN_W, ATTN_W + KV_W, ATTN_W + 2 * KV_W, ATTN_W + 2 * KV_W + LRU_W], axis=-1)
    q = rmsnorm(q.reshape(B, L, N_Q_HEADS, HEAD_DIM), q_norm)
    k = rmsnorm(k.reshape(B, L, N_KV_HEADS, HEAD_DIM), k_norm)
    v = v.reshape(B, L, N_KV_HEADS, HEAD_DIM)
    if ctx is None:
        k_all, v_all = k, v
        h0_f = h0_b = jnp.zeros((B, LRU_W), h.dtype)
    else:
        k_ctx, v_ctx, h0_f, h0_b = ctx
        cos, sin = grid_rope(L)
        q = apply_rope(q, cos, sin)
        k = apply_rope(k, cos, sin)
        k_all = jnp.concatenate([k_ctx, k], axis=1)
        v_all = jnp.concatenate([v_ctx, v], axis=1)
    attn = block_attention(q.reshape(B, L, N_KV_HEADS, Q_PER_KV, HEAD_DIM), k_all, v_all)
    xc = dwconv(lx, conv_w, conv_b, LRU_CONV_LEFT)
    rec, hf_last, hb_first = rg_lru(xc, gate_w, gate_b, lam, h0_f, h0_b)
    rec = jax.nn.gelu(lg) * rec
    out = jnp.concatenate([attn, rec], axis=-1) @ w_out
    if ctx is None:
        return out, (k, v, hf_last.astype(h.dtype), hb_first.astype(h.dtype))
    return out, None


def hyena_filters(L, w1, b1, w2, b2, w3, freq, decay):
    n = jnp.arange(L, dtype=F32)
    t = n / max(L - 1, 1)
    bands = jnp.linspace(1e-4, HY_BANDS - 1, HY_BANDS, dtype=F32)
    f = 2.0 * math.pi * n[:, None] * bands[None, :] / L
    z = jnp.concatenate([t[:, None], jnp.cos(f), -jnp.sin(f)], axis=-1).astype(w1.dtype)
    z = jnp.sin(freq[0] * (z @ w1 + b1))
    z = jnp.sin(freq[1] * (z @ w2 + b2))
    filt = (z @ w3).astype(F32) * jnp.exp(-t[:, None] * jnp.abs(decay.astype(F32)))
    filt = filt / jnp.sum(jnp.abs(filt), axis=0, keepdims=True)
    return filt.reshape(L, HY_ORDER, HY_W)


def sym_longconv(u, hk, d):
    L = u.shape[1]
    kern = jnp.concatenate([hk, jnp.zeros((1, hk.shape[1]), F32), hk[:0:-1]], axis=0)
    uf = jnp.fft.rfft(u.astype(F32), n=2 * L, axis=1)
    kf = jnp.fft.rfft(kern, n=2 * L, axis=0)
    y = jnp.fft.irfft(uf * kf[None], n=2 * L, axis=1)[:, :L]
    return (y + u.astype(F32) * d.astype(F32)).astype(u.dtype)


def multi_pool(x, pool_w, pool_scale):
    B, L, _ = x.shape
    xf = x.astype(F32)
    cs = jnp.concatenate([jnp.zeros((B, 1, POOL_W), F32), jnp.cumsum(xf, axis=1)], axis=1)
    t = jnp.arange(L)
    outs = []
    for gi, w in enumerate(POOL_WINDOWS):
        lo = jnp.clip(t - w // 2, 0, L)
        hi = jnp.clip(t + w // 2, 0, L)
        sl = slice(gi * POOL_GW, (gi + 1) * POOL_GW)
        csg = cs[:, :, sl]
        mean = (csg[:, hi] - csg[:, lo]) / (hi - lo).astype(F32)[None, :, None]
        outs.append((mean - xf[:, :, sl]).astype(x.dtype) @ pool_w[gi])
    return jnp.concatenate(outs, axis=-1) * pool_scale


def mixer_cd(h, w_in, conv_w, conv_b, fw1, fb1, fw2, fb2, fw3, ffreq, fdecay, fskip,
             pool_w, pool_scale, w_out):
    L = h.shape[1]
    proj = h @ w_in
    hy, pl = proj[..., :(HY_ORDER + 1) * HY_W], proj[..., (HY_ORDER + 1) * HY_W:]
    hy = dwconv(hy, conv_w, conv_b, HY_CONV_LEFT)
    x1, x2, v = jnp.split(hy, HY_ORDER + 1, axis=-1)
    filt = hyena_filters(L, fw1, fb1, fw2, fb2, fw3, ffreq, fdecay)
    z = x1 * sym_longconv(v, filt[:, 0], fskip[0])
    z = x2 * sym_longconv(z, filt[:, 1], fskip[1])
    out = jnp.concatenate([z, multi_pool(pl, pool_w, pool_scale)], axis=-1) @ w_out
    return out


def setup_inputs(seed: int = 0) -> dict:
    key = jax.random.key(seed)
    ks = iter(jax.random.split(key, 40))

    def nrm(shape, scale):
        return jax.random.normal(next(ks), shape, F32) * scale

    def gain(shape):
        return 1.0 + nrm(shape, 0.05)

    u = jax.random.uniform(next(ks), (N_EVEN, 2, LRU_W), F32, 0.9, 0.999)
    s = u ** (1.0 / LRU_C)
    lru_lambda = jnp.log(s) - jnp.log1p(-s)
    hy_decay = jax.random.uniform(next(ks), (N_ODD, HY_ORDER * HY_W), F32, HY_DECAY_MIN, HY_DECAY_MAX)
    return {
        'x_prompt': nrm((BATCH, SEQ, D_MODEL), 1.0),
        'x_sample': nrm((DEC_BATCH, DEC_SEQ, D_MODEL), 1.0),
        'cache_k': nrm((DEC_BATCH, N_EVEN, PAST_LEN, N_KV_HEADS, HEAD_DIM), 1.0),
        'cache_v': nrm((DEC_BATCH, N_EVEN, PAST_LEN, N_KV_HEADS, HEAD_DIM), 1.0),
        'state_lru_fwd': nrm((DEC_BATCH, N_EVEN, LRU_W), 0.5),
        'state_lru_bwd': nrm((DEC_BATCH, N_EVEN, LRU_W), 0.5),
        'c': nrm((DEC_BATCH, D_MODEL), 1.0),
        'c_ctx': nrm((D_MODEL,), 1.0),
        'norm_g': gain((DEPTH, 3, D_MODEL)),
        'w_mod': nrm((DEPTH, D_MODEL, N_MOD * D_MODEL), 0.5 * D_MODEL ** -0.5),
        'b_mod': nrm((DEPTH, N_MOD * D_MODEL), 0.01),
        'ffn_w13': nrm((DEPTH, 2, D_MODEL, 2 * D_FF), D_MODEL ** -0.5),
        'ffn_w2': nrm((DEPTH, 2, D_FF, D_MODEL), D_FF ** -0.5),
        'ab_w_in': nrm((N_EVEN, D_MODEL, IN_W_AB), D_MODEL ** -0.5),
        'ab_q_norm': gain((N_EVEN, HEAD_DIM)),
        'ab_k_norm': gain((N_EVEN, HEAD_DIM)),
        'lru_conv_w': nrm((N_EVEN, LRU_CONV, LRU_W), LRU_CONV ** -0.5),
        'lru_conv_b': nrm((N_EVEN, LRU_W), 0.01),
        'lru_gate_w': nrm((N_EVEN, 2, 2, LRU_HEADS, LRU_BLK, LRU_BLK), LRU_BLK ** -0.5),
        'lru_gate_b': nrm((N_EVEN, 2, 2, LRU_W), 0.01),
        'lru_lambda': lru_lambda,
        'ab_w_out': nrm((N_EVEN, MIX_W_AB, D_MODEL), MIX_W_AB ** -0.5),
        'cd_w_in': nrm((N_ODD, D_MODEL, IN_W_CD), D_MODEL ** -0.5),
        'hy_conv_w': nrm((N_ODD, HY_CONV, (HY_ORDER + 1) * HY_W), HY_CONV ** -0.5),
        'hy_conv_b': nrm((N_ODD, (HY_ORDER + 1) * HY_W), 0.01),
        'hy_w1': nrm((N_ODD, HY_EMB, HY_FH), HY_EMB ** -0.5),
        'hy_b1': nrm((N_ODD, HY_FH), 0.02),
        'hy_w2': nrm((N_ODD, HY_FH, HY_FH), HY_FH ** -0.5),
        'hy_b2': nrm((N_ODD, HY_FH), 0.02),
        'hy_w3': nrm((N_ODD, HY_FH, HY_ORDER * HY_W), HY_FH ** -0.5),
        'hy_freq': gain((N_ODD, 2, HY_FH)),
        'hy_decay': hy_decay,
        'hy_skip': nrm((N_ODD, HY_ORDER, HY_W), 0.5),
        'pool_w': nrm((N_ODD, POOL_GROUPS, POOL_GW, POOL_GW), POOL_GW ** -0.5),
        'pool_scale': gain((N_ODD, POOL_W)),
        'cd_w_out': nrm((N_ODD, MIX_W_CD, D_MODEL), MIX_W_CD ** -0.5),
    }


def reference(x_prompt, x_sample, cache_k, cache_v, state_lru_fwd, state_lru_bwd, c, c_ctx,
              norm_g, w_mod, b_mod, ffn_w13, ffn_w2,
              ab_w_in, ab_q_norm, ab_k_norm, lru_conv_w, lru_conv_b, lru_gate_w, lru_gate_b,
              lru_lambda, ab_w_out,
              cd_w_in, hy_conv_w, hy_conv_b, hy_w1, hy_b1, hy_w2, hy_b2, hy_w3, hy_freq,
              hy_decay, hy_skip, pool_w, pool_scale, cd_w_out):
    xp, xs = x_prompt, x_sample
    k_list, v_list, hf_list, hb_list = [], [], [], []
    for l in range(DEPTH):
        mp = modulation(c_ctx, w_mod[l], b_mod[l])[None, None]
        ms = modulation(c, w_mod[l], b_mod[l])[:, None]
        xp = ffn_sublayer(xp, mp, 0, norm_g[l, 0], ffn_w13[l, 0], ffn_w2[l, 0])
        xs = ffn_sublayer(xs, ms, 0, norm_g[l, 0], ffn_w13[l, 0], ffn_w2[l, 0])
        hp = modulate(xp, norm_g[l, 1], mp, 1)
        hs = modulate(xs, norm_g[l, 1], ms, 1)
        if l % 2 == 0:
            e = l // 2
            ab = (ab_w_in[e], ab_q_norm[e], ab_k_norm[e], lru_conv_w[e], lru_conv_b[e],
                  lru_gate_w[e], lru_gate_b[e], lru_lambda[e], ab_w_out[e])
            op, (kc, vc, hf, hb) = mixer_ab(hp, *ab, ctx=None)
            os_, _ = mixer_ab(hs, *ab, ctx=(cache_k[:, e], cache_v[:, e],
                                             state_lru_fwd[:, e], state_lru_bwd[:, e]))
            k_list.append(kc)
            v_list.append(vc)
            hf_list.append(hf)
            hb_list.append(hb)
        else:
            o = l // 2
            cd = (cd_w_in[o], hy_conv_w[o], hy_conv_b[o], hy_w1[o], hy_b1[o], hy_w2[o], hy_b2[o],
                  hy_w3[o], hy_freq[o], hy_decay[o], hy_skip[o], pool_w[o], pool_scale[o], cd_w_out[o])
            op = mixer_cd(hp, *cd)
            os_ = mixer_cd(hs, *cd)
        xp = xp + mp[..., 5, :] * op
        xs = xs + ms[..., 5, :] * os_
        xp = ffn_sublayer(xp, mp, 2, norm_g[l, 2], ffn_w13[l, 1], ffn_w2[l, 1])
        xs = ffn_sublayer(xs, ms, 2, norm_g[l, 2], ffn_w13[l, 1], ffn_w2[l, 1])
    new_cache_k = jnp.stack(k_list, axis=1)
    new_cache_v = jnp.stack(v_list, axis=1)
    new_state_lru_fwd = jnp.stack(hf_list, axis=1)
    new_state_lru_bwd = jnp.stack(hb_list, axis=1)
    return (xp, xs, new_cache_k, new_cache_v, new_state_lru_fwd, new_state_lru_bwd)
```

```python
import functools
import math

import numpy as np
import jax
import jax.numpy as jnp
from jax import lax
from jax.experimental import pallas as pl
from jax.experimental.pallas import tpu as pltpu

F32 = jnp.float32
BF16 = jnp.bfloat16

D_MODEL = 2048
BATCH = 32
SEQ = 256
DEPTH = 2
DEC_BATCH = 2
DEC_SEQ = 1024
PAST_LEN = 512
GRID_W = 64
N_MOD = 9
NORM_EPS = 1e-6
D_FF = 5632
HEAD_DIM = 128
N_Q_HEADS = 8
N_KV_HEADS = 2
Q_PER_KV = N_Q_HEADS // N_KV_HEADS
ATTN_W = N_Q_HEADS * HEAD_DIM
KV_W = N_KV_HEADS * HEAD_DIM
QKV_W = ATTN_W + 2 * KV_W
ROPE_THETA = 10000.0
LRU_W = 1024
LRU_HEADS = 8
LRU_BLK = LRU_W // LRU_HEADS
LRU_CONV = 4
LRU_CONV_LEFT = 2
LRU_C = 8.0
HY_W = 1024
HY_ORDER = 2
HY_CONV = 3
HY_CONV_LEFT = 1
HY_EMB = 33
HY_EMB_PAD = 128
HY_BANDS = (HY_EMB - 1) // 2
HY_FH = 64
POOL_W = 1024
POOL_WINDOWS = (2, 4, 8, 16)
POOL_GW = POOL_W // len(POOL_WINDOWS)

N_CTX = BATCH * SEQ
N_LAT = DEC_BATCH * DEC_SEQ
N_TOK = N_CTX + N_LAT
N_COND = 1 + DEC_BATCH
COND_PAD = 8

VMEM_LIMIT_BYTES = 56 * 1024 * 1024

ROW_TILE = 512
FF_TILE = 512
MOD_TILE = 1024


def _params(sem):
    return pltpu.CompilerParams(dimension_semantics=sem, vmem_limit_bytes=VMEM_LIMIT_BYTES)


def _cond_index(i, tile):
    return jnp.maximum((i * tile) // DEC_SEQ - (N_CTX // DEC_SEQ - 1), 0)


def _sigmoid(x):
    return 1.0 / (1.0 + jnp.exp(-x))


def _dot(a, b):
    return jnp.dot(a, b, preferred_element_type=F32)


def _dot_f32(a, b):
    return jnp.dot(a, b, preferred_element_type=F32, precision=lax.Precision.HIGHEST)


def _modulated_norm(x, g, mod_ref, j):
    ms = jnp.mean(x * x, axis=-1, keepdims=True)
    y = x * lax.rsqrt(ms + NORM_EPS) * g
    return y * (1.0 + mod_ref[0, 3 * j + 1:3 * j + 2, :]) + mod_ref[0, 3 * j:3 * j + 1, :]


def _head_norm(x, g):
    ms = jnp.mean(x * x, axis=-1, keepdims=True)
    return x * lax.rsqrt(ms + NORM_EPS) * g


def _shift_rows(x, off):
    n = x.shape[0]
    if off == 0:
        return x
    t = lax.broadcasted_iota(jnp.int32, x.shape, 0)
    y = pltpu.roll(x, (-off) % n, axis=0)
    return jnp.where(t >= -off if off < 0 else t < n - off, y, 0.0)


def _dwconv(x, w, b, left):
    acc = b + _shift_rows(x, -left) * w[0:1, :]
    for j in range(1, w.shape[0]):
        acc = acc + _shift_rows(x, j - left) * w[j:j + 1, :]
    return acc


def _mod_kernel(c_ref, w_ref, b_ref, o_ref):
    c = c_ref[...]
    s = (c * _sigmoid(c)).astype(BF16)
    o_ref[0] = _dot(s, w_ref[0].astype(BF16)) + b_ref[0]


def _modulation(cond, w_mod, b_mod):
    n = N_MOD * D_MODEL
    return pl.pallas_call(
        _mod_kernel,
        grid=(DEPTH, n // MOD_TILE),
        in_specs=[
            pl.BlockSpec((COND_PAD, D_MODEL), lambda l, j: (0, 0)),
            pl.BlockSpec((1, D_MODEL, MOD_TILE), lambda l, j: (l, 0, j)),
            pl.BlockSpec((1, 1, MOD_TILE), lambda l, j: (l, 0, j)),
        ],
        out_specs=pl.BlockSpec((1, COND_PAD, MOD_TILE), lambda l, j: (l, 0, j)),
        out_shape=jax.ShapeDtypeStruct((DEPTH, COND_PAD, n), F32),
        compiler_params=_params(("arbitrary", "arbitrary")),
        name="modulation",
    )(cond, w_mod, b_mod.reshape(DEPTH, 1, n))


def _ffn_kernel(x_ref, mod_ref, g_ref, w1_ref, w3_ref, w2_ref, o_ref, h_ref, acc_ref, *, j):
    f = pl.program_id(1)

    @pl.when(f == 0)
    def _():
        h_ref[...] = _modulated_norm(x_ref[...], g_ref[...], mod_ref, j).astype(BF16)
        acc_ref[...] = jnp.zeros_like(acc_ref)

    h = h_ref[...]
    gt = _dot(h, w1_ref[...])
    up = _dot(h, w3_ref[...])
    a = (gt * _sigmoid(gt) * up).astype(BF16)
    acc_ref[...] += _dot(a, w2_ref[...])

    @pl.when(f == pl.num_programs(1) - 1)
    def _():
        gate = mod_ref[0, 3 * j + 2:3 * j + 3, :]
        o_ref[...] = x_ref[...] + (0.5 * gate) * acc_ref[...]


def _ffn(x, mod, g, w13, w2, j):
    nf = D_FF // FF_TILE
    return pl.pallas_call(
        functools.partial(_ffn_kernel, j=j),
        grid=(N_TOK // ROW_TILE, nf),
        in_specs=[
            pl.BlockSpec((ROW_TILE, D_MODEL), lambda i, f: (i, 0)),
            pl.BlockSpec((1, N_MOD, D_MODEL), lambda i, f: (_cond_index(i, ROW_TILE), 0, 0)),
            pl.BlockSpec((1, D_MODEL), lambda i, f: (0, 0)),
            pl.BlockSpec((D_MODEL, FF_TILE), lambda i, f: (0, f)),
            pl.BlockSpec((D_MODEL, FF_TILE), lambda i, f: (0, nf + f)),
            pl.BlockSpec((FF_TILE, D_MODEL), lambda i, f: (f, 0)),
        ],
        out_specs=pl.BlockSpec((ROW_TILE, D_MODEL), lambda i, f: (i, 0)),
        out_shape=jax.ShapeDtypeStruct((N_TOK, D_MODEL), F32),
        scratch_shapes=[pltpu.VMEM((ROW_TILE, D_MODEL), BF16), pltpu.VMEM((ROW_TILE, D_MODEL), F32)],
        compiler_params=_params(("arbitrary", "arbitrary")),
        name="ffn",
    )(x, mod, g, w13, w13, w2)


IN_CHUNK = 512


def _inproj_kernel(x_ref, mod_ref, g_ref, w_ref, *o_refs, widths):
    h = _modulated_norm(x_ref[...], g_ref[...], mod_ref, 1).astype(BF16)
    off = 0
    for o_ref, wd in zip(o_refs, widths):
        for c0 in range(0, wd, IN_CHUNK):
            o_ref[:, c0:c0 + IN_CHUNK] = _dot(h, w_ref[:, off + c0:off + c0 + IN_CHUNK])
        off += wd


def _inproj(x, mod, g, w, widths):
    n = sum(widths)
    return pl.pallas_call(
        functools.partial(_inproj_kernel, widths=widths),
        grid=(N_TOK // ROW_TILE,),
        in_specs=[
            pl.BlockSpec((ROW_TILE, D_MODEL), lambda i: (i, 0)),
            pl.BlockSpec((1, N_MOD, D_MODEL), lambda i: (_cond_index(i, ROW_TILE), 0, 0)),
            pl.BlockSpec((1, D_MODEL), lambda i: (0, 0)),
            pl.BlockSpec((D_MODEL, n), lambda i: (0, 0), pipeline_mode=pl.Buffered(1)),
        ],
        out_specs=[pl.BlockSpec((ROW_TILE, wd), lambda i: (i, 0)) for wd in widths],
        out_shape=[jax.ShapeDtypeStruct((N_TOK, wd), F32) for wd in widths],
        compiler_params=_params(("arbitrary",)),
        name="mixer_in_proj",
    )(x, mod, g, w)


def _outproj_kernel(x_ref, mod_ref, a_ref, b_ref, w_ref, o_ref):
    wa = w_ref.shape[0] // 2
    y = _dot(a_ref[...], w_ref[0:wa, :]) + _dot(b_ref[...], w_ref[wa:, :])
    o_ref[...] = x_ref[...] + mod_ref[0, 5:6, :] * y


def _outproj(x, mod, a, b, w):
    wa = a.shape[1]
    return pl.pallas_call(
        _outproj_kernel,
        grid=(N_TOK // ROW_TILE,),
        in_specs=[
            pl.BlockSpec((ROW_TILE, D_MODEL), lambda i: (i, 0)),
            pl.BlockSpec((1, N_MOD, D_MODEL), lambda i: (_cond_index(i, ROW_TILE), 0, 0)),
            pl.BlockSpec((ROW_TILE, wa), lambda i: (i, 0)),
            pl.BlockSpec((ROW_TILE, wa), lambda i: (i, 0)),
            pl.BlockSpec((2 * wa, D_MODEL), lambda i: (0, 0), pipeline_mode=pl.Buffered(1)),
        ],
        out_specs=pl.BlockSpec((ROW_TILE, D_MODEL), lambda i: (i, 0)),
        out_shape=jax.ShapeDtypeStruct((N_TOK, D_MODEL), F32),
        compiler_params=_params(("arbitrary",)),
        name="mixer_out_proj",
    )(x, mod, a, b, w)


def _rope(x, cos, sin):
    lane = lax.broadcasted_iota(jnp.int32, x.shape, 1)
    swapped = jnp.where((lane & 1) == 0,pltpu.roll(x, HEAD_DIM - 1, axis=1), pltpu.roll(x, 1, axis=1))
    return x * cos + swapped * sin


def _attend(q_heads, k, v, o_ref, col0):
    t = q_heads[0].shape[0]
    q = jnp.concatenate(q_heads, axis=0)
    s = lax.dot_general(q, k, (((1,), (1,)), ((), ())), preferred_element_type=F32) * (HEAD_DIM ** -0.5)
    e = jnp.exp(s - jnp.max(s, axis=-1, keepdims=True))
    o = _dot(e.astype(BF16), v) / jnp.sum(e, axis=-1, keepdims=True)
    for i in range(len(q_heads)):
        o_ref[:, col0 + i * HEAD_DIM:col0 + (i + 1) * HEAD_DIM] = o[i * t:(i + 1) * t].astype(o_ref.dtype)


def _attn_ctx_kernel(qkv_ref, qn_ref, kn_ref, o_ref, kc_ref, vc_ref):
    for kv in range(N_KV_HEADS):
        kcol = ATTN_W + kv * HEAD_DIM
        vcol = ATTN_W + KV_W + kv * HEAD_DIM
        k = _head_norm(qkv_ref[:, kcol:kcol + HEAD_DIM], kn_ref[...])
        v = qkv_ref[:, vcol:vcol + HEAD_DIM]
        kc_ref[:, kv * HEAD_DIM:(kv + 1) * HEAD_DIM] = k
        vc_ref[:, kv * HEAD_DIM:(kv + 1) * HEAD_DIM] = v
        qs = []
        for i in range(Q_PER_KV):
            qcol = (kv * Q_PER_KV + i) * HEAD_DIM
            qs.append(_head_norm(qkv_ref[:, qcol:qcol + HEAD_DIM], qn_ref[...]).astype(BF16))
        _attend(qs, k.astype(BF16), v.astype(BF16), o_ref, kv * Q_PER_KV * HEAD_DIM)


def _attn_ctx(qkv, qn, kn):
    return pl.pallas_call(
        _attn_ctx_kernel,
        grid=(BATCH,),
        in_specs=[
            pl.BlockSpec((SEQ, QKV_W), lambda b: (b, 0)),
            pl.BlockSpec((1, HEAD_DIM), lambda b: (0, 0)),
            pl.BlockSpec((1, HEAD_DIM), lambda b: (0, 0)),
        ],
        out_specs=[
            pl.BlockSpec((SEQ, ATTN_W), lambda b: (b, 0)),
            pl.BlockSpec((SEQ, KV_W), lambda b: (b, 0)),
            pl.BlockSpec((SEQ, KV_W), lambda b: (b, 0)),
        ],
        out_shape=[
            jax.ShapeDtypeStruct((N_TOK, ATTN_W), BF16),
            jax.ShapeDtypeStruct((N_CTX, KV_W), F32),
            jax.ShapeDtypeStruct((N_CTX, KV_W), F32),
        ],
        compiler_params=_params(("arbitrary",)),
        name="attention_context",
    )(qkv, qn, kn)


Q_TILE = 256


def _attn_lat_kernel(qkv_ref, ck_ref, cv_ref, qn_ref, kn_ref, cos_ref, sin_ref, prev_ref, o_ref, k_scr, v_scr):
    del prev_ref
    qi = pl.program_id(1)

    @pl.when(qi == 0)
    def _():
        k_scr[0:PAST_LEN, :] = ck_ref[0].astype(BF16)
        v_scr[0:PAST_LEN, :] = cv_ref[0].astype(BF16)
        for kv in range(N_KV_HEADS):
            kcol = ATTN_W + kv * HEAD_DIM
            vcol = ATTN_W + KV_W + kv * HEAD_DIM
            k = _head_norm(qkv_ref[:, kcol:kcol + HEAD_DIM], kn_ref[...])
            k = _rope(k, cos_ref[...], sin_ref[...])
            k_scr[PAST_LEN:, kv * HEAD_DIM:(kv + 1) * HEAD_DIM] = k.astype(BF16)
            v_scr[PAST_LEN:, kv * HEAD_DIM:(kv + 1) * HEAD_DIM] = qkv_ref[:, vcol:vcol + HEAD_DIM].astype(BF16)

    r0 = pl.multiple_of(qi * Q_TILE, Q_TILE)
    cos = cos_ref[pl.ds(r0, Q_TILE), :]
    sin = sin_ref[pl.ds(r0, Q_TILE), :]
    for kv in range(N_KV_HEADS):
        qs = []
        for i in range(Q_PER_KV):
            qcol = (kv * Q_PER_KV + i) * HEAD_DIM
            q = _head_norm(qkv_ref[pl.ds(r0, Q_TILE), qcol:qcol + HEAD_DIM], qn_ref[...])
            qs.append(_rope(q, cos, sin).astype(BF16))
        _attend(qs, k_scr[:, kv * HEAD_DIM:(kv + 1) * HEAD_DIM], v_scr[:, kv * HEAD_DIM:(kv + 1) * HEAD_DIM],
                o_ref, kv * Q_PER_KV * HEAD_DIM)


def _attn_lat(qkv, cache_k, cache_v, qn, kn, cos, sin, attn_prev):
    row0 = N_CTX // DEC_SEQ
    qt = DEC_SEQ // Q_TILE
    return pl.pallas_call(
        _attn_lat_kernel,
        grid=(DEC_BATCH, qt),
        in_specs=[
            pl.BlockSpec((DEC_SEQ, QKV_W), lambda b, q: (row0 + b, 0)),
            pl.BlockSpec((1, PAST_LEN, KV_W), lambda b, q: (b, 0, 0)),
            pl.BlockSpec((1, PAST_LEN, KV_W), lambda b, q: (b, 0, 0)),
            pl.BlockSpec((1, HEAD_DIM), lambda b, q: (0, 0)),
            pl.BlockSpec((1, HEAD_DIM), lambda b, q: (0, 0)),
            pl.BlockSpec((DEC_SEQ, HEAD_DIM), lambda b, q: (0, 0)),
            pl.BlockSpec((DEC_SEQ, HEAD_DIM), lambda b, q: (0, 0)),
            pl.BlockSpec(memory_space=pl.ANY),
        ],
        out_specs=pl.BlockSpec((Q_TILE, ATTN_W), lambda b, q: ((row0 + b) * qt + q, 0)),
        out_shape=jax.ShapeDtypeStruct((N_TOK, ATTN_W), BF16),
        scratch_shapes=[pltpu.VMEM((PAST_LEN + DEC_SEQ, KV_W), BF16), pltpu.VMEM((PAST_LEN + DEC_SEQ, KV_W), BF16)],
        input_output_aliases={7: 0},
        compiler_params=_params(("arbitrary", "arbitrary")),
        name="attention_latent",
    )(qkv, cache_k, cache_v, qn, kn, cos, sin, attn_prev)


def _lru_kernel(l_ref, cw_ref, cb_ref, gw_ref, gb_ref, lam_ref, h0f_ref, h0b_ref, *rest, aliased):
    if aliased:
        rest = rest[1:]
    o_ref, hf_ref, hb_ref, pre_scr, af_scr, bf_scr, ab_scr, bb_scr, yf_scr, yb_scr = rest
    n = l_ref.shape[0]
    xc = _dwconv(l_ref[:, 0:LRU_W], cw_ref[...], cb_ref[...], LRU_CONV_LEFT)
    xcb = xc.astype(BF16)
    for hd in range(LRU_HEADS):
        pre_scr[:, hd * 4 * LRU_BLK:(hd + 1) * 4 * LRU_BLK] = _dot(xcb[:, hd * LRU_BLK:(hd + 1) * LRU_BLK], gw_ref[hd])
    for d, (a_scr, b_scr) in enumerate(((af_scr, bf_scr), (ab_scr, bb_scr))):
        z = -lam_ref[d:d + 1, :]
        softplus = jnp.maximum(z, 0.0) + jnp.log1p(jnp.exp(-jnp.abs(z)))
        for hd in range(LRU_HEADS):
            cols = slice(hd * LRU_BLK, (hd + 1) * LRU_BLK)
            base = hd * 4 * LRU_BLK + d * 2 * LRU_BLK
            r = _sigmoid(pre_scr[:, base:base + LRU_BLK] + gb_ref[2 * d:2 * d + 1, cols])
            i = _sigmoid(pre_scr[:, base + LRU_BLK:base + 2 * LRU_BLK] + gb_ref[2 * d + 1:2 * d + 2, cols])
            log_a = (-LRU_C) * r * softplus[:, cols]
            a = jnp.exp(log_a)
            a_scr[:, cols] = a
            b_scr[:, cols] = jnp.sqrt(jnp.tanh(-log_a) * (1.0 + a * a)) * (i * xc[:, cols])

    def step(t, carry):
        hf, hb = carry
        tb = n - 1 - t
        hf = af_scr[pl.ds(t, 1), :] * hf + bf_scr[pl.ds(t, 1), :]
        hb = ab_scr[pl.ds(tb, 1), :] * hb + bb_scr[pl.ds(tb, 1), :]
        yf_scr[pl.ds(t, 1), :] = hf
        yb_scr[pl.ds(tb, 1), :] = hb
        return hf, hb

    hf, hb = lax.fori_loop(0, n, step, (h0f_ref[0], h0b_ref[0]), unroll=8)
    hf_ref[0] = hf
    hb_ref[0] = hb
    lg = l_ref[:, LRU_W:]
    gelu = 0.5 * lg * (1.0 + jnp.tanh(math.sqrt(2.0 / math.pi) * (lg + 0.044715 * (lg * lg * lg))))
    o_ref[...] = (gelu * (yf_scr[...] + yb_scr[...])).astype(o_ref.dtype)


def _lru(l_all, cw, cb, gw, gb, lam, h0f, h0b, seq_len, n_seq, row0, prev):
    blk0 = row0 // seq_len
    aliased = prev is not None
    vec = lambda b: (0, 0)
    in_specs = [
        pl.BlockSpec((seq_len, 2 * LRU_W), lambda b: (blk0 + b, 0)),
        pl.BlockSpec((LRU_CONV, LRU_W), vec),
        pl.BlockSpec((1, LRU_W), vec),
        pl.BlockSpec((LRU_HEADS, LRU_BLK, 4 * LRU_BLK), lambda b: (0, 0, 0)),
        pl.BlockSpec((4, LRU_W), vec),
        pl.BlockSpec((2, LRU_W), vec),
        pl.BlockSpec((1, 1, LRU_W), lambda b: (b, 0, 0)),
        pl.BlockSpec((1, 1, LRU_W), lambda b: (b, 0, 0)),
    ]
    args = [l_all, cw, cb, gw, gb, lam, h0f, h0b]
    if aliased:
        in_specs.append(pl.BlockSpec(memory_space=pl.ANY))
        args.append(prev)
    return pl.pallas_call(
        functools.partial(_lru_kernel, aliased=aliased),
        grid=(n_seq,),
        in_specs=in_specs,
        out_specs=[
            pl.BlockSpec((seq_len, LRU_W), lambda b: (blk0 + b, 0)),
            pl.BlockSpec((1, 1, LRU_W), lambda b: (b, 0, 0)),
            pl.BlockSpec((1, 1, LRU_W), lambda b: (b, 0, 0)),
        ],
        out_shape=[
            jax.ShapeDtypeStruct((N_TOK, LRU_W), BF16),
            jax.ShapeDtypeStruct((n_seq, 1, LRU_W), F32),
            jax.ShapeDtypeStruct((n_seq, 1, LRU_W), F32),
        ],
        scratch_shapes=[pltpu.VMEM((seq_len, 4 * LRU_W), F32)] + [pltpu.VMEM((seq_len, LRU_W), F32)] * 6,
        input_output_aliases={8: 0} if aliased else {},
        compiler_params=_params(("arbitrary",)),
        name="rg_lru",
    )(*args)


def _hy_filter_kernel(z_ref, t_ref, ck_ref, alt_ref, w1_ref, b1_ref, w2_ref, b2_ref, w3_ref, fr_ref, dec_ref,
                      kf_ref, kn_ref):
    z = jnp.sin(fr_ref[0:1, :] * (_dot_f32(z_ref[...], w1_ref[...]) + b1_ref[...]))
    z = jnp.sin(fr_ref[1:2, :] * (_dot_f32(z, w2_ref[...]) + b2_ref[...]))
    filt = _dot_f32(z, w3_ref[...]) * jnp.exp(-t_ref[...] * jnp.abs(dec_ref[...]))
    filt = filt / jnp.sum(jnp.abs(filt), axis=0, keepdims=True)
    kf = _dot_f32(ck_ref[...], filt)
    kn = jnp.sum(alt_ref[...] * filt, axis=0, keepdims=True)
    for o in range(HY_ORDER):
        kf_ref[o] = kf[:, o * HY_W:(o + 1) * HY_W]
        kn_ref[o:o + 1, :] = kn[:, o * HY_W:(o + 1) * HY_W]


def _hy_filter(consts, w1, b1, w2, b2, w3, freq, decay, seq_len):
    return pl.pallas_call(
        _hy_filter_kernel,
        out_shape=[jax.ShapeDtypeStruct((HY_ORDER, seq_len, HY_W), F32), jax.ShapeDtypeStruct((HY_ORDER, HY_W), F32)],
        compiler_params=pltpu.CompilerParams(vmem_limit_bytes=VMEM_LIMIT_BYTES),
        name="hyena_filter",
    )(consts["z"], consts["t"], consts["ck"], consts["alt_w"], w1, b1, w2, b2, w3, freq, decay)


def _longconv(u, fwd_ref, inv_ref, kf2, kn, alt, skip):
    n = u.shape[0]
    spec = _dot(fwd_ref[...], u.astype(BF16)) * kf2
    y = _dot(inv_ref[...], spec.astype(BF16))
    nyq = jnp.sum(alt * u, axis=0, keepdims=True) * kn * (1.0 / (2 * n))
    return y + alt * nyq + u * skip


def _hyena_kernel(x1_ref, x2_ref, v_ref, cw_ref, cb_ref, kf_ref, kn_ref, skip_ref, fwd_ref, inv_ref, alt_ref, *rest):
    o_ref = rest[-1]
    alt = alt_ref[...]
    z = _dwconv(v_ref[...], cw_ref[2], cb_ref[2:3, :], HY_CONV_LEFT)
    for o, g_ref in enumerate((x1_ref, x2_ref)):
        gate = _dwconv(g_ref[...], cw_ref[o], cb_ref[o:o + 1, :], HY_CONV_LEFT)
        kf2 = jnp.concatenate([kf_ref[o], kf_ref[o]], axis=0)
        z = gate * _longconv(z, fwd_ref, inv_ref, kf2, kn_ref[o:o + 1, :], alt, skip_ref[o:o + 1, :])
    o_ref[...] = z.astype(o_ref.dtype)


def _hyena(hy_all, cw, cb, kf, kn, skip, consts, seq_len, n_seq, row0, cw_chunk, prev):
    blk0 = row0 // seq_len
    nc = HY_W // cw_chunk
    in_specs = [pl.BlockSpec((seq_len, cw_chunk), lambda b, c, g=g: (blk0 + b, g * nc + c)) for g in range(3)]
    in_specs += [
        pl.BlockSpec((HY_ORDER + 1, HY_CONV, cw_chunk), lambda b, c: (0, 0, c)),
        pl.BlockSpec((HY_ORDER + 1, cw_chunk), lambda b, c: (0, c)),
        pl.BlockSpec((HY_ORDER, seq_len, cw_chunk), lambda b, c: (0, 0, c)),
        pl.BlockSpec((HY_ORDER, cw_chunk), lambda b, c: (0, c)),
        pl.BlockSpec((HY_ORDER, cw_chunk), lambda b, c: (0, c)),
        pl.BlockSpec((2 * seq_len, seq_len), lambda b, c: (0, 0)),
        pl.BlockSpec((seq_len, 2 * seq_len), lambda b, c: (0, 0)),
        pl.BlockSpec((seq_len, 1), lambda b, c: (0, 0)),
    ]
    args = [hy_all, hy_all, hy_all, cw.reshape(HY_CONV, HY_ORDER + 1, HY_W).transpose(1, 0, 2), cb.reshape(HY_ORDER + 1, HY_W),
            kf, kn, skip, consts["fwd"].astype(BF16), consts["inv"].astype(BF16), consts["alt"]]
    aliases = {}
    if prev is not None:
        aliases = {len(args): 0}
        in_specs.append(pl.BlockSpec(memory_space=pl.ANY))
        args.append(prev)
    return pl.pallas_call(
        _hyena_kernel,
        grid=(n_seq, nc),
        in_specs=in_specs,
        out_specs=pl.BlockSpec((seq_len, cw_chunk), lambda b, c: (blk0 + b, c)),
        out_shape=jax.ShapeDtypeStruct((N_TOK, HY_W), BF16),
        input_output_aliases=aliases,
        compiler_params=_params(("arbitrary", "arbitrary")),
        name="hyena",
    )(*args)


def _pool_kernel(p_ref, w_ref, s_ref, *rest):
    o_ref = rest[-1]
    n = p_ref.shape[0]
    t = lax.broadcasted_iota(jnp.int32, (n, 1), 0)
    for gi, win in enumerate(POOL_WINDOWS):
        cols = slice(gi * POOL_GW, (gi + 1) * POOL_GW)
        x = p_ref[:, cols]
        half = win // 2
        back, fwd = x, x
        m = 1
        while m < half:
            back = back + _shift_rows(back, -m)
            fwd = fwd + _shift_rows(fwd, m)
            m *= 2
        s = _shift_rows(back, -1) + fwd
        cnt =(jnp.minimum(t + half, n) - jnp.maximum(t - half, 0)).astype(F32)
        y = _dot((s / cnt - x).astype(BF16), w_ref[gi])
        o_ref[:, cols] = (y * s_ref[:, cols]).astype(o_ref.dtype)


def _pool(p_all, w, scale, seq_len, n_seq, row0, prev):
    blk0 = row0 // seq_len
    aliased = prev is not None
    in_specs = [
        pl.BlockSpec((seq_len, POOL_W), lambda b: (blk0 + b, 0)),
        pl.BlockSpec((len(POOL_WINDOWS), POOL_GW, POOL_GW), lambda b: (0, 0, 0)),
        pl.BlockSpec((1, POOL_W), lambda b: (0, 0)),
    ]
    args = [p_all, w, scale]
    if aliased:
        in_specs.append(pl.BlockSpec(memory_space=pl.ANY))
        args.append(prev)
    return pl.pallas_call(
        _pool_kernel,
        grid=(n_seq,),
        in_specs=in_specs,
        out_specs=pl.BlockSpec((seq_len, POOL_W), lambda b: (blk0 + b, 0)),
        out_shape=jax.ShapeDtypeStruct((N_TOK, POOL_W), BF16),
        input_output_aliases={3: 0} if aliased else {},
        compiler_params=_params(("arbitrary",)),
        name="multi_pool",
    )(*args)


def _rope_tables():
    rows = DEC_SEQ // GRID_W
    r = np.repeat(np.arange(rows), GRID_W).astype(np.float64)
    col = np.tile(np.arange(GRID_W), rows).astype(np.float64)
    half = HEAD_DIM // 2
    inv = ROPE_THETA ** (-np.arange(0, half, 2, dtype=np.float64) / half)
    ang = np.concatenate([r[:, None] * inv, col[:, None] * inv], axis=-1)
    cos = np.repeat(np.cos(ang), 2, axis=-1)
    sin = np.repeat(np.sin(ang), 2, axis=-1) * np.tile(np.array([-1.0, 1.0]), half)
    return jnp.asarray(cos, F32), jnp.asarray(sin, F32)


def _hyena_tables(n):
    idx = np.arange(n, dtype=np.float64)
    t = idx / max(n - 1, 1)
    bands = np.linspace(1e-4, HY_BANDS - 1, HY_BANDS)
    f = 2.0 * math.pi * idx[:, None] * bands[None, :] / n
    z = np.zeros((n, HY_EMB_PAD))
    z[:, 0] = t
    z[:, 1:1 + HY_BANDS] = np.cos(f)
    z[:, 1 + HY_BANDS:HY_EMB] = -np.sin(f)
    ang = math.pi * ((idx[:, None] * idx[None, :]) % (2 * n)) / n
    wgt_n = np.where(idx == 0, 1.0, 2.0)
    alt = np.where(idx % 2 == 0, 1.0, -1.0)
    fwd = np.concatenate([np.cos(ang), np.sin(ang)], axis=0)
    inv = np.concatenate([np.cos(ang) * wgt_n[None, :], np.sin(ang) * wgt_n[None, :]], axis=1) / (2 * n)
    return {
        "z": jnp.asarray(z, F32),
        "t": jnp.asarray(t[:, None], F32),
        "ck": jnp.asarray(np.cos(ang) * wgt_n[None, :], F32),
        "alt_w": jnp.asarray((alt * wgt_n)[:, None], F32),
        "alt": jnp.asarray(alt[:, None], F32),
        "fwd": jnp.asarray(fwd, F32),
        "inv": jnp.asarray(inv, F32),
    }


def kernel(x_prompt, x_sample, cache_k, cache_v, state_lru_fwd, state_lru_bwd, c, c_ctx, norm_g, w_mod, b_mod, ffn_w13, ffn_w2, ab_w_in, ab_q_norm, ab_k_norm, lru_conv_w, lru_conv_b, lru_gate_w, lru_gate_b, lru_lambda, ab_w_out, cd_w_in, hy_conv_w, hy_conv_b, hy_w1, hy_b1, hy_w2, hy_b2, hy_w3, hy_freq, hy_decay, hy_skip, pool_w, pool_scale, cd_w_out):
    x = jnp.concatenate([x_prompt.reshape(N_CTX, D_MODEL), x_sample.reshape(N_LAT, D_MODEL)], axis=0)
    cond = jnp.concatenate([c_ctx[None], c, jnp.zeros((COND_PAD - N_COND, D_MODEL), F32)], axis=0)
    mods = _modulation(cond, w_mod, b_mod)[:, :N_COND].reshape(DEPTH, N_COND, N_MOD, D_MODEL)
    rope_cos, rope_sin = _rope_tables()

    k_list, v_list, hf_list, hb_list = [], [], [], []
    for l in range(DEPTH):
        mod = mods[l]
        g = norm_g[l]
        x = _ffn(x, mod, g[0:1], ffn_w13[l, 0].astype(BF16), ffn_w2[l, 0].astype(BF16), 0)
        if l % 2 == 0:
            e = l // 2
            qkv, lxg = _inproj(x, mod, g[1:2], ab_w_in[e].astype(BF16), (QKV_W, 2 * LRU_W))
            qn, kn = ab_q_norm[e][None], ab_k_norm[e][None]
            attn, kc, vc = _attn_ctx(qkv, qn, kn)
            attn = _attn_lat(qkv, cache_k[:, e].reshape(DEC_BATCH, PAST_LEN, KV_W),
                             cache_v[:, e].reshape(DEC_BATCH, PAST_LEN, KV_W), qn, kn, rope_cos, rope_sin, attn)
            gw = jnp.transpose(lru_gate_w[e], (2, 3, 0, 1, 4)).reshape(LRU_HEADS, LRU_BLK, 4 * LRU_BLK).astype(BF16)
            gb = lru_gate_b[e].reshape(4, LRU_W)
            lru_args = (lru_conv_w[e], lru_conv_b[e][None], gw, gb, lru_lambda[e])
            zeros = jnp.zeros((BATCH, 1, LRU_W), F32)
            rec, hf, hb = _lru(lxg, *lru_args, zeros, zeros, SEQ, BATCH, 0, None)
            rec, _, _ = _lru(lxg, *lru_args, state_lru_fwd[:, e][:, None], state_lru_bwd[:, e][:, None],
                             DEC_SEQ, DEC_BATCH, N_CTX, rec)
            x = _outproj(x, mod, attn, rec, ab_w_out[e].astype(BF16))
            k_list.append(kc.reshape(BATCH, SEQ, N_KV_HEADS, HEAD_DIM))
            v_list.append(vc.reshape(BATCH, SEQ, N_KV_HEADS, HEAD_DIM))
            hf_list.append(hf.reshape(BATCH, LRU_W))
            hb_list.append(hb.reshape(BATCH, LRU_W))
        else:
            o = l // 2
            hy, pw = _inproj(x, mod, g[1:2], cd_w_in[o].astype(BF16), ((HY_ORDER + 1) * HY_W, POOL_W))
            w1 = jnp.zeros((HY_EMB_PAD, HY_FH), F32).at[:HY_EMB].set(hy_w1[o])
            z_out, p_out = None, None
            for seq_len, n_seq, row0, chunk in ((SEQ, BATCH, 0, HY_W), (DEC_SEQ, DEC_BATCH, N_CTX, 256)):
                consts = _hyena_tables(seq_len)
                kf, kn = _hy_filter(consts, w1, hy_b1[o][None], hy_w2[o], hy_b2[o][None], hy_w3[o], hy_freq[o],
                                    hy_decay[o][None], seq_len)
                z_out = _hyena(hy, hy_conv_w[o], hy_conv_b[o], kf, kn, hy_skip[o], consts,
                               seq_len, n_seq, row0, chunk, z_out)
                p_out = _pool(pw, pool_w[o].astype(BF16), pool_scale[o][None], seq_len, n_seq, row0, p_out)
            x = _outproj(x, mod, z_out, p_out, cd_w_out[o].astype(BF16))
        x = _ffn(x, mod, g[2:3], ffn_w13[l, 1].astype(BF16), ffn_w2[l, 1].astype(BF16), 2)

    y_prompt = x[:N_CTX].reshape(BATCH, SEQ, D_MODEL)
    y_sample = x[N_CTX:].reshape(DEC_BATCH, DEC_SEQ, D_MODEL)
    return (y_prompt, y_sample, jnp.stack(k_list, axis=1), jnp.stack(v_list, axis=1),
            jnp.stack(hf_list, axis=1), jnp.stack(hb_list, axis=1))
```

```python
import functools
import math

import numpy as np
import jax
import jax.numpy as jnp
from jax import lax
from jax.experimental import pallas as pl
from jax.experimental.pallas import tpu as pltpu

F32 = jnp.float32
BF16 = jnp.bfloat16

D_MODEL = 2048
BATCH = 32
SEQ = 256
DEPTH = 2
DEC_BATCH = 2
DEC_SEQ = 1024
PAST_LEN = 512
GRID_W = 64
N_MOD = 9
NORM_EPS = 1e-6
D_FF = 5632
HEAD_DIM = 128
N_Q_HEADS = 8
N_KV_HEADS = 2
Q_PER_KV = N_Q_HEADS // N_KV_HEADS
ATTN_W = N_Q_HEADS * HEAD_DIM
KV_W = N_KV_HEADS * HEAD_DIM
QKV_W = ATTN_W + 2 * KV_W
ROPE_THETA = 10000.0
LRU_W = 1024
LRU_HEADS = 8
LRU_BLK = LRU_W // LRU_HEADS
LRU_CONV = 4
LRU_CONV_LEFT = 2
LRU_C = 8.0
HY_W = 1024
HY_ORDER = 2
HY_CONV = 3
HY_CONV_LEFT = 1
HY_EMB = 33
HY_EMB_PAD = 128
HY_BANDS = (HY_EMB - 1) // 2
HY_FH = 64
POOL_W = 1024
POOL_WINDOWS = (2, 4, 8, 16)
POOL_GW = POOL_W // len(POOL_WINDOWS)

N_CTX = BATCH * SEQ
N_LAT = DEC_BATCH * DEC_SEQ
N_TOK = N_CTX + N_LAT
N_COND = 1 + DEC_BATCH
COND_PAD = 8

VMEM_LIMIT_BYTES = 56 * 1024 * 1024

ROW_TILE = 512
FFN_ROW_TILE = 1024
FF_TILE = 256
FFN_NORM_ROWS = 32
FFN_RES_ROWS = 128
MOD_TILE = 1024


def _params(sem):
    return pltpu.CompilerParams(dimension_semantics=sem, vmem_limit_bytes=VMEM_LIMIT_BYTES)


def _cond_index(i, tile):
    return jnp.maximum((i * tile) // DEC_SEQ - (N_CTX // DEC_SEQ - 1), 0)


def _sigmoid(x):
    return 1.0 / (1.0 + jnp.exp(-x))


def _dot(a, b):
    return jnp.dot(a, b, preferred_element_type=F32)


def _dot_f32(a, b):
    return jnp.dot(a, b, preferred_element_type=F32, precision=lax.Precision.HIGHEST)


def _modulated_norm(x, g, mod_ref, j):
    ms = jnp.mean(x * x, axis=-1, keepdims=True)
    y = x * lax.rsqrt(ms + NORM_EPS) * g
    return y * (1.0 + mod_ref[0, 3 * j + 1:3 * j + 2, :]) + mod_ref[0, 3 * j:3 * j + 1, :]


def _head_norm(x, g):
    ms = jnp.mean(x * x, axis=-1, keepdims=True)
    return x * lax.rsqrt(ms + NORM_EPS) * g


def _shift_rows(x, off):
    n = x.shape[0]
    if off == 0:
        return x
    t = lax.broadcasted_iota(jnp.int32, x.shape, 0)
    y = pltpu.roll(x, (-off) % n, axis=0)
    return jnp.where(t >= -off if off < 0 else t < n - off, y, 0.0)


def _dwconv(x, w, b, left):
    acc = b + _shift_rows(x, -left) * w[0:1, :]
    for j in range(1, w.shape[0]):
        acc = acc + _shift_rows(x, j - left) * w[j:j + 1, :]
    return acc


def _mod_kernel(c_ref, w_ref, b_ref, o_ref):
    c = c_ref[...]
    s = (c * _sigmoid(c)).astype(BF16)
    o_ref[0] = _dot(s, w_ref[0].astype(BF16)) + b_ref[0]


def _modulation(cond, w_mod, b_mod):
    n = N_MOD * D_MODEL
    return pl.pallas_call(
        _mod_kernel,
        grid=(DEPTH, n // MOD_TILE),
        in_specs=[
            pl.BlockSpec((COND_PAD, D_MODEL), lambda l, j: (0, 0)),
            pl.BlockSpec((1, D_MODEL, MOD_TILE), lambda l, j: (l, 0, j)),
            pl.BlockSpec((1, 1, MOD_TILE), lambda l, j: (l, 0, j)),
        ],
        out_specs=pl.BlockSpec((1, COND_PAD, MOD_TILE), lambda l, j: (l, 0, j)),
        out_shape=jax.ShapeDtypeStruct((DEPTH, COND_PAD, n), F32),
        compiler_params=_params(("arbitrary", "arbitrary")),
        name="modulation",
    )(cond, w_mod, b_mod.reshape(DEPTH, 1, n))


def _ffn_kernel(x_ref, mod_ref, g_ref, w1_ref, w3_ref, w2_ref, o_ref, h_ref, *, j):
    f = pl.program_id(1)

    @pl.when(f == 0)
    def _():
        gain = g_ref[...] * (1.0 + mod_ref[0, 3 * j + 1:3 * j + 2, :])
        shift = mod_ref[0, 3 * j:3 * j + 1, :]

        def chunk(r, carry):
            rows = pl.ds(pl.multiple_of(r * FFN_NORM_ROWS, FFN_NORM_ROWS), FFN_NORM_ROWS)
            x = x_ref[rows, :]
            ms = jnp.mean(x * x, axis=-1, keepdims=True)
            h_ref[rows, :] = (x * lax.rsqrt(ms + NORM_EPS) * gain + shift).astype(BF16)
            o_ref[rows, :] = jnp.zeros((FFN_NORM_ROWS, D_MODEL), F32)
            return carry

        lax.fori_loop(0, FFN_ROW_TILE // FFN_NORM_ROWS, chunk, 0, unroll=4)

    h = h_ref[...]
    gt = _dot(h, w1_ref[...].astype(BF16))
    up = _dot(h, w3_ref[...].astype(BF16))
    a = (gt * _sigmoid(gt) * up).astype(BF16)
    o_ref[...] += _dot(a, w2_ref[...].astype(BF16))

    @pl.when(f == pl.num_programs(1) - 1)
    def _():
        gate = 0.5 * mod_ref[0, 3 * j + 2:3 * j + 3, :]

        def chunk(r, carry):
            rows = pl.ds(pl.multiple_of(r * FFN_RES_ROWS, FFN_RES_ROWS), FFN_RES_ROWS)
            o_ref[rows, :] = x_ref[rows, :] + gate * o_ref[rows, :]
            return carry

        lax.fori_loop(0, FFN_ROW_TILE // FFN_RES_ROWS, chunk, 0)


def _ffn(x, mod, g, w13, w2, l, sub, j):
    nf = D_FF // FF_TILE
    return pl.pallas_call(
        functools.partial(_ffn_kernel, j=j),
        grid=(N_TOK // FFN_ROW_TILE, nf),
        in_specs=[
            pl.BlockSpec((FFN_ROW_TILE, D_MODEL), lambda i, f: (i, 0)),
            pl.BlockSpec((1, N_MOD, D_MODEL), lambda i, f: (_cond_index(i, FFN_ROW_TILE), 0, 0)),
            pl.BlockSpec((1, D_MODEL), lambda i, f: (0, 0)),
            pl.BlockSpec((None, None, D_MODEL, FF_TILE), lambda i, f: (l, sub, 0, f)),
            pl.BlockSpec((None, None, D_MODEL, FF_TILE), lambda i, f: (l, sub, 0, nf + f)),
            pl.BlockSpec((None, None, FF_TILE, D_MODEL), lambda i, f: (l, sub, f, 0)),
        ],
        out_specs=pl.BlockSpec((FFN_ROW_TILE, D_MODEL), lambda i, f: (i, 0)),
        out_shape=jax.ShapeDtypeStruct((N_TOK, D_MODEL), F32),
        scratch_shapes=[pltpu.VMEM((FFN_ROW_TILE, D_MODEL), BF16)],
        compiler_params=_params(("arbitrary", "arbitrary")),
        name="ffn",
    )(x, mod, g, w13, w13, w2)


IN_CHUNK = 512


def _inproj_kernel(x_ref, mod_ref, g_ref, w_ref, *o_refs, widths):
    h = _modulated_norm(x_ref[...], g_ref[...], mod_ref, 1).astype(BF16)
    off = 0
    for o_ref, wd in zip(o_refs, widths):
        for c0 in range(0, wd, IN_CHUNK):
            o_ref[:, c0:c0 + IN_CHUNK] = _dot(h, w_ref[:, off + c0:off + c0 + IN_CHUNK])
        off += wd


def _inproj(x, mod, g, w, widths):
    n = sum(widths)
    return pl.pallas_call(
        functools.partial(_inproj_kernel, widths=widths),
        grid=(N_TOK // ROW_TILE,),
        in_specs=[
            pl.BlockSpec((ROW_TILE, D_MODEL), lambda i: (i, 0)),
            pl.BlockSpec((1, N_MOD, D_MODEL), lambda i: (_cond_index(i, ROW_TILE), 0, 0)),
            pl.BlockSpec((1, D_MODEL), lambda i: (0, 0)),
            pl.BlockSpec((D_MODEL, n), lambda i: (0, 0), pipeline_mode=pl.Buffered(1)),
        ],
        out_specs=[pl.BlockSpec((ROW_TILE, wd), lambda i: (i, 0)) for wd in widths],
        out_shape=[jax.ShapeDtypeStruct((N_TOK, wd), F32) for wd in widths],
        compiler_params=_params(("arbitrary",)),
        name="mixer_in_proj",
    )(x, mod, g, w)


def _outproj_kernel(x_ref, mod_ref, a_ref, b_ref, w_ref, o_ref):
    wa = w_ref.shape[0] // 2
    y = _dot(a_ref[...], w_ref[0:wa, :]) + _dot(b_ref[...], w_ref[wa:, :])
    o_ref[...] = x_ref[...] + mod_ref[0, 5:6, :] * y


def _outproj(x, mod, a, b, w):
    wa = a.shape[1]
    return pl.pallas_call(
        _outproj_kernel,
        grid=(N_TOK // ROW_TILE,),
        in_specs=[
            pl.BlockSpec((ROW_TILE, D_MODEL), lambda i: (i, 0)),
            pl.BlockSpec((1, N_MOD, D_MODEL), lambda i: (_cond_index(i, ROW_TILE), 0, 0)),
            pl.BlockSpec((ROW_TILE, wa), lambda i: (i, 0)),
            pl.BlockSpec((ROW_TILE, wa), lambda i: (i, 0)),
            pl.BlockSpec((2 * wa, D_MODEL), lambda i: (0, 0), pipeline_mode=pl.Buffered(1)),
        ],
        out_specs=pl.BlockSpec((ROW_TILE, D_MODEL), lambda i: (i, 0)),
        out_shape=jax.ShapeDtypeStruct((N_TOK, D_MODEL), F32),
        compiler_params=_params(("arbitrary",)),
        name="mixer_out_proj",
    )(x, mod, a, b, w)


def _rope(x, cos, sin):
    lane = lax.broadcasted_iota(jnp.int32, x.shape, 1)
    swapped = jnp.where((lane & 1) == 0,pltpu.roll(x, HEAD_DIM - 1, axis=1), pltpu.roll(x, 1, axis=1))
    return x * cos + swapped * sin


def _attend(q_heads, k, v, o_ref, col0):
    t = q_heads[0].shape[0]
    q = jnp.concatenate(q_heads, axis=0)
    s = lax.dot_general(q, k, (((1,), (1,)), ((), ())), preferred_element_type=F32) * (HEAD_DIM ** -0.5)
    e = jnp.exp(s - jnp.max(s, axis=-1, keepdims=True))
    o = _dot(e.astype(BF16), v) / jnp.sum(e, axis=-1, keepdims=True)
    for i in range(len(q_heads)):
        o_ref[:, col0 + i * HEAD_DIM:col0 + (i + 1) * HEAD_DIM] = o[i * t:(i + 1) * t].astype(o_ref.dtype)


def _attn_ctx_kernel(qkv_ref, qn_ref, kn_ref, o_ref, kc_ref, vc_ref):
    for kv in range(N_KV_HEADS):
        kcol = ATTN_W + kv * HEAD_DIM
        vcol = ATTN_W + KV_W + kv * HEAD_DIM
        k = _head_norm(qkv_ref[:, kcol:kcol + HEAD_DIM], kn_ref[...])
        v = qkv_ref[:, vcol:vcol + HEAD_DIM]
        kc_ref[:, kv * HEAD_DIM:(kv + 1) * HEAD_DIM] = k
        vc_ref[:, kv * HEAD_DIM:(kv + 1) * HEAD_DIM] = v
        qs = []
        for i in range(Q_PER_KV):
            qcol = (kv * Q_PER_KV + i) * HEAD_DIM
            qs.append(_head_norm(qkv_ref[:, qcol:qcol + HEAD_DIM], qn_ref[...]).astype(BF16))
        _attend(qs, k.astype(BF16), v.astype(BF16), o_ref, kv * Q_PER_KV * HEAD_DIM)


def _attn_ctx(qkv, qn, kn):
    return pl.pallas_call(
        _attn_ctx_kernel,
        grid=(BATCH,),
        in_specs=[
            pl.BlockSpec((SEQ, QKV_W), lambda b: (b, 0)),
            pl.BlockSpec((1, HEAD_DIM), lambda b: (0, 0)),
            pl.BlockSpec((1, HEAD_DIM), lambda b: (0, 0)),
        ],
        out_specs=[
            pl.BlockSpec((SEQ, ATTN_W), lambda b: (b, 0)),
            pl.BlockSpec((SEQ, KV_W), lambda b: (b, 0)),
            pl.BlockSpec((SEQ, KV_W), lambda b: (b, 0)),
        ],
        out_shape=[
            jax.ShapeDtypeStruct((N_TOK, ATTN_W), BF16),
            jax.ShapeDtypeStruct((N_CTX, KV_W), F32),
            jax.ShapeDtypeStruct((N_CTX, KV_W), F32),
        ],
        compiler_params=_params(("arbitrary",)),
        name="attention_context",
    )(qkv, qn, kn)


Q_TILE = 256


def _attn_lat_kernel(qkv_ref, ck_ref, cv_ref, qn_ref, kn_ref, cos_ref, sin_ref, prev_ref, o_ref, k_scr, v_scr):
    del prev_ref
    qi = pl.program_id(1)

    @pl.when(qi == 0)
    def _():
        k_scr[0:PAST_LEN, :] = ck_ref[0].astype(BF16)
        v_scr[0:PAST_LEN, :] = cv_ref[0].astype(BF16)
        for kv in range(N_KV_HEADS):
            kcol = ATTN_W + kv * HEAD_DIM
            vcol = ATTN_W + KV_W + kv * HEAD_DIM
            k = _head_norm(qkv_ref[:, kcol:kcol + HEAD_DIM], kn_ref[...])
            k = _rope(k, cos_ref[...], sin_ref[...])
            k_scr[PAST_LEN:, kv * HEAD_DIM:(kv + 1) * HEAD_DIM] = k.astype(BF16)
            v_scr[PAST_LEN:, kv * HEAD_DIM:(kv + 1) * HEAD_DIM] = qkv_ref[:, vcol:vcol + HEAD_DIM].astype(BF16)

    r0 = pl.multiple_of(qi * Q_TILE, Q_TILE)
    cos = cos_ref[pl.ds(r0, Q_TILE), :]
    sin = sin_ref[pl.ds(r0, Q_TILE), :]
    for kv in range(N_KV_HEADS):
        qs = []
        for i in range(Q_PER_KV):
            qcol = (kv * Q_PER_KV + i) * HEAD_DIM
            q = _head_norm(qkv_ref[pl.ds(r0, Q_TILE), qcol:qcol + HEAD_DIM], qn_ref[...])
            qs.append(_rope(q, cos, sin).astype(BF16))
        _attend(qs, k_scr[:, kv * HEAD_DIM:(kv + 1) * HEAD_DIM], v_scr[:, kv * HEAD_DIM:(kv + 1) * HEAD_DIM],
                o_ref, kv * Q_PER_KV * HEAD_DIM)


def _attn_lat(qkv, cache_k, cache_v, qn, kn, cos, sin, attn_prev):
    row0 = N_CTX // DEC_SEQ
    qt = DEC_SEQ // Q_TILE
    return pl.pallas_call(
        _attn_lat_kernel,
        grid=(DEC_BATCH, qt),
        in_specs=[
            pl.BlockSpec((DEC_SEQ, QKV_W), lambda b, q: (row0 + b, 0)),
            pl.BlockSpec((1, PAST_LEN, KV_W), lambda b, q: (b, 0, 0)),
            pl.BlockSpec((1, PAST_LEN, KV_W), lambda b, q: (b, 0, 0)),
            pl.BlockSpec((1, HEAD_DIM), lambda b, q: (0, 0)),
            pl.BlockSpec((1, HEAD_DIM), lambda b, q: (0, 0)),
            pl.BlockSpec((DEC_SEQ, HEAD_DIM), lambda b, q: (0, 0)),
            pl.BlockSpec((DEC_SEQ, HEAD_DIM), lambda b, q: (0, 0)),
            pl.BlockSpec(memory_space=pl.ANY),
        ],
        out_specs=pl.BlockSpec((Q_TILE, ATTN_W), lambda b, q: ((row0 + b) * qt + q, 0)),
        out_shape=jax.ShapeDtypeStruct((N_TOK, ATTN_W), BF16),
        scratch_shapes=[pltpu.VMEM((PAST_LEN + DEC_SEQ, KV_W), BF16), pltpu.VMEM((PAST_LEN + DEC_SEQ, KV_W), BF16)],
        input_output_aliases={7: 0},
        compiler_params=_params(("arbitrary", "arbitrary")),
        name="attention_latent",
    )(qkv, cache_k, cache_v, qn, kn, cos, sin, attn_prev)


def _lru_kernel(l_ref, cw_ref, cb_ref, gw_ref, gb_ref, lam_ref, h0f_ref, h0b_ref, *rest, aliased):
    if aliased:
        rest = rest[1:]
    o_ref, hf_ref, hb_ref, pre_scr, af_scr, bf_scr, ab_scr, bb_scr, yf_scr, yb_scr = rest
    n = l_ref.shape[0]
    xc = _dwconv(l_ref[:, 0:LRU_W], cw_ref[...], cb_ref[...], LRU_CONV_LEFT)
    xcb = xc.astype(BF16)
    for hd in range(LRU_HEADS):
        pre_scr[:, hd * 4 * LRU_BLK:(hd + 1) * 4 * LRU_BLK] = _dot(xcb[:, hd * LRU_BLK:(hd + 1) * LRU_BLK], gw_ref[hd])
    for d, (a_scr, b_scr) in enumerate(((af_scr, bf_scr), (ab_scr, bb_scr))):
        z = -lam_ref[d:d + 1, :]
        softplus = jnp.maximum(z, 0.0) + jnp.log1p(jnp.exp(-jnp.abs(z)))
        for hd in range(LRU_HEADS):
            cols = slice(hd * LRU_BLK, (hd + 1) * LRU_BLK)
            base = hd * 4 * LRU_BLK + d * 2 * LRU_BLK
            r = _sigmoid(pre_scr[:, base:base + LRU_BLK] + gb_ref[2 * d:2 * d + 1, cols])
            i = _sigmoid(pre_scr[:, base + LRU_BLK:base + 2 * LRU_BLK] + gb_ref[2 * d + 1:2 * d + 2, cols])
            log_a = (-LRU_C) * r * softplus[:, cols]
            a = jnp.exp(log_a)
            a_scr[:, cols] = a
            b_scr[:, cols] = jnp.sqrt(jnp.tanh(-log_a) * (1.0 + a * a)) * (i * xc[:, cols])

    def step(t, carry):
        hf, hb = carry
        tb = n - 1 - t
        hf = af_scr[pl.ds(t, 1), :] * hf + bf_scr[pl.ds(t, 1), :]
        hb = ab_scr[pl.ds(tb, 1), :] * hb + bb_scr[pl.ds(tb, 1), :]
        yf_scr[pl.ds(t, 1), :] = hf
        yb_scr[pl.ds(tb, 1), :] = hb
        return hf, hb

    hf, hb = lax.fori_loop(0, n, step, (h0f_ref[0], h0b_ref[0]), unroll=8)
    hf_ref[0] = hf
    hb_ref[0] = hb
    lg = l_ref[:, LRU_W:]
    gelu = 0.5 * lg * (1.0 + jnp.tanh(math.sqrt(2.0 / math.pi) * (lg + 0.044715 * (lg * lg * lg))))
    o_ref[...] = (gelu * (yf_scr[...] + yb_scr[...])).astype(o_ref.dtype)


def _lru(l_all, cw, cb, gw, gb, lam, h0f, h0b, seq_len, n_seq, row0, prev):
    blk0 = row0 // seq_len
    aliased = prev is not None
    vec = lambda b: (0, 0)
    in_specs = [
        pl.BlockSpec((seq_len, 2 * LRU_W), lambda b: (blk0 + b, 0)),
        pl.BlockSpec((LRU_CONV, LRU_W), vec),
        pl.BlockSpec((1, LRU_W), vec),
        pl.BlockSpec((LRU_HEADS, LRU_BLK, 4 * LRU_BLK), lambda b: (0, 0, 0)),
        pl.BlockSpec((4, LRU_W), vec),
        pl.BlockSpec((2, LRU_W), vec),
        pl.BlockSpec((1, 1, LRU_W), lambda b: (b, 0, 0)),
        pl.BlockSpec((1, 1, LRU_W), lambda b: (b, 0, 0)),
    ]
    args = [l_all, cw, cb, gw, gb, lam, h0f, h0b]
    if aliased:
        in_specs.append(pl.BlockSpec(memory_space=pl.ANY))
        args.append(prev)
    return pl.pallas_call(
        functools.partial(_lru_kernel, aliased=aliased),
        grid=(n_seq,),
        in_specs=in_specs,
        out_specs=[
            pl.BlockSpec((seq_len, LRU_W), lambda b: (blk0 + b, 0)),
            pl.BlockSpec((1, 1, LRU_W), lambda b: (b, 0, 0)),
            pl.BlockSpec((1, 1, LRU_W), lambda b: (b, 0, 0)),
        ],
        out_shape=[
            jax.ShapeDtypeStruct((N_TOK, LRU_W), BF16),
            jax.ShapeDtypeStruct((n_seq, 1, LRU_W), F32),
            jax.ShapeDtypeStruct((n_seq, 1, LRU_W), F32),
        ],
        scratch_shapes=[pltpu.VMEM((seq_len, 4 * LRU_W), F32)] + [pltpu.VMEM((seq_len, LRU_W), F32)] * 6,
        input_output_aliases={8: 0} if aliased else {},
        compiler_params=_params(("arbitrary",)),
        name="rg_lru",
    )(*args)


def _hy_filter_kernel(z_ref, t_ref, ck_ref, alt_ref, w1_ref, b1_ref, w2_ref, b2_ref, w3_ref, fr_ref, dec_ref,
                      kf_ref, kn_ref):
    z = jnp.sin(fr_ref[0:1, :] * (_dot_f32(z_ref[...], w1_ref[...]) + b1_ref[...]))
    z = jnp.sin(fr_ref[1:2, :] * (_dot_f32(z, w2_ref[...]) + b2_ref[...]))
    filt = _dot_f32(z, w3_ref[...]) * jnp.exp(-t_ref[...] * jnp.abs(dec_ref[...]))
    filt = filt / jnp.sum(jnp.abs(filt), axis=0, keepdims=True)
    kf = _dot_f32(ck_ref[...], filt)
    kn = jnp.sum(alt_ref[...] * filt, axis=0, keepdims=True)
    for o in range(HY_ORDER):
        kf_ref[o] = kf[:, o * HY_W:(o + 1) * HY_W]
        kn_ref[o:o + 1, :] = kn[:, o * HY_W:(o + 1) * HY_W]


def _hy_filter(consts, w1, b1, w2, b2, w3, freq, decay, seq_len):
    return pl.pallas_call(
        _hy_filter_kernel,
        out_shape=[jax.ShapeDtypeStruct((HY_ORDER, seq_len, HY_W), F32), jax.ShapeDtypeStruct((HY_ORDER, HY_W), F32)],
        compiler_params=pltpu.CompilerParams(vmem_limit_bytes=VMEM_LIMIT_BYTES),
        name="hyena_filter",
    )(consts["z"], consts["t"], consts["ck"], consts["alt_w"], w1, b1, w2, b2, w3, freq, decay)


def _longconv(u, fwd_ref, inv_ref, kf2, kn, alt, skip):
    n = u.shape[0]
    spec = _dot(fwd_ref[...], u.astype(BF16)) * kf2
    y = _dot(inv_ref[...], spec.astype(BF16))
    nyq = jnp.sum(alt * u, axis=0, keepdims=True) * kn * (1.0 / (2 * n))
    return y + alt * nyq + u * skip


def _hyena_kernel(x1_ref, x2_ref, v_ref, cw_ref, cb_ref, kf_ref, kn_ref, skip_ref, fwd_ref, inv_ref, alt_ref, *rest):
    o_ref = rest[-1]
    alt = alt_ref[...]
    z = _dwconv(v_ref[...], cw_ref[2], cb_ref[2:3, :], HY_CONV_LEFT)
    for o, g_ref in enumerate((x1_ref, x2_ref)):
        gate = _dwconv(g_ref[...], cw_ref[o], cb_ref[o:o + 1, :], HY_CONV_LEFT)
        kf2 = jnp.concatenate([kf_ref[o], kf_ref[o]], axis=0)
        z = gate * _longconv(z, fwd_ref, inv_ref, kf2, kn_ref[o:o + 1, :], alt, skip_ref[o:o + 1, :])
    o_ref[...] = z.astype(o_ref.dtype)


def _hyena(hy_all, cw, cb, kf, kn, skip, consts, seq_len, n_seq, row0, cw_chunk, prev):
    blk0 = row0 // seq_len
    nc = HY_W // cw_chunk
    in_specs = [pl.BlockSpec((seq_len, cw_chunk), lambda b, c, g=g: (blk0 + b, g * nc + c)) for g in range(3)]
    in_specs += [
        pl.BlockSpec((HY_ORDER + 1, HY_CONV, cw_chunk), lambda b, c: (0, 0, c)),
        pl.BlockSpec((HY_ORDER + 1, cw_chunk), lambda b, c: (0, c)),
        pl.BlockSpec((HY_ORDER, seq_len, cw_chunk), lambda b, c: (0, 0, c)),
        pl.BlockSpec((HY_ORDER, cw_chunk), lambda b, c: (0, c)),
        pl.BlockSpec((HY_ORDER, cw_chunk), lambda b, c: (0, c)),
        pl.BlockSpec((2 * seq_len, seq_len), lambda b, c: (0, 0)),
        pl.BlockSpec((seq_len, 2 * seq_len), lambda b, c: (0, 0)),
        pl.BlockSpec((seq_len, 1), lambda b, c: (0, 0)),
    ]
    args = [hy_all, hy_all, hy_all, cw.reshape(HY_CONV, HY_ORDER + 1, HY_W).transpose(1, 0, 2), cb.reshape(HY_ORDER + 1, HY_W),
            kf, kn, skip, consts["fwd"].astype(BF16), consts["inv"].astype(BF16), consts["alt"]]
    aliases = {}
    if prev is not None:
        aliases = {len(args): 0}
        in_specs.append(pl.BlockSpec(memory_space=pl.ANY))
        args.append(prev)
    return pl.pallas_call(
        _hyena_kernel,
        grid=(n_seq, nc),
        in_specs=in_specs,
        out_specs=pl.BlockSpec((seq_len, cw_chunk), lambda b, c: (blk0 + b, c)),
        out_shape=jax.ShapeDtypeStruct((N_TOK, HY_W), BF16),
        input_output_aliases=aliases,
        compiler_params=_params(("arbitrary", "arbitrary")),
        name="hyena",
    )(*args)


def _pool_kernel(p_ref, w_ref, s_ref, *rest):
    o_ref = rest[-1]
    n = p_ref.shape[0]
    t = lax.broadcasted_iota(jnp.int32, (n, 1), 0)
    for gi, win in enumerate(POOL_WINDOWS):
        cols = slice(gi * POOL_GW, (gi + 1) * POOL_GW)
        x = p_ref[:, cols]
        half = win // 2
        back, fwd = x, x
        m = 1
        while m < half:
            back = back + _shift_rows(back, -m)
            fwd = fwd + _shift_rows(fwd, m)
            m *= 2
        s = _shift_rows(back, -1) + fwd
        cnt =(jnp.minimum(t + half, n) - jnp.maximum(t - half, 0)).astype(F32)
        y = _dot((s / cnt - x).astype(BF16), w_ref[gi])
        o_ref[:, cols] = (y * s_ref[:, cols]).astype(o_ref.dtype)


def _pool(p_all, w, scale, seq_len, n_seq, row0, prev):
    blk0 = row0 // seq_len
    aliased = prev is not None
    in_specs = [
        pl.BlockSpec((seq_len, POOL_W), lambda b: (blk0 + b, 0)),
        pl.BlockSpec((len(POOL_WINDOWS), POOL_GW, POOL_GW), lambda b: (0, 0, 0)),
        pl.BlockSpec((1, POOL_W), lambda b: (0, 0)),
    ]
    args = [p_all, w, scale]
    if aliased:
        in_specs.append(pl.BlockSpec(memory_space=pl.ANY))
        args.append(prev)
    return pl.pallas_call(
        _pool_kernel,
        grid=(n_seq,),
        in_specs=in_specs,
        out_specs=pl.BlockSpec((seq_len, POOL_W), lambda b: (blk0 + b, 0)),
        out_shape=jax.ShapeDtypeStruct((N_TOK, POOL_W), BF16),
        input_output_aliases={3: 0} if aliased else {},
        compiler_params=_params(("arbitrary",)),
        name="multi_pool",
    )(*args)


def _rope_tables():
    rows = DEC_SEQ // GRID_W
    r = np.repeat(np.arange(rows), GRID_W).astype(np.float64)
    col = np.tile(np.arange(GRID_W), rows).astype(np.float64)
    half = HEAD_DIM // 2
    inv = ROPE_THETA ** (-np.arange(0, half, 2, dtype=np.float64) / half)
    ang = np.concatenate([r[:, None] * inv, col[:, None] * inv], axis=-1)
    cos = np.repeat(np.cos(ang), 2, axis=-1)
    sin = np.repeat(np.sin(ang), 2, axis=-1) * np.tile(np.array([-1.0, 1.0]), half)
    return jnp.asarray(cos, F32), jnp.asarray(sin, F32)


def _hyena_tables(n):
    idx = np.arange(n, dtype=np.float64)
    t = idx / max(n - 1, 1)
    bands = np.linspace(1e-4, HY_BANDS - 1, HY_BANDS)
    f = 2.0 * math.pi * idx[:, None] * bands[None, :] / n
    z = np.zeros((n, HY_EMB_PAD))
    z[:, 0] = t
    z[:, 1:1 + HY_BANDS] = np.cos(f)
    z[:, 1 + HY_BANDS:HY_EMB] = -np.sin(f)
    ang = math.pi * ((idx[:, None] * idx[None, :]) % (2 * n)) / n
    wgt_n = np.where(idx == 0, 1.0, 2.0)
    alt = np.where(idx % 2 == 0, 1.0, -1.0)
    fwd = np.concatenate([np.cos(ang), np.sin(ang)], axis=0)
    inv = np.concatenate([np.cos(ang) * wgt_n[None, :], np.sin(ang) * wgt_n[None, :]], axis=1) / (2 * n)
    return {
        "z": jnp.asarray(z, F32),
        "t": jnp.asarray(t[:, None], F32),
        "ck": jnp.asarray(np.cos(ang) * wgt_n[None, :], F32),
        "alt_w": jnp.asarray((alt * wgt_n)[:, None], F32),
        "alt": jnp.asarray(alt[:, None], F32),
        "fwd": jnp.asarray(fwd, F32),
        "inv": jnp.asarray(inv, F32),
    }


def kernel(x_prompt, x_sample, cache_k, cache_v, state_lru_fwd, state_lru_bwd, c, c_ctx, norm_g, w_mod, b_mod, ffn_w13, ffn_w2, ab_w_in, ab_q_norm, ab_k_norm, lru_conv_w, lru_conv_b, lru_gate_w, lru_gate_b, lru_lambda, ab_w_out, cd_w_in, hy_conv_w, hy_conv_b, hy_w1, hy_b1, hy_w2, hy_b2, hy_w3, hy_freq, hy_decay, hy_skip, pool_w, pool_scale, cd_w_out):
    x = jnp.concatenate([x_prompt.reshape(N_CTX, D_MODEL), x_sample.reshape(N_LAT, D_MODEL)], axis=0)
    cond = jnp.concatenate([c_ctx[None], c, jnp.zeros((COND_PAD - N_COND, D_MODEL), F32)], axis=0)
    mods = _modulation(cond, w_mod, b_mod)[:, :N_COND].reshape(DEPTH, N_COND, N_MOD, D_MODEL)
    rope_cos, rope_sin = _rope_tables()

    k_list, v_list, hf_list, hb_list = [], [], [], []
    for l in range(DEPTH):
        mod = mods[l]
        g = norm_g[l]
        x = _ffn(x, mod, g[0:1], ffn_w13, ffn_w2, l, 0, 0)
        if l % 2 == 0:
            e = l // 2
            qkv, lxg = _inproj(x, mod, g[1:2], ab_w_in[e].astype(BF16), (QKV_W, 2 * LRU_W))
            qn, kn = ab_q_norm[e][None], ab_k_norm[e][None]
            attn, kc, vc = _attn_ctx(qkv, qn, kn)
            attn = _attn_lat(qkv, cache_k[:, e].reshape(DEC_BATCH, PAST_LEN, KV_W),
                             cache_v[:, e].reshape(DEC_BATCH, PAST_LEN, KV_W), qn, kn, rope_cos, rope_sin, attn)
            gw = jnp.transpose(lru_gate_w[e], (2, 3, 0, 1, 4)).reshape(LRU_HEADS, LRU_BLK, 4 * LRU_BLK).astype(BF16)
            gb = lru_gate_b[e].reshape(4, LRU_W)
            lru_args = (lru_conv_w[e], lru_conv_b[e][None], gw, gb, lru_lambda[e])
            zeros = jnp.zeros((BATCH, 1, LRU_W), F32)
            rec, hf, hb = _lru(lxg, *lru_args, zeros, zeros, SEQ, BATCH, 0, None)
            rec, _, _ = _lru(lxg, *lru_args, state_lru_fwd[:, e][:, None], state_lru_bwd[:, e][:, None],
                             DEC_SEQ, DEC_BATCH, N_CTX, rec)
            x = _outproj(x, mod, attn, rec, ab_w_out[e].astype(BF16))
            k_list.append(kc.reshape(BATCH, SEQ, N_KV_HEADS, HEAD_DIM))
            v_list.append(vc.reshape(BATCH, SEQ, N_KV_HEADS, HEAD_DIM))
            hf_list.append(hf.reshape(BATCH, LRU_W))
            hb_list.append(hb.reshape(BATCH, LRU_W))
        else:
            o = l // 2
            hy, pw = _inproj(x, mod, g[1:2], cd_w_in[o].astype(BF16), ((HY_ORDER + 1) * HY_W, POOL_W))
            w1 = jnp.zeros((HY_EMB_PAD, HY_FH), F32).at[:HY_EMB].set(hy_w1[o])
            z_out, p_out = None, None
            for seq_len, n_seq, row0, chunk in ((SEQ, BATCH, 0, HY_W), (DEC_SEQ, DEC_BATCH, N_CTX, 256)):
                consts = _hyena_tables(seq_len)
                kf, kn = _hy_filter(consts, w1, hy_b1[o][None], hy_w2[o], hy_b2[o][None], hy_w3[o], hy_freq[o],
                                    hy_decay[o][None], seq_len)
                z_out = _hyena(hy, hy_conv_w[o], hy_conv_b[o], kf, kn, hy_skip[o], consts,
                               seq_len, n_seq, row0, chunk, z_out)
                p_out = _pool(pw, pool_w[o].astype(BF16), pool_scale[o][None], seq_len, n_seq, row0, p_out)
            x = _outproj(x, mod, z_out, p_out, cd_w_out[o].astype(BF16))
        x = _ffn(x, mod, g[2:3], ffn_w13, ffn_w2, l, 1, 2)

    y_prompt = x[:N_CTX].reshape(BATCH, SEQ, D_MODEL)
    y_sample = x[N_CTX:].reshape(DEC_BATCH, DEC_SEQ, D_MODEL)
    return (y_prompt, y_sample, jnp.stack(k_list, axis=1), jnp.stack(v_list, axis=1),
            jnp.stack(hf_list, axis=1), jnp.stack(hb_list, axis=1))
```

```python
import functools
import math

import numpy as np
import jax
import jax.numpy as jnp
from jax import lax
from jax.experimental import pallas as pl
from jax.experimental.pallas import tpu as pltpu

F32 = jnp.float32
BF16 = jnp.bfloat16

D_MODEL = 2048
BATCH = 32
SEQ = 256
DEPTH = 2
DEC_BATCH = 2
DEC_SEQ = 1024
PAST_LEN = 512
GRID_W = 64
N_MOD = 9
NORM_EPS = 1e-6
D_FF = 5632
HEAD_DIM = 128
N_Q_HEADS = 8
N_KV_HEADS = 2
Q_PER_KV = N_Q_HEADS // N_KV_HEADS
ATTN_W = N_Q_HEADS * HEAD_DIM
KV_W = N_KV_HEADS * HEAD_DIM
QKV_W = ATTN_W + 2 * KV_W
ROPE_THETA = 10000.0
LRU_W = 1024
LRU_HEADS = 8
LRU_BLK = LRU_W // LRU_HEADS
LRU_CONV = 4
LRU_CONV_LEFT = 2
LRU_C = 8.0
HY_W = 1024
HY_ORDER = 2
HY_CONV = 3
HY_CONV_LEFT = 1
HY_EMB = 33
HY_EMB_PAD = 128
HY_BANDS = (HY_EMB - 1) // 2
HY_FH = 64
POOL_W = 1024
POOL_WINDOWS = (2, 4, 8, 16)
POOL_GW = POOL_W // len(POOL_WINDOWS)

N_CTX = BATCH * SEQ
N_LAT = DEC_BATCH * DEC_SEQ
N_TOK = N_CTX + N_LAT
N_COND = 1 + DEC_BATCH
COND_PAD = 8

VMEM_LIMIT_BYTES = 56 * 1024 * 1024

ROW_TILE = 512
FFN_ROW_TILE = 1024
FF_TILE = 512
FFN_NORM_ROWS = 32
MOD_TILE = 1024
Q_TILE = 256
ATTN_CTX_GROUP = 2
LRU_CTX_GROUP = 2
HY_CTX_GROUP = 2
POOL_CTX_GROUP = 4
HY_CHAIN_W = 256
HY_LAT_BLOCK_W = 512


def _params(sem):
    return pltpu.CompilerParams(dimension_semantics=sem, vmem_limit_bytes=VMEM_LIMIT_BYTES)


def _cond_index(i, tile):
    return jnp.maximum((i * tile) // DEC_SEQ - (N_CTX // DEC_SEQ - 1), 0)


def _sigmoid(x):
    return 1.0 / (1.0 + jnp.exp(-x))


def _dot(a, b):
    return jnp.dot(a, b, preferred_element_type=F32)


def _dot_f32(a, b):
    return jnp.dot(a, b, preferred_element_type=F32, precision=lax.Precision.HIGHEST)


def _split_bf16(x):
    hi = x.astype(BF16)
    return hi, (x - hi.astype(F32)).astype(BF16)


def _modulated_norm(x, g, mod_ref, j):
    ms = jnp.mean(x * x, axis=-1, keepdims=True)
    y = x * lax.rsqrt(ms + NORM_EPS) * g
    return y * (1.0 + mod_ref[0, 3 * j + 1:3 * j + 2, :]) + mod_ref[0, 3 * j:3 * j + 1, :]


def _head_norm(x, g):
    ms = jnp.mean(x * x, axis=-1, keepdims=True)
    return x * lax.rsqrt(ms + NORM_EPS) * g


class _RowShifter:
    def __init__(self, period):
        self.period = period
        self._masks = {}

    def __call__(self, x, off):
        if off == 0:
            return x
        key = (x.shape, off)
        if key not in self._masks:
            t = lax.broadcasted_iota(jnp.int32, x.shape, 0) & (self.period - 1)
            self._masks[key] = t >= -off if off < 0 else t < self.period - off
        return jnp.where(self._masks[key], pltpu.roll(x, (-off) % x.shape[0], axis=0), 0.0)


def _dwconv(x, w, b, left, shift):
    acc = b + shift(x, -left) * w[0:1, :]
    for j in range(1, w.shape[0]):
        acc = acc + shift(x, j - left) * w[j:j + 1, :]
    return acc


def _mod_kernel(c_ref, w_ref, b_ref, o_ref):
    c = c_ref[...]
    s = (c * _sigmoid(c)).astype(BF16)
    o_ref[0] = _dot(s, w_ref[0].astype(BF16)) + b_ref[0]


def _modulation(cond, w_mod, b_mod):
    n = N_MOD * D_MODEL
    return pl.pallas_call(
        _mod_kernel,
        grid=(DEPTH, n // MOD_TILE),
        in_specs=[
            pl.BlockSpec((COND_PAD, D_MODEL), lambda l, j: (0, 0)),
            pl.BlockSpec((1, D_MODEL, MOD_TILE), lambda l, j: (l, 0, j)),
            pl.BlockSpec((1, 1, MOD_TILE), lambda l, j: (l, 0, j)),
        ],
        out_specs=pl.BlockSpec((1, COND_PAD, MOD_TILE), lambda l, j: (l, 0, j)),
        out_shape=jax.ShapeDtypeStruct((DEPTH, COND_PAD, n), F32),
        compiler_params=_params(("arbitrary", "arbitrary")),
        name="modulation",
    )(cond, w_mod, b_mod.reshape(DEPTH, 1, n))


def _ffn_kernel(*refs, j, n_in, n_out, in_split, out_split):
    x_hbm = refs[:n_in]
    mod_ref, g_ref, w1_ref, w3_ref, w2_ref = refs[n_in:n_in + 5]
    o_hbm = refs[n_in + 5:n_in + 5 + n_out]
    buf, h_ref, sem_in, sem_out = refs[n_in + 5 + n_out:]
    i, f = pl.program_id(0), pl.program_id(1)
    n, nf = pl.num_programs(0), pl.num_programs(1)
    slot = lax.rem(i, 2)
    acc = buf.at[slot]

    def tile_rows(t):
        return pl.ds(pl.multiple_of(t * FFN_ROW_TILE, FFN_ROW_TILE), FFN_ROW_TILE)

    def in_copy(src, t, s):
        return pltpu.make_async_copy(src.at[tile_rows(t), :], buf.at[s], sem_in.at[s])

    def out_copy(dst, t, s):
        return pltpu.make_async_copy(buf.at[s], dst.at[tile_rows(t), :], sem_out.at[s])

    def start_split(copy, refs_, split, t, s):
        if len(refs_) == 1:
            copy(refs_[0], t, s).start()
        else:
            @pl.when(t < split)
            def _():
                copy(refs_[0], t, s).start()

            @pl.when(t >= split)
            def _():
                copy(refs_[1], t - split, s).start()

    @pl.when(f == 0)
    def _():
        @pl.when(i == 0)
        def _():
            start_split(in_copy, x_hbm, in_split, i, slot)

        in_copy(x_hbm[0], 0, slot).wait()
        gain = g_ref[...] * (1.0 + mod_ref[0, 3 * j + 1:3 * j + 2, :])
        shift = mod_ref[0, 3 * j:3 * j + 1, :]

        def chunk(r, carry):
            rows = pl.ds(pl.multiple_of(r * FFN_NORM_ROWS, FFN_NORM_ROWS), FFN_NORM_ROWS)
            x = acc[rows, :]
            ms = jnp.mean(x * x, axis=-1, keepdims=True)
            h_ref[rows, :] = (x * lax.rsqrt(ms + NORM_EPS) * gain + shift).astype(BF16)
            return carry

        lax.fori_loop(0, FFN_ROW_TILE // FFN_NORM_ROWS, chunk, 0, unroll=4)

    @pl.when((f == 1) & (i + 1 < n))
    def _():
        @pl.when(i >= 1)
        def _():
            out_copy(o_hbm[0], 0, 1 - slot).wait()

        start_split(in_copy, x_hbm, in_split, i + 1, 1 - slot)

    h = h_ref[...]
    gt = _dot(h, w1_ref[...].astype(BF16))
    up = _dot(h, w3_ref[...].astype(BF16))
    a = (gt * _sigmoid(gt) * up).astype(BF16)
    half_gate = 0.5 * mod_ref[0, 3 * j + 2:3 * j + 3, :]
    acc[...] += half_gate * _dot(a, w2_ref[...].astype(BF16))

    @pl.when(f == nf - 1)
    def _():
        start_split(out_copy, o_hbm, out_split, i, slot)

        @pl.when(i == n - 1)
        def _():
            out_copy(o_hbm[0], 0, slot).wait()
            out_copy(o_hbm[0], 0, 1 - slot).wait()


def _ffn(xs, mod, g, w13, w2, l, sub, j, out_rows):
    nf = D_FF // FF_TILE
    n_tiles = N_TOK // FFN_ROW_TILE
    assert nf >= 2 and n_tiles >= 2 and sum(x.shape[0] for x in xs) == N_TOK and sum(out_rows) == N_TOK
    assert all(x.shape[0] % FFN_ROW_TILE == 0 for x in xs) and all(r % FFN_ROW_TILE == 0 for r in out_rows)
    any_spec = pl.BlockSpec(memory_space=pl.ANY)
    outs = pl.pallas_call(
        functools.partial(_ffn_kernel, j=j, n_in=len(xs), n_out=len(out_rows),
                          in_split=xs[0].shape[0] // FFN_ROW_TILE, out_split=out_rows[0] // FFN_ROW_TILE),
        grid=(n_tiles, nf),
        in_specs=[any_spec] * len(xs) + [
            pl.BlockSpec((1, N_MOD, D_MODEL), lambda i, f: (_cond_index(i, FFN_ROW_TILE), 0, 0)),
            pl.BlockSpec((1, D_MODEL), lambda i, f: (0, 0)),
            pl.BlockSpec((None, None, D_MODEL, FF_TILE), lambda i, f: (l, sub, 0, f)),
            pl.BlockSpec((None, None, D_MODEL, FF_TILE), lambda i, f: (l, sub, 0, nf + f)),
            pl.BlockSpec((None, None, FF_TILE, D_MODEL), lambda i, f: (l, sub, f, 0)),
        ],
        out_specs=[any_spec] * len(out_rows),
        out_shape=[jax.ShapeDtypeStruct((r, D_MODEL), F32) for r in out_rows],
        scratch_shapes=[
            pltpu.VMEM((2, FFN_ROW_TILE, D_MODEL), F32),
            pltpu.VMEM((FFN_ROW_TILE, D_MODEL), BF16),
            pltpu.SemaphoreType.DMA((2,)),
            pltpu.SemaphoreType.DMA((2,)),
        ],
        compiler_params=_params(("arbitrary", "arbitrary")),
        name="ffn",
    )(*xs, mod, g, w13, w13, w2)
    return outs


IN_CHUNK = 512


def _inproj_kernel(x_ref, mod_ref, g_ref, w_ref, *o_refs, widths):
    h = _modulated_norm(x_ref[...], g_ref[...], mod_ref, 1).astype(BF16)
    off = 0
    for o_ref, wd in zip(o_refs, widths):
        for c0 in range(0, wd, IN_CHUNK):
            o_ref[:, c0:c0 + IN_CHUNK] = _dot(h, w_ref[:, off + c0:off + c0 + IN_CHUNK])
        off += wd


def _inproj(x, mod, g, w, widths):
    n = sum(widths)
    return pl.pallas_call(
        functools.partial(_inproj_kernel, widths=widths),
        grid=(N_TOK // ROW_TILE,),
        in_specs=[
            pl.BlockSpec((ROW_TILE, D_MODEL), lambda i: (i, 0)),
            pl.BlockSpec((1, N_MOD, D_MODEL), lambda i: (_cond_index(i, ROW_TILE), 0, 0)),
            pl.BlockSpec((1, D_MODEL), lambda i: (0, 0)),
            pl.BlockSpec((D_MODEL, n), lambda i: (0, 0), pipeline_mode=pl.Buffered(1)),
        ],
        out_specs=[pl.BlockSpec((ROW_TILE, wd), lambda i: (i, 0)) for wd in widths],
        out_shape=[jax.ShapeDtypeStruct((N_TOK, wd), F32) for wd in widths],
        compiler_params=_params(("arbitrary",)),
        name="mixer_in_proj",
    )(x, mod, g, w)


def _outproj_kernel(x_ref, mod_ref, ac_ref, al_ref, bc_ref, bl_ref, w_ref, o_ref):
    wa = w_ref.shape[0] // 2
    is_ctx = pl.program_id(0) < N_CTX // ROW_TILE
    a = jnp.where(is_ctx, ac_ref[...], al_ref[...])
    b = jnp.where(is_ctx, bc_ref[...], bl_ref[...])
    y = _dot(a, w_ref[0:wa, :]) + _dot(b, w_ref[wa:, :])
    o_ref[...] = x_ref[...] + mod_ref[0, 5:6, :] * y


def _outproj(x, mod, a_ctx, a_lat, b_ctx, b_lat, w):
    wa = a_ctx.shape[1]
    nc = N_CTX // ROW_TILE
    ctx_spec = pl.BlockSpec((ROW_TILE, wa), lambda i: (jnp.minimum(i, nc - 1), 0))
    lat_spec = pl.BlockSpec((ROW_TILE, wa), lambda i: (jnp.maximum(i - nc, 0), 0))
    return pl.pallas_call(
        _outproj_kernel,
        grid=(N_TOK // ROW_TILE,),
        in_specs=[
            pl.BlockSpec((ROW_TILE, D_MODEL), lambda i: (i, 0)),
            pl.BlockSpec((1, N_MOD, D_MODEL), lambda i: (_cond_index(i, ROW_TILE), 0, 0)),
            ctx_spec, lat_spec, ctx_spec, lat_spec,
            pl.BlockSpec((2 * wa, D_MODEL), lambda i: (0, 0), pipeline_mode=pl.Buffered(1)),
        ],
        out_specs=pl.BlockSpec((ROW_TILE, D_MODEL), lambda i: (i, 0)),
        out_shape=jax.ShapeDtypeStruct((N_TOK, D_MODEL), F32),
        compiler_params=_params(("arbitrary",)),
        name="mixer_out_proj",
    )(x, mod, a_ctx, a_lat, b_ctx, b_lat, w)


def _rope(x, cos, sin):
    lane = lax.broadcasted_iota(jnp.int32, x.shape, 1)
    swapped = jnp.where((lane & 1) == 0, pltpu.roll(x, HEAD_DIM - 1, axis=1), pltpu.roll(x, 1, axis=1))
    return x * cos + swapped * sin


def _attend(q_heads, k, v_ones, o_ref, rows, col0):
    t = q_heads[0].shape[0]
    q = jnp.concatenate(q_heads, axis=0)
    s = lax.dot_general(q, k, (((1,), (1,)), ((), ())), preferred_element_type=F32) * (HEAD_DIM ** -0.5)
    e = jnp.exp(s - jnp.max(s, axis=-1, keepdims=True)).astype(BF16)
    oe = _dot(e, v_ones)
    o = oe[:, 0:HEAD_DIM] / oe[:, HEAD_DIM:]
    for i in range(len(q_heads)):
        o_ref[rows, col0 + i * HEAD_DIM:col0 + (i + 1) * HEAD_DIM] = o[i * t:(i + 1) * t].astype(o_ref.dtype)


def _attn_ctx_kernel(qkv_ref, qn_ref, kn_ref, o_ref, kc_ref, vc_ref):
    ones = jnp.ones((SEQ, HEAD_DIM), BF16)
    for s in range(ATTN_CTX_GROUP):
        rows = slice(s * SEQ, (s + 1) * SEQ)
        for kv in range(N_KV_HEADS):
            kcol = ATTN_W + kv * HEAD_DIM
            vcol = ATTN_W + KV_W + kv * HEAD_DIM
            k = _head_norm(qkv_ref[rows, kcol:kcol + HEAD_DIM], kn_ref[...])
            v = qkv_ref[rows, vcol:vcol + HEAD_DIM]
            kc_ref[rows, kv * HEAD_DIM:(kv + 1) * HEAD_DIM] = k
            vc_ref[rows, kv * HEAD_DIM:(kv + 1) * HEAD_DIM] = v
            qs = []
            for i in range(Q_PER_KV):
                qcol = (kv * Q_PER_KV + i) * HEAD_DIM
                qs.append(_head_norm(qkv_ref[rows, qcol:qcol + HEAD_DIM], qn_ref[...]).astype(BF16))
            _attend(qs, k.astype(BF16), jnp.concatenate([v.astype(BF16), ones], axis=1), o_ref, rows,
                    kv * Q_PER_KV * HEAD_DIM)


def _attn_ctx(qkv, qn, kn):
    rows = ATTN_CTX_GROUP * SEQ
    return pl.pallas_call(
        _attn_ctx_kernel,
        grid=(BATCH // ATTN_CTX_GROUP,),
        in_specs=[
            pl.BlockSpec((rows, QKV_W), lambda b: (b, 0)),
            pl.BlockSpec((1, HEAD_DIM), lambda b: (0, 0)),
            pl.BlockSpec((1, HEAD_DIM), lambda b: (0, 0)),
        ],
        out_specs=[
            pl.BlockSpec((rows, ATTN_W), lambda b: (b, 0)),
            pl.BlockSpec((rows, KV_W), lambda b: (b, 0)),
            pl.BlockSpec((rows, KV_W), lambda b: (b, 0)),
        ],
        out_shape=[
            jax.ShapeDtypeStruct((N_CTX, ATTN_W), BF16),
            jax.ShapeDtypeStruct((N_CTX, KV_W), F32),
            jax.ShapeDtypeStruct((N_CTX, KV_W), F32),
        ],
        compiler_params=_params(("arbitrary",)),
        name="attention_context",
    )(qkv, qn, kn)


def _attn_lat_kernel(qkv_ref, ck_ref, cv_ref, qn_ref, kn_ref, cos_ref, sin_ref, o_ref, k_scr, v_scr):
    qi = pl.program_id(1)

    @pl.when(qi == 0)
    def _():
        k_scr[0:PAST_LEN, :] = ck_ref[0].astype(BF16)
        for kv in range(N_KV_HEADS):
            kcol = ATTN_W + kv * HEAD_DIM
            vcol = ATTN_W + KV_W + kv * HEAD_DIM
            k = _head_norm(qkv_ref[:, kcol:kcol + HEAD_DIM], kn_ref[...])
            k = _rope(k, cos_ref[...], sin_ref[...])
            k_scr[PAST_LEN:, kv * HEAD_DIM:(kv + 1) * HEAD_DIM] = k.astype(BF16)
            v0 = 2 * kv * HEAD_DIM
            v_scr[0:PAST_LEN, v0:v0 + HEAD_DIM] = cv_ref[0, :, kv * HEAD_DIM:(kv + 1) * HEAD_DIM].astype(BF16)
            v_scr[PAST_LEN:, v0:v0 + HEAD_DIM] = qkv_ref[:, vcol:vcol + HEAD_DIM].astype(BF16)
            v_scr[:, v0 + HEAD_DIM:v0 + 2 * HEAD_DIM] = jnp.ones((PAST_LEN + DEC_SEQ, HEAD_DIM), BF16)

    r0 = pl.multiple_of(qi * Q_TILE, Q_TILE)
    cos = cos_ref[pl.ds(r0, Q_TILE), :]
    sin = sin_ref[pl.ds(r0, Q_TILE), :]
    for kv in range(N_KV_HEADS):
        qs = []
        for i in range(Q_PER_KV):
            qcol = (kv * Q_PER_KV + i) * HEAD_DIM
            q = _head_norm(qkv_ref[pl.ds(r0, Q_TILE), qcol:qcol + HEAD_DIM], qn_ref[...])
            qs.append(_rope(q, cos, sin).astype(BF16))
        _attend(qs, k_scr[:, kv * HEAD_DIM:(kv + 1) * HEAD_DIM], v_scr[:, 2 * kv * HEAD_DIM:2 * (kv + 1) * HEAD_DIM],
                o_ref, slice(None), kv * Q_PER_KV * HEAD_DIM)


def _attn_lat(qkv, cache_k, cache_v, qn, kn, cos, sin):
    row0 = N_CTX // DEC_SEQ
    qt = DEC_SEQ // Q_TILE
    return pl.pallas_call(
        _attn_lat_kernel,
        grid=(DEC_BATCH, qt),
        in_specs=[
            pl.BlockSpec((DEC_SEQ, QKV_W), lambda b, q: (row0 + b, 0)),
            pl.BlockSpec((1, PAST_LEN, KV_W), lambda b, q: (b, 0, 0)),
            pl.BlockSpec((1, PAST_LEN, KV_W), lambda b, q: (b, 0, 0)),
            pl.BlockSpec((1, HEAD_DIM), lambda b, q: (0, 0)),
            pl.BlockSpec((1, HEAD_DIM), lambda b, q: (0, 0)),
            pl.BlockSpec((DEC_SEQ, HEAD_DIM), lambda b, q: (0, 0)),
            pl.BlockSpec((DEC_SEQ, HEAD_DIM), lambda b, q: (0, 0)),
        ],
        out_specs=pl.BlockSpec((Q_TILE, ATTN_W), lambda b, q: (b * qt + q, 0)),
        out_shape=jax.ShapeDtypeStruct((N_LAT, ATTN_W), BF16),
        scratch_shapes=[pltpu.VMEM((PAST_LEN + DEC_SEQ, KV_W), BF16),
                        pltpu.VMEM((PAST_LEN + DEC_SEQ, 2 * KV_W), BF16)],
        compiler_params=_params(("arbitrary", "arbitrary")),
        name="attention_latent",
    )(qkv, cache_k, cache_v, qn, kn, cos, sin)


def _lru_kernel(l_ref, cw_ref, cb_ref, gw_ref, gb_ref, lam_ref, h0f_ref, h0b_ref, o_ref, hf_ref, hb_ref,
                af_scr, bf_scr, ab_scr, bb_scr, yf_scr, yb_scr, *, seq_len):
    group = l_ref.shape[0] // seq_len
    xc = _dwconv(l_ref[:, 0:LRU_W], cw_ref[...], cb_ref[...], LRU_CONV_LEFT, _RowShifter(seq_len))
    xcb = xc.astype(BF16)
    rate = []
    for d in range(2):
        z = -lam_ref[d:d + 1, :]
        rate.append(LRU_C * (jnp.maximum(z, 0.0) + jnp.log1p(jnp.exp(-jnp.abs(z)))))
    for hd in range(LRU_HEADS):
        cols = slice(hd * LRU_BLK, (hd + 1) * LRU_BLK)
        pre = _dot(xcb[:, cols], gw_ref[hd].astype(BF16))
        for d, (a_scr, b_scr) in enumerate(((af_scr, bf_scr), (ab_scr, bb_scr))):
            base = d * 2 * LRU_BLK
            r = _sigmoid(pre[:, base:base + LRU_BLK] + gb_ref[2 * d:2 * d + 1, cols])
            i = _sigmoid(pre[:, base + LRU_BLK:base + 2 * LRU_BLK] + gb_ref[2 * d + 1:2 * d + 2, cols])
            neg_log_a = r * rate[d][:, cols]
            a = jnp.exp(-neg_log_a)
            a_scr[:, cols] = a
            var = jnp.tanh(neg_log_a) * (1.0 + a * a)
            b_scr[:, cols] = jnp.where(var > 0.0, var * lax.rsqrt(var), 0.0) * (i * xc[:, cols])

    def step(t, carry):
        out = []
        for g in range(group):
            hf, hb = carry[2 * g], carry[2 * g + 1]
            tf = g * seq_len + t
            tb = g * seq_len + seq_len - 1 - t
            hf = af_scr[pl.ds(tf, 1), :] * hf + bf_scr[pl.ds(tf, 1), :]
            hb = ab_scr[pl.ds(tb, 1), :] * hb + bb_scr[pl.ds(tb, 1), :]
            yf_scr[pl.ds(tf, 1), :] = hf
            yb_scr[pl.ds(tb, 1), :] = hb
            out += [hf, hb]
        return tuple(out)

    init = []
    for g in range(group):
        init += [h0f_ref[g], h0b_ref[g]]
    final = lax.fori_loop(0, seq_len, step, tuple(init), unroll=8)
    for g in range(group):
        hf_ref[g] = final[2 * g]
        hb_ref[g] = final[2 * g + 1]
    lg = l_ref[:, LRU_W:]
    gelu = 0.5 * lg * (1.0 + jnp.tanh(math.sqrt(2.0 / math.pi) * (lg + 0.044715 * (lg * lg * lg))))
    o_ref[...] = (gelu * (yf_scr[...] + yb_scr[...])).astype(o_ref.dtype)


def _lru(l_all, cw, cb, gw, gb, lam, h0f, h0b, seq_len, n_seq, row0, group):
    rows = seq_len * group
    blk0 = row0 // rows
    vec = lambda b: (0, 0)
    return pl.pallas_call(
        functools.partial(_lru_kernel, seq_len=seq_len),
        grid=(n_seq // group,),
        in_specs=[
            pl.BlockSpec((rows, 2 * LRU_W), lambda b: (blk0 + b, 0)),
            pl.BlockSpec((LRU_CONV, LRU_W), vec),
            pl.BlockSpec((1, LRU_W), vec),
            pl.BlockSpec((LRU_HEADS, LRU_BLK, 4 * LRU_BLK), lambda b: (0, 0, 0)),
            pl.BlockSpec((4, LRU_W), vec),
            pl.BlockSpec((2, LRU_W), vec),
            pl.BlockSpec((group, 1, LRU_W), lambda b: (b, 0, 0)),
            pl.BlockSpec((group, 1, LRU_W), lambda b: (b, 0, 0)),
        ],
        out_specs=[
            pl.BlockSpec((rows, LRU_W), lambda b: (b, 0)),
            pl.BlockSpec((group, 1, LRU_W), lambda b: (b, 0, 0)),
            pl.BlockSpec((group, 1, LRU_W), lambda b: (b, 0, 0)),
        ],
        out_shape=[
            jax.ShapeDtypeStruct((n_seq * seq_len, LRU_W), BF16),
            jax.ShapeDtypeStruct((n_seq, 1, LRU_W), F32),
            jax.ShapeDtypeStruct((n_seq, 1, LRU_W), F32),
        ],
        scratch_shapes=[pltpu.VMEM((rows, LRU_W), F32)] * 6,
        compiler_params=_params(("arbitrary",)),
        name="rg_lru",
    )(l_all, cw, cb, gw, gb, lam, h0f, h0b)


def _hy_filter_kernel(z_ref, t_ref, ckh_ref, ckl_ref, alt_ref, w1_ref, b1_ref, w2_ref, b2_ref, w3_ref, fr_ref,
                      dec_ref, kf_ref, kn_ref):
    z = jnp.sin(fr_ref[0:1, :] * (_dot_f32(z_ref[...], w1_ref[...]) + b1_ref[...]))
    z = jnp.sin(fr_ref[1:2, :] * (_dot_f32(z, w2_ref[...]) + b2_ref[...]))
    filt = _dot_f32(z, w3_ref[...]) * jnp.exp(-t_ref[...] * jnp.abs(dec_ref[...]))
    filt = filt / jnp.sum(jnp.abs(filt), axis=0, keepdims=True)
    fh, fl = _split_bf16(filt)
    ckh = ckh_ref[...]
    kf = _dot(ckh, fh) + (_dot(ckh, fl) + _dot(ckl_ref[...], fh))
    kn = jnp.sum(alt_ref[...] * filt, axis=0, keepdims=True)
    for o in range(HY_ORDER):
        kf_ref[o] = kf[:, o * HY_W:(o + 1) * HY_W]
        kn_ref[o:o + 1, :] = kn[:, o * HY_W:(o + 1) * HY_W]


def _hy_filter(consts, w1, b1, w2, b2, w3, freq, decay, seq_len):
    ckh, ckl = _split_bf16(consts["ck"])
    return pl.pallas_call(
        _hy_filter_kernel,
        out_shape=[jax.ShapeDtypeStruct((HY_ORDER, seq_len, HY_W), F32), jax.ShapeDtypeStruct((HY_ORDER, HY_W), F32)],
        compiler_params=pltpu.CompilerParams(vmem_limit_bytes=VMEM_LIMIT_BYTES),
        name="hyena_filter",
    )(consts["z"], consts["t"], ckh, ckl, consts["alt_w"], w1, b1, w2, b2, w3, freq, decay)


def _longconv(u, fwd_ref, inv_ref, kf2, kn, alt, skip):
    n = u.shape[0]
    spec = _dot(fwd_ref[...], u.astype(BF16)) * kf2
    y = _dot(inv_ref[...], spec.astype(BF16))
    nyq = jnp.sum(alt * u, axis=0, keepdims=True) * kn * (1.0 / (2 * n))
    return y + alt * nyq + u * skip


def _hyena_kernel(x1_ref, x2_ref, v_ref, cw_ref, cb_ref, kf_ref, kn_ref, skip_ref, fwd_ref, inv_ref, alt_ref, o_ref,
                  *, seq_len, chain_w):
    shift = _RowShifter(seq_len)
    alt = alt_ref[...]
    v = _dwconv(v_ref[...], cw_ref[2], cb_ref[2:3, :], HY_CONV_LEFT, shift)
    gates = [_dwconv(g_ref[...], cw_ref[o], cb_ref[o:o + 1, :], HY_CONV_LEFT, shift)
             for o, g_ref in enumerate((x1_ref, x2_ref))]
    for s in range(v.shape[0] // seq_len):
        rows = slice(s * seq_len, (s + 1) * seq_len)
        for c0 in range(0, v.shape[1], chain_w):
            cols = slice(c0, c0 + chain_w)
            z = v[rows, cols]
            for o in range(HY_ORDER):
                kfo = kf_ref[o, :, cols]
                z = gates[o][rows, cols] * _longconv(z, fwd_ref, inv_ref, jnp.concatenate([kfo, kfo], axis=0),
                                                     kn_ref[o:o + 1, cols], alt, skip_ref[o:o + 1, cols])
            o_ref[rows, cols] = z.astype(o_ref.dtype)


def _hyena(hy_all, cw, cb, kf, kn, skip, consts, seq_len, n_seq, row0, group, block_w, chain_w):
    rows = seq_len * group
    blk0 = row0 // rows
    nc = HY_W // block_w
    const = dict(pipeline_mode=pl.Buffered(1))
    in_specs = [pl.BlockSpec((rows, block_w), lambda b, c, g=g: (blk0 + b, g * nc + c)) for g in range(3)]
    in_specs += [
        pl.BlockSpec((HY_ORDER + 1, HY_CONV, block_w), lambda b, c: (0, 0, c)),
        pl.BlockSpec((HY_ORDER + 1, block_w), lambda b, c: (0, c)),
        pl.BlockSpec((HY_ORDER, seq_len, block_w), lambda b, c: (0, 0, c)),
        pl.BlockSpec((HY_ORDER, block_w), lambda b, c: (0, c)),
        pl.BlockSpec((HY_ORDER, block_w), lambda b, c: (0, c)),
        pl.BlockSpec((2 * seq_len, seq_len), lambda b, c: (0, 0), **const),
        pl.BlockSpec((seq_len, 2 * seq_len), lambda b, c: (0, 0), **const),
        pl.BlockSpec((seq_len, 1), lambda b, c: (0, 0)),
    ]
    return pl.pallas_call(
        functools.partial(_hyena_kernel, seq_len=seq_len, chain_w=chain_w),
        grid=(n_seq // group, nc),
        in_specs=in_specs,
        out_specs=pl.BlockSpec((rows, block_w), lambda b, c: (b, c)),
        out_shape=jax.ShapeDtypeStruct((n_seq * seq_len, HY_W), BF16),
        compiler_params=_params(("arbitrary", "arbitrary")),
        name="hyena",
    )(hy_all, hy_all, hy_all, cw.reshape(HY_CONV, HY_ORDER + 1, HY_W).transpose(1, 0, 2),
      cb.reshape(HY_ORDER + 1, HY_W), kf, kn, skip, consts["fwd"].astype(BF16), consts["inv"].astype(BF16),
      consts["alt"])


def _pool_kernel(p_ref, w_ref, s_ref, o_ref, *, seq_len):
    shift = _RowShifter(seq_len)
    t = lax.broadcasted_iota(jnp.int32, (p_ref.shape[0], 1), 0) & (seq_len - 1)
    for gi, win in enumerate(POOL_WINDOWS):
        cols = slice(gi * POOL_GW, (gi + 1) * POOL_GW)
        x = p_ref[:, cols]
        half = win // 2
        back, fwd = x, x
        m = 1
        while m < half:
            back = back + shift(back, -m)
            fwd = fwd + shift(fwd, m)
            m *= 2
        s = shift(back, -1) + fwd
        cnt = (jnp.minimum(t + half, seq_len) - jnp.maximum(t - half, 0)).astype(F32)
        y = _dot((s / cnt - x).astype(BF16), w_ref[gi].astype(BF16))
        o_ref[:, cols] = (y * s_ref[:, cols]).astype(o_ref.dtype)


def _pool(p_all, w, scale, seq_len, n_seq, row0, group):
    rows = seq_len * group
    blk0 = row0 // rows
    return pl.pallas_call(
        functools.partial(_pool_kernel, seq_len=seq_len),
        grid=(n_seq // group,),
        in_specs=[
            pl.BlockSpec((rows, POOL_W), lambda b: (blk0 + b, 0)),
            pl.BlockSpec((len(POOL_WINDOWS), POOL_GW, POOL_GW), lambda b: (0, 0, 0)),
            pl.BlockSpec((1, POOL_W), lambda b: (0, 0)),
        ],
        out_specs=pl.BlockSpec((rows, POOL_W), lambda b: (b, 0)),
        out_shape=jax.ShapeDtypeStruct((n_seq * seq_len, POOL_W), BF16),
        compiler_params=_params(("arbitrary",)),
        name="multi_pool",
    )(p_all, w, scale)


def _rope_tables():
    rows = DEC_SEQ // GRID_W
    r = np.repeat(np.arange(rows), GRID_W).astype(np.float64)
    col = np.tile(np.arange(GRID_W), rows).astype(np.float64)
    half = HEAD_DIM // 2
    inv = ROPE_THETA ** (-np.arange(0, half, 2, dtype=np.float64) / half)
    ang = np.concatenate([r[:, None] * inv, col[:, None] * inv], axis=-1)
    cos = np.repeat(np.cos(ang), 2, axis=-1)
    sin = np.repeat(np.sin(ang), 2, axis=-1) * np.tile(np.array([-1.0, 1.0]), half)
    return jnp.asarray(cos, F32), jnp.asarray(sin, F32)


def _hyena_tables(n):
    idx = np.arange(n, dtype=np.float64)
    t = idx / max(n - 1, 1)
    bands = np.linspace(1e-4, HY_BANDS - 1, HY_BANDS)
    f = 2.0 * math.pi * idx[:, None] * bands[None, :] / n
    z = np.zeros((n, HY_EMB_PAD))
    z[:, 0] = t
    z[:, 1:1 + HY_BANDS] = np.cos(f)
    z[:, 1 + HY_BANDS:HY_EMB] = -np.sin(f)
    ang = math.pi * ((idx[:, None] * idx[None, :]) % (2 * n)) / n
    wgt_n = np.where(idx == 0, 1.0, 2.0)
    alt = np.where(idx % 2 == 0, 1.0, -1.0)
    fwd = np.concatenate([np.cos(ang), np.sin(ang)], axis=0)
    inv = np.concatenate([np.cos(ang) * wgt_n[None, :], np.sin(ang) * wgt_n[None, :]], axis=1) / (2 * n)
    return {
        "z": jnp.asarray(z, F32),
        "t": jnp.asarray(t[:, None], F32),
        "ck": jnp.asarray(np.cos(ang) * wgt_n[None, :], F32),
        "alt_w": jnp.asarray((alt * wgt_n)[:, None], F32),
        "alt": jnp.asarray(alt[:, None], F32),
        "fwd": jnp.asarray(fwd, F32),
        "inv": jnp.asarray(inv, F32),
    }


def kernel(x_prompt, x_sample, cache_k, cache_v, state_lru_fwd, state_lru_bwd, c, c_ctx, norm_g, w_mod, b_mod, ffn_w13, ffn_w2, ab_w_in, ab_q_norm, ab_k_norm, lru_conv_w, lru_conv_b, lru_gate_w, lru_gate_b, lru_lambda, ab_w_out, cd_w_in, hy_conv_w, hy_conv_b, hy_w1, hy_b1, hy_w2, hy_b2, hy_w3, hy_freq, hy_decay, hy_skip, pool_w, pool_scale, cd_w_out):
    xs = [x_prompt.reshape(N_CTX, D_MODEL), x_sample.reshape(N_LAT, D_MODEL)]
    cond = jnp.concatenate([c_ctx[None], c, jnp.zeros((COND_PAD - N_COND, D_MODEL), F32)], axis=0)
    mods = _modulation(cond, w_mod, b_mod)[:, :N_COND].reshape(DEPTH, N_COND, N_MOD, D_MODEL)
    rope_cos, rope_sin = _rope_tables()

    k_list, v_list, hf_list, hb_list = [], [], [], []
    for l in range(DEPTH):
        mod = mods[l]
        g = norm_g[l]
        (x,) = _ffn(xs, mod, g[0:1], ffn_w13, ffn_w2, l, 0, 0, [N_TOK])
        if l % 2 == 0:
            e = l // 2
            qkv, lxg = _inproj(x, mod, g[1:2], ab_w_in[e].astype(BF16), (QKV_W, 2 * LRU_W))
            qn, kn = ab_q_norm[e][None], ab_k_norm[e][None]
            attn_c, kc, vc = _attn_ctx(qkv, qn, kn)
            attn_l = _attn_lat(qkv, cache_k[:, e].reshape(DEC_BATCH, PAST_LEN, KV_W),
                               cache_v[:, e].reshape(DEC_BATCH, PAST_LEN, KV_W), qn, kn, rope_cos, rope_sin)
            gw = jnp.transpose(lru_gate_w[e], (2, 3, 0, 1, 4)).reshape(LRU_HEADS, LRU_BLK, 4 * LRU_BLK)
            gb = lru_gate_b[e].reshape(4, LRU_W)
            lru_args = (lru_conv_w[e], lru_conv_b[e][None], gw, gb, lru_lambda[e])
            zeros = jnp.zeros((BATCH, 1, LRU_W), F32)
            rec_c, hf, hb = _lru(lxg, *lru_args, zeros, zeros, SEQ, BATCH, 0, LRU_CTX_GROUP)
            rec_l, _, _ = _lru(lxg, *lru_args, state_lru_fwd[:, e][:, None], state_lru_bwd[:, e][:, None],
                               DEC_SEQ, DEC_BATCH, N_CTX, 1)
            x = _outproj(x, mod, attn_c, attn_l, rec_c, rec_l, ab_w_out[e].astype(BF16))
            k_list.append(kc.reshape(BATCH, SEQ, N_KV_HEADS, HEAD_DIM))
            v_list.append(vc.reshape(BATCH, SEQ, N_KV_HEADS, HEAD_DIM))
            hf_list.append(hf.reshape(BATCH, LRU_W))
            hb_list.append(hb.reshape(BATCH, LRU_W))
        else:
            o = l // 2
            hy, pw = _inproj(x, mod, g[1:2], cd_w_in[o].astype(BF16), ((HY_ORDER + 1) * HY_W, POOL_W))
            w1 = jnp.zeros((HY_EMB_PAD, HY_FH), F32).at[:HY_EMB].set(hy_w1[o])
            z_out, p_out = [], []
            for seq_len, n_seq, row0, hy_cfg, pool_group in (
                    (SEQ, BATCH, 0, (HY_CTX_GROUP, HY_W, HY_W), POOL_CTX_GROUP),
                    (DEC_SEQ, DEC_BATCH, N_CTX, (1, HY_LAT_BLOCK_W, HY_CHAIN_W), 1)):
                consts = _hyena_tables(seq_len)
                kf, kn = _hy_filter(consts, w1, hy_b1[o][None], hy_w2[o], hy_b2[o][None], hy_w3[o], hy_freq[o],
                                    hy_decay[o][None], seq_len)
                z_out.append(_hyena(hy, hy_conv_w[o], hy_conv_b[o], kf, kn, hy_skip[o], consts,
                                    seq_len, n_seq, row0, *hy_cfg))
                p_out.append(_pool(pw, pool_w[o], pool_scale[o][None], seq_len, n_seq, row0, pool_group))
            x = _outproj(x, mod, z_out[0], z_out[1], p_out[0], p_out[1], cd_w_out[o].astype(BF16))
        xs = _ffn([x], mod, g[2:3], ffn_w13, ffn_w2, l, 1, 2, [N_TOK] if l + 1 < DEPTH else [N_CTX, N_LAT])

    y_prompt = xs[0].reshape(BATCH, SEQ, D_MODEL)
    y_sample = xs[1].reshape(DEC_BATCH, DEC_SEQ, D_MODEL)
    return (y_prompt, y_sample, jnp.stack(k_list, axis=1), jnp.stack(v_list, axis=1),
            jnp.stack(hf_list, axis=1), jnp.stack(hb_list, axis=1))
```

```python
import functools
import math

import numpy as np
import jax
import jax.numpy as jnp
from jax import lax
from jax.experimental import pallas as pl
from jax.experimental.pallas import tpu as pltpu

F32 = jnp.float32
BF16 = jnp.bfloat16

D_MODEL = 2048
BATCH = 32
SEQ = 256
DEPTH = 2
DEC_BATCH = 2
DEC_SEQ = 1024
PAST_LEN = 512
GRID_W = 64
N_MOD = 9
NORM_EPS = 1e-6
D_FF = 5632
HEAD_DIM = 128
N_Q_HEADS = 8
N_KV_HEADS = 2
Q_PER_KV = N_Q_HEADS // N_KV_HEADS
ATTN_W = N_Q_HEADS * HEAD_DIM
KV_W = N_KV_HEADS * HEAD_DIM
QKV_W = ATTN_W + 2 * KV_W
ROPE_THETA = 10000.0
LRU_W = 1024
LRU_HEADS = 8
LRU_BLK = LRU_W // LRU_HEADS
LRU_CONV = 4
LRU_CONV_LEFT = 2
LRU_C = 8.0
HY_W = 1024
HY_ORDER = 2
HY_CONV = 3
HY_CONV_LEFT = 1
HY_EMB = 33
HY_EMB_PAD = 128
HY_BANDS = (HY_EMB - 1) // 2
HY_FH = 64
POOL_W = 1024
POOL_WINDOWS = (2, 4, 8, 16)
POOL_GW = POOL_W // len(POOL_WINDOWS)

N_CTX = BATCH * SEQ
N_LAT = DEC_BATCH * DEC_SEQ
N_TOK = N_CTX + N_LAT
N_COND = 1 + DEC_BATCH
COND_PAD = 8

VMEM_LIMIT_BYTES = 56 * 1024 * 1024

ROW_TILE = 512
FFN_ROW_TILE = 1024
FF_TILE = 512
FFN_NORM_ROWS = 32
MOD_TILE = 1024
Q_TILE = 256
ATTN_CTX_GROUP = 4
LRU_CTX_GROUP = 2
LRU_SCAN_ROWS = 8
HY_CTX_GROUP = 2
POOL_CTX_GROUP = 4
HY_CHAIN_W = 512
HY_LAT_BLOCK_W = 512


def _params(sem):
    return pltpu.CompilerParams(dimension_semantics=sem, vmem_limit_bytes=VMEM_LIMIT_BYTES)


def _cond_index(i, tile):
    return jnp.maximum((i * tile) // DEC_SEQ - (N_CTX // DEC_SEQ - 1), 0)


def _sigmoid(x):
    return 1.0 / (1.0 + jnp.exp(-x))


def _dot(a, b):
    return jnp.dot(a, b, preferred_element_type=F32)


def _dot_f32(a, b):
    return jnp.dot(a, b, preferred_element_type=F32, precision=lax.Precision.HIGHEST)


def _split_bf16(x):
    hi = x.astype(BF16)
    return hi, (x - hi.astype(F32)).astype(BF16)


def _dot3(a_hi, a_lo, b):
    b_hi, b_lo = _split_bf16(b)
    return _dot(a_hi, b_hi) + (_dot(a_hi, b_lo) + _dot(a_lo, b_hi))


def _modulated_norm(x, g, mod_ref, j):
    ms = jnp.mean(x * x, axis=-1, keepdims=True)
    y = x * lax.rsqrt(ms + NORM_EPS) * g
    return y * (1.0 + mod_ref[0, 3 * j + 1:3 * j + 2, :]) + mod_ref[0, 3 * j:3 * j + 1, :]


def _head_norm(x, g):
    ms = jnp.mean(x * x, axis=-1, keepdims=True)
    return x * lax.rsqrt(ms + NORM_EPS) * g


class _RowShifter:
    def __init__(self, period):
        self.period = period
        self._masks = {}

    def __call__(self, x, off):
        if off == 0:
            return x
        key = (x.shape, off)
        if key not in self._masks:
            t = lax.broadcasted_iota(jnp.int32, x.shape, 0) & (self.period - 1)
            self._masks[key] = t >= -off if off < 0 else t < self.period - off
        return jnp.where(self._masks[key], pltpu.roll(x, (-off) % x.shape[0], axis=0), 0.0)


def _dwconv(x, w, b, left, shift):
    acc = b + shift(x, -left) * w[0:1, :]
    for j in range(1, w.shape[0]):
        acc = acc + shift(x, j - left) * w[j:j + 1, :]
    return acc


def _mod_kernel(c_ref, w_ref, b_ref, o_ref):
    c = c_ref[...]
    s = (c * _sigmoid(c)).astype(BF16)
    o_ref[0] = _dot(s, w_ref[0].astype(BF16)) + b_ref[0]


def _modulation(cond, w_mod, b_mod):
    n = N_MOD * D_MODEL
    return pl.pallas_call(
        _mod_kernel,
        grid=(DEPTH, n // MOD_TILE),
        in_specs=[
            pl.BlockSpec((COND_PAD, D_MODEL), lambda l, j: (0, 0)),
            pl.BlockSpec((1, D_MODEL, MOD_TILE), lambda l, j: (l, 0, j)),
            pl.BlockSpec((1, 1, MOD_TILE), lambda l, j: (l, 0, j)),
        ],
        out_specs=pl.BlockSpec((1, COND_PAD, MOD_TILE), lambda l, j: (l, 0, j)),
        out_shape=jax.ShapeDtypeStruct((DEPTH, COND_PAD, n), F32),
        compiler_params=_params(("arbitrary", "arbitrary")),
        name="modulation",
    )(cond, w_mod, b_mod.reshape(DEPTH, 1, n))


def _ffn_kernel(*refs, j, n_in, n_out, in_split, out_split):
    x_hbm = refs[:n_in]
    mod_ref, g_ref, w1_ref, w3_ref, w2_ref = refs[n_in:n_in + 5]
    o_hbm = refs[n_in + 5:n_in + 5 + n_out]
    buf, h_ref, sem_in, sem_out = refs[n_in + 5 + n_out:]
    i, f = pl.program_id(0), pl.program_id(1)
    n, nf = pl.num_programs(0), pl.num_programs(1)
    slot = lax.rem(i, 2)
    acc = buf.at[slot]

    def tile_rows(t):
        return pl.ds(pl.multiple_of(t * FFN_ROW_TILE, FFN_ROW_TILE), FFN_ROW_TILE)

    def in_copy(src, t, s):
        return pltpu.make_async_copy(src.at[tile_rows(t), :], buf.at[s], sem_in.at[s])

    def out_copy(dst, t, s):
        return pltpu.make_async_copy(buf.at[s], dst.at[tile_rows(t), :], sem_out.at[s])

    def start_split(copy, refs_, split, t, s):
        if len(refs_) == 1:
            copy(refs_[0], t, s).start()
        else:
            @pl.when(t < split)
            def _():
                copy(refs_[0], t, s).start()

            @pl.when(t >= split)
            def _():
                copy(refs_[1], t - split, s).start()

    @pl.when(f == 0)
    def _():
        @pl.when(i == 0)
        def _():
            start_split(in_copy, x_hbm, in_split, i, slot)

        in_copy(x_hbm[0], 0, slot).wait()
        gain = g_ref[...] * (1.0 + mod_ref[0, 3 * j + 1:3 * j + 2, :])
        shift = mod_ref[0, 3 * j:3 * j + 1, :]

        def chunk(r, carry):
            rows = pl.ds(pl.multiple_of(r * FFN_NORM_ROWS, FFN_NORM_ROWS), FFN_NORM_ROWS)
            x = acc[rows, :]
            ms = jnp.mean(x * x, axis=-1, keepdims=True)
            h_ref[rows, :] = (x * lax.rsqrt(ms + NORM_EPS) * gain + shift).astype(BF16)
            return carry

        lax.fori_loop(0, FFN_ROW_TILE // FFN_NORM_ROWS, chunk, 0, unroll=4)

    @pl.when((f == 1) & (i + 1 < n))
    def _():
        @pl.when(i >= 1)
        def _():
            out_copy(o_hbm[0], 0, 1 - slot).wait()

        start_split(in_copy, x_hbm, in_split, i + 1, 1 - slot)

    h = h_ref[...]
    gt = _dot(h, w1_ref[...].astype(BF16))
    up = _dot(h, w3_ref[...].astype(BF16))
    a = (gt * _sigmoid(gt) * up).astype(BF16)
    half_gate = 0.5 * mod_ref[0, 3 * j + 2:3 * j + 3, :]
    acc[...] += half_gate * _dot(a, w2_ref[...].astype(BF16))

    @pl.when(f == nf - 1)
    def _():
        start_split(out_copy, o_hbm, out_split, i, slot)

        @pl.when(i == n - 1)
        def _():
            out_copy(o_hbm[0], 0, slot).wait()
            out_copy(o_hbm[0], 0, 1 - slot).wait()


def _ffn(xs, mod, g, w13, w2, l, sub, j, out_rows):
    nf = D_FF // FF_TILE
    n_tiles = N_TOK // FFN_ROW_TILE
    assert nf >= 2 and n_tiles >= 2 and sum(x.shape[0] for x in xs) == N_TOK and sum(out_rows) == N_TOK
    assert all(x.shape[0] % FFN_ROW_TILE == 0 for x in xs) and all(r % FFN_ROW_TILE == 0 for r in out_rows)
    any_spec = pl.BlockSpec(memory_space=pl.ANY)
    outs = pl.pallas_call(
        functools.partial(_ffn_kernel, j=j, n_in=len(xs), n_out=len(out_rows),
                          in_split=xs[0].shape[0] // FFN_ROW_TILE, out_split=out_rows[0] // FFN_ROW_TILE),
        grid=(n_tiles, nf),
        in_specs=[any_spec] * len(xs) + [
            pl.BlockSpec((1, N_MOD, D_MODEL), lambda i, f: (_cond_index(i, FFN_ROW_TILE), 0, 0)),
            pl.BlockSpec((1, D_MODEL), lambda i, f: (0, 0)),
            pl.BlockSpec((None, None, D_MODEL, FF_TILE), lambda i, f: (l, sub, 0, f)),
            pl.BlockSpec((None, None, D_MODEL, FF_TILE), lambda i, f: (l, sub, 0, nf + f)),
            pl.BlockSpec((None, None, FF_TILE, D_MODEL), lambda i, f: (l, sub, f, 0)),
        ],
        out_specs=[any_spec] * len(out_rows),
        out_shape=[jax.ShapeDtypeStruct((r, D_MODEL), F32) for r in out_rows],
        scratch_shapes=[
            pltpu.VMEM((2, FFN_ROW_TILE, D_MODEL), F32),
            pltpu.VMEM((FFN_ROW_TILE, D_MODEL), BF16),
            pltpu.SemaphoreType.DMA((2,)),
            pltpu.SemaphoreType.DMA((2,)),
        ],
        compiler_params=_params(("arbitrary", "arbitrary")),
        name="ffn",
    )(*xs, mod, g, w13, w13, w2)
    return outs


IN_CHUNK = 512


def _inproj_kernel(x_ref, mod_ref, g_ref, w_ref, *o_refs, widths):
    h = _modulated_norm(x_ref[...], g_ref[...], mod_ref, 1).astype(BF16)
    off = 0
    for o_ref, wd in zip(o_refs, widths):
        for c0 in range(0, wd, IN_CHUNK):
            o_ref[:, c0:c0 + IN_CHUNK] = _dot(h, w_ref[:, off + c0:off + c0 + IN_CHUNK])
        off += wd


def _inproj(x, mod, g, w, widths):
    n = sum(widths)
    return pl.pallas_call(
        functools.partial(_inproj_kernel, widths=widths),
        grid=(N_TOK // ROW_TILE,),
        in_specs=[
            pl.BlockSpec((ROW_TILE, D_MODEL), lambda i: (i, 0)),
            pl.BlockSpec((1, N_MOD, D_MODEL), lambda i: (_cond_index(i, ROW_TILE), 0, 0)),
            pl.BlockSpec((1, D_MODEL), lambda i: (0, 0)),
            pl.BlockSpec((D_MODEL, n), lambda i: (0, 0), pipeline_mode=pl.Buffered(1)),
        ],
        out_specs=[pl.BlockSpec((ROW_TILE, wd), lambda i: (i, 0)) for wd in widths],
        out_shape=[jax.ShapeDtypeStruct((N_TOK, wd), F32) for wd in widths],
        compiler_params=_params(("arbitrary",)),
        name="mixer_in_proj",
    )(x, mod, g, w)


def _outproj_kernel(x_ref, mod_ref, ac_ref, al_ref, bc_ref, bl_ref, w_ref, o_ref):
    wa = w_ref.shape[0] // 2
    is_ctx = pl.program_id(0) < N_CTX // ROW_TILE
    a = jnp.where(is_ctx, ac_ref[...], al_ref[...])
    b = jnp.where(is_ctx, bc_ref[...], bl_ref[...])
    y = _dot(a, w_ref[0:wa, :]) + _dot(b, w_ref[wa:, :])
    o_ref[...] = x_ref[...] + mod_ref[0, 5:6, :] * y


def _outproj(x, mod, a_ctx, a_lat, b_ctx, b_lat, w):
    wa = a_ctx.shape[1]
    nc = N_CTX // ROW_TILE
    ctx_spec = pl.BlockSpec((ROW_TILE, wa), lambda i: (jnp.minimum(i, nc - 1), 0))
    lat_spec = pl.BlockSpec((ROW_TILE, wa), lambda i: (jnp.maximum(i - nc, 0), 0))
    return pl.pallas_call(
        _outproj_kernel,
        grid=(N_TOK // ROW_TILE,),
        in_specs=[
            pl.BlockSpec((ROW_TILE, D_MODEL), lambda i: (i, 0)),
            pl.BlockSpec((1, N_MOD, D_MODEL), lambda i: (_cond_index(i, ROW_TILE), 0, 0)),
            ctx_spec, lat_spec, ctx_spec, lat_spec,
            pl.BlockSpec((2 * wa, D_MODEL), lambda i: (0, 0), pipeline_mode=pl.Buffered(1)),
        ],
        out_specs=pl.BlockSpec((ROW_TILE, D_MODEL), lambda i: (i, 0)),
        out_shape=jax.ShapeDtypeStruct((N_TOK, D_MODEL), F32),
        compiler_params=_params(("arbitrary",)),
        name="mixer_out_proj",
    )(x, mod, a_ctx, a_lat, b_ctx, b_lat, w)


def _rope(x, cos, sin):
    lane = lax.broadcasted_iota(jnp.int32, x.shape, 1)
    swapped = jnp.where((lane & 1) == 0, pltpu.roll(x, HEAD_DIM - 1, axis=1), pltpu.roll(x, 1, axis=1))
    return x * cos + swapped * sin


def _attend(q_heads, k, v_ones, o_ref, rows, col0):
    t = q_heads[0].shape[0]
    q = jnp.concatenate(q_heads, axis=0)
    s = lax.dot_general(q, k, (((1,), (1,)), ((), ())), preferred_element_type=F32) * (HEAD_DIM ** -0.5)
    e = jnp.exp(s - jnp.max(s, axis=-1, keepdims=True)).astype(BF16)
    oe = _dot(e, v_ones)
    o = oe[:, 0:HEAD_DIM] / oe[:, HEAD_DIM:]
    for i in range(len(q_heads)):
        o_ref[rows, col0 + i * HEAD_DIM:col0 + (i + 1) * HEAD_DIM] = o[i * t:(i + 1) * t].astype(o_ref.dtype)


def _attn_ctx_kernel(qkv_ref, qn_ref, kn_ref, o_ref, kc_ref, vc_ref):
    ones = jnp.ones((SEQ, HEAD_DIM), BF16)
    for s in range(ATTN_CTX_GROUP):
        rows = slice(s * SEQ, (s + 1) * SEQ)
        for kv in range(N_KV_HEADS):
            kcol = ATTN_W + kv * HEAD_DIM
            vcol = ATTN_W + KV_W + kv * HEAD_DIM
            k = _head_norm(qkv_ref[rows, kcol:kcol + HEAD_DIM], kn_ref[...])
            v = qkv_ref[rows, vcol:vcol + HEAD_DIM]
            kc_ref[rows, kv * HEAD_DIM:(kv + 1) * HEAD_DIM] = k
            vc_ref[rows, kv * HEAD_DIM:(kv + 1) * HEAD_DIM] = v
            qs = []
            for i in range(Q_PER_KV):
                qcol = (kv * Q_PER_KV + i) * HEAD_DIM
                qs.append(_head_norm(qkv_ref[rows, qcol:qcol + HEAD_DIM], qn_ref[...]).astype(BF16))
            _attend(qs, k.astype(BF16), jnp.concatenate([v.astype(BF16), ones], axis=1), o_ref, rows,
                    kv * Q_PER_KV * HEAD_DIM)


def _attn_ctx(qkv, qn, kn):
    rows = ATTN_CTX_GROUP * SEQ
    return pl.pallas_call(
        _attn_ctx_kernel,
        grid=(BATCH // ATTN_CTX_GROUP,),
        in_specs=[
            pl.BlockSpec((rows, QKV_W), lambda b: (b, 0)),
            pl.BlockSpec((1, HEAD_DIM), lambda b: (0, 0)),
            pl.BlockSpec((1, HEAD_DIM), lambda b: (0, 0)),
        ],
        out_specs=[
            pl.BlockSpec((rows, ATTN_W), lambda b: (b, 0)),
            pl.BlockSpec((rows, KV_W), lambda b: (b, 0)),
            pl.BlockSpec((rows, KV_W), lambda b: (b, 0)),
        ],
        out_shape=[
            jax.ShapeDtypeStruct((N_CTX, ATTN_W), BF16),
            jax.ShapeDtypeStruct((N_CTX, KV_W), F32),
            jax.ShapeDtypeStruct((N_CTX, KV_W), F32),
        ],
        compiler_params=_params(("arbitrary",)),
        name="attention_context",
    )(qkv, qn, kn)


def _attn_lat_kernel(qkv_ref, ck_ref, cv_ref, qn_ref, kn_ref, cos_ref, sin_ref, o_ref, k_scr, v_scr):
    qi = pl.program_id(1)

    @pl.when(qi == 0)
    def _():
        k_scr[0:PAST_LEN, :] = ck_ref[0].astype(BF16)
        for kv in range(N_KV_HEADS):
            kcol = ATTN_W + kv * HEAD_DIM
            vcol = ATTN_W + KV_W + kv * HEAD_DIM
            k = _head_norm(qkv_ref[:, kcol:kcol + HEAD_DIM], kn_ref[...])
            k = _rope(k, cos_ref[...], sin_ref[...])
            k_scr[PAST_LEN:, kv * HEAD_DIM:(kv + 1) * HEAD_DIM] = k.astype(BF16)
            v0 = 2 * kv * HEAD_DIM
            v_scr[0:PAST_LEN, v0:v0 + HEAD_DIM] = cv_ref[0, :, kv * HEAD_DIM:(kv + 1) * HEAD_DIM].astype(BF16)
            v_scr[PAST_LEN:, v0:v0 + HEAD_DIM] = qkv_ref[:, vcol:vcol + HEAD_DIM].astype(BF16)
            v_scr[:, v0 + HEAD_DIM:v0 + 2 * HEAD_DIM] = jnp.ones((PAST_LEN + DEC_SEQ, HEAD_DIM), BF16)

    r0 = pl.multiple_of(qi * Q_TILE, Q_TILE)
    cos = cos_ref[pl.ds(r0, Q_TILE), :]
    sin = sin_ref[pl.ds(r0, Q_TILE), :]
    for kv in range(N_KV_HEADS):
        qs = []
        for i in range(Q_PER_KV):
            qcol = (kv * Q_PER_KV + i) * HEAD_DIM
            q = _head_norm(qkv_ref[pl.ds(r0, Q_TILE), qcol:qcol + HEAD_DIM], qn_ref[...])
            qs.append(_rope(q, cos, sin).astype(BF16))
        _attend(qs, k_scr[:, kv * HEAD_DIM:(kv + 1) * HEAD_DIM], v_scr[:, 2 * kv * HEAD_DIM:2 * (kv + 1) * HEAD_DIM],
                o_ref, slice(None), kv * Q_PER_KV * HEAD_DIM)


def _attn_lat(qkv, cache_k, cache_v, qn, kn, cos, sin):
    row0 = N_CTX // DEC_SEQ
    qt = DEC_SEQ // Q_TILE
    return pl.pallas_call(
        _attn_lat_kernel,
        grid=(DEC_BATCH, qt),
        in_specs=[
            pl.BlockSpec((DEC_SEQ, QKV_W), lambda b, q: (row0 + b, 0)),
            pl.BlockSpec((1, PAST_LEN, KV_W), lambda b, q: (b, 0, 0)),
            pl.BlockSpec((1, PAST_LEN, KV_W), lambda b, q: (b, 0, 0)),
            pl.BlockSpec((1, HEAD_DIM), lambda b, q: (0, 0)),
            pl.BlockSpec((1, HEAD_DIM), lambda b, q: (0, 0)),
            pl.BlockSpec((DEC_SEQ, HEAD_DIM), lambda b, q: (0, 0)),
            pl.BlockSpec((DEC_SEQ, HEAD_DIM), lambda b, q: (0, 0)),
        ],
        out_specs=pl.BlockSpec((Q_TILE, ATTN_W), lambda b, q: (b * qt + q, 0)),
        out_shape=jax.ShapeDtypeStruct((N_LAT, ATTN_W), BF16),
        scratch_shapes=[pltpu.VMEM((PAST_LEN + DEC_SEQ, KV_W), BF16),
                        pltpu.VMEM((PAST_LEN + DEC_SEQ, 2 * KV_W), BF16)],
        compiler_params=_params(("arbitrary", "arbitrary")),
        name="attention_latent",
    )(qkv, cache_k, cache_v, qn, kn, cos, sin)


def _lru_kernel(l_ref, cw_ref, cb_ref, gw_ref, gb_ref, lam_ref, h0f_ref, h0b_ref, o_ref, hf_ref, hb_ref,
                af_scr, bf_scr, ab_scr, bb_scr, yf_scr, yb_scr, *, seq_len):
    group = l_ref.shape[0] // seq_len
    xc = _dwconv(l_ref[:, 0:LRU_W], cw_ref[...], cb_ref[...], LRU_CONV_LEFT, _RowShifter(seq_len))
    xcb = xc.astype(BF16)
    half_xc = 0.5 * xc
    half_rate = []
    for d in range(2):
        z = -lam_ref[d:d + 1, :]
        half_rate.append((0.5 * LRU_C) * (jnp.maximum(z, 0.0) + jnp.log1p(jnp.exp(-jnp.abs(z)))))
    half_gb = 0.5 * gb_ref[...]
    for hd in range(LRU_HEADS):
        cols = slice(hd * LRU_BLK, (hd + 1) * LRU_BLK)
        pre = _dot(xcb[:, cols], gw_ref[hd].astype(BF16))
        for d, (a_scr, b_scr) in enumerate(((af_scr, bf_scr), (ab_scr, bb_scr))):
            base = d * 2 * LRU_BLK
            tr = jnp.tanh(pre[:, base:base + LRU_BLK] + half_gb[2 * d:2 * d + 1, cols])
            ti = jnp.tanh(pre[:, base + LRU_BLK:base + 2 * LRU_BLK] + half_gb[2 * d + 1:2 * d + 2, cols])
            hr = half_rate[d][:, cols]
            neg_log_a = tr * hr + hr
            a = jnp.exp(-neg_log_a)
            a_scr[:, cols] = a
            var = jnp.tanh(neg_log_a) * (1.0 + a * a)
            b_scr[:, cols] = jnp.where(var > 0.0, var * lax.rsqrt(var), 0.0) * ((ti + 1.0) * half_xc[:, cols])

    n_tiles = seq_len // LRU_SCAN_ROWS

    def tile_step(k, carry):
        out = list(carry)
        base_f = pl.multiple_of(k * LRU_SCAN_ROWS, LRU_SCAN_ROWS)
        base_b = pl.multiple_of((n_tiles - 1 - k) * LRU_SCAN_ROWS, LRU_SCAN_ROWS)
        for g in range(group):
            rows_f = pl.ds(g * seq_len + base_f, LRU_SCAN_ROWS)
            rows_b = pl.ds(g * seq_len + base_b, LRU_SCAN_ROWS)
            af, bf, yf = af_scr.at[rows_f, :], bf_scr.at[rows_f, :], yf_scr.at[rows_f, :]
            ab, bb, yb = ab_scr.at[rows_b, :], bb_scr.at[rows_b, :], yb_scr.at[rows_b, :]
            hf, hb = out[2 * g], out[2 * g + 1]
            for r in range(LRU_SCAN_ROWS):
                rb = LRU_SCAN_ROWS - 1 - r
                hf = af[r:r + 1, :] * hf + bf[r:r + 1, :]
                hb = ab[rb:rb + 1, :] * hb + bb[rb:rb + 1, :]
                yf[r:r + 1, :] = hf
                yb[rb:rb + 1, :] = hb
            out[2 * g], out[2 * g + 1] = hf, hb
        return tuple(out)

    init = []
    for g in range(group):
        init += [h0f_ref[g], h0b_ref[g]]
    final = lax.fori_loop(0, n_tiles, tile_step, tuple(init))
    for g in range(group):
        hf_ref[g] = final[2 * g]
        hb_ref[g] = final[2 * g + 1]
    lg = l_ref[:, LRU_W:]
    gelu = 0.5 * lg * (1.0 + jnp.tanh(math.sqrt(2.0 / math.pi) * (lg + 0.044715 * (lg * lg * lg))))
    o_ref[...] = (gelu * (yf_scr[...] + yb_scr[...])).astype(o_ref.dtype)


def _lru(l_all, cw, cb, gw, gb, lam, h0f, h0b, seq_len, n_seq, row0, group):
    rows = seq_len * group
    blk0 = row0 // rows
    vec = lambda b: (0, 0)
    return pl.pallas_call(
        functools.partial(_lru_kernel, seq_len=seq_len),
        grid=(n_seq // group,),
        in_specs=[
            pl.BlockSpec((rows, 2 * LRU_W), lambda b: (blk0 + b, 0)),
            pl.BlockSpec((LRU_CONV, LRU_W), vec),
            pl.BlockSpec((1, LRU_W), vec),
            pl.BlockSpec((LRU_HEADS, LRU_BLK, 4 * LRU_BLK), lambda b: (0, 0, 0)),
            pl.BlockSpec((4, LRU_W), vec),
            pl.BlockSpec((2, LRU_W), vec),
            pl.BlockSpec((group, 1, LRU_W), lambda b: (b, 0, 0)),
            pl.BlockSpec((group, 1, LRU_W), lambda b: (b, 0, 0)),
        ],
        out_specs=[
            pl.BlockSpec((rows, LRU_W), lambda b: (b, 0)),
            pl.BlockSpec((group, 1, LRU_W), lambda b: (b, 0, 0)),
            pl.BlockSpec((group, 1, LRU_W), lambda b: (b, 0, 0)),
        ],
        out_shape=[
            jax.ShapeDtypeStruct((n_seq * seq_len, LRU_W), BF16),
            jax.ShapeDtypeStruct((n_seq, 1, LRU_W), F32),
            jax.ShapeDtypeStruct((n_seq, 1, LRU_W), F32),
        ],
        scratch_shapes=[pltpu.VMEM((rows, LRU_W), F32)] * 6,
        compiler_params=_params(("arbitrary",)),
        name="rg_lru",
    )(l_all, cw, cb, gw, gb, lam, h0f, h0b)


def _hy_filter_kernel(z_ref, t_ref, ckh_ref, ckl_ref, alt_ref, w1_ref, b1_ref, w2_ref, b2_ref, w3_ref, fr_ref,
                      dec_ref, kf_ref, kn_ref):
    z = jnp.sin(fr_ref[0:1, :] * (_dot_f32(z_ref[...], w1_ref[...]) + b1_ref[...]))
    z = jnp.sin(fr_ref[1:2, :] * (_dot_f32(z, w2_ref[...]) + b2_ref[...]))
    zh, zl = _split_bf16(z)
    filt = _dot3(zh, zl, w3_ref[...]) * jnp.exp(-t_ref[...] * jnp.abs(dec_ref[...]))
    filt = filt / jnp.sum(jnp.abs(filt), axis=0, keepdims=True)
    kf = _dot3(ckh_ref[...], ckl_ref[...], filt)
    kn = jnp.sum(alt_ref[...] * filt, axis=0, keepdims=True)
    for o in range(HY_ORDER):
        kf_ref[o] = kf[:, o * HY_W:(o + 1) * HY_W]
        kn_ref[o:o + 1, :] = kn[:, o * HY_W:(o + 1) * HY_W]


def _hy_filter(consts, w1, b1, w2, b2, w3, freq, decay, seq_len):
    ckh, ckl = _split_bf16(consts["ck"])
    return pl.pallas_call(
        _hy_filter_kernel,
        out_shape=[jax.ShapeDtypeStruct((HY_ORDER, seq_len, HY_W), F32), jax.ShapeDtypeStruct((HY_ORDER, HY_W), F32)],
        compiler_params=pltpu.CompilerParams(vmem_limit_bytes=VMEM_LIMIT_BYTES),
        name="hyena_filter",
    )(consts["z"], consts["t"], ckh, ckl, consts["alt_w"], w1, b1, w2, b2, w3, freq, decay)


def _longconv(u, fwd_ref, inv_ref, kf2, kn, alt, skip):
    n = u.shape[0]
    spec = _dot(fwd_ref[...], u.astype(BF16)) * kf2
    y = _dot(inv_ref[...], spec.astype(BF16))
    nyq = jnp.sum(alt * u, axis=0, keepdims=True) * kn * (1.0 / (2 * n))
    return y + alt * nyq + u * skip


def _hyena_kernel(x1_ref, x2_ref, v_ref, cw_ref, cb_ref, kf_ref, kn_ref, skip_ref, fwd_ref, inv_ref, alt_ref, o_ref,
                  *, seq_len, chain_w):
    shift = _RowShifter(seq_len)
    alt = alt_ref[...]
    v = _dwconv(v_ref[...], cw_ref[2], cb_ref[2:3, :], HY_CONV_LEFT, shift)
    gates = [_dwconv(g_ref[...], cw_ref[o], cb_ref[o:o + 1, :], HY_CONV_LEFT, shift)
             for o, g_ref in enumerate((x1_ref, x2_ref))]
    for s in range(v.shape[0] // seq_len):
        rows = slice(s * seq_len, (s + 1) * seq_len)
        for c0 in range(0, v.shape[1], chain_w):
            cols = slice(c0, c0 + chain_w)
            z = v[rows, cols]
            for o in range(HY_ORDER):
                kfo = kf_ref[o, :, cols]
                z = gates[o][rows, cols] * _longconv(z, fwd_ref, inv_ref, jnp.concatenate([kfo, kfo], axis=0),
                                                     kn_ref[o:o + 1, cols], alt, skip_ref[o:o + 1, cols])
            o_ref[rows, cols] = z.astype(o_ref.dtype)


def _hyena(hy_all, cw, cb, kf, kn, skip, consts, seq_len, n_seq, row0, group, block_w, chain_w):
    rows = seq_len * group
    blk0 = row0 // rows
    nc = HY_W // block_w
    const = dict(pipeline_mode=pl.Buffered(1))
    in_specs = [pl.BlockSpec((rows, block_w), lambda b, c, g=g: (blk0 + b, g * nc + c)) for g in range(3)]
    in_specs += [
        pl.BlockSpec((HY_ORDER + 1, HY_CONV, block_w), lambda b, c: (0, 0, c)),
        pl.BlockSpec((HY_ORDER + 1, block_w), lambda b, c: (0, c)),
        pl.BlockSpec((HY_ORDER, seq_len, block_w), lambda b, c: (0, 0, c)),
        pl.BlockSpec((HY_ORDER, block_w), lambda b, c: (0, c)),
        pl.BlockSpec((HY_ORDER, block_w), lambda b, c: (0, c)),
        pl.BlockSpec((2 * seq_len, seq_len), lambda b, c: (0, 0), **const),
        pl.BlockSpec((seq_len, 2 * seq_len), lambda b, c: (0, 0), **const),
        pl.BlockSpec((seq_len, 1), lambda b, c: (0, 0)),
    ]
    return pl.pallas_call(
        functools.partial(_hyena_kernel, seq_len=seq_len, chain_w=chain_w),
        grid=(n_seq // group, nc),
        in_specs=in_specs,
        out_specs=pl.BlockSpec((rows, block_w), lambda b, c: (b, c)),
        out_shape=jax.ShapeDtypeStruct((n_seq * seq_len, HY_W), BF16),
        compiler_params=_params(("arbitrary", "arbitrary")),
        name="hyena",
    )(hy_all, hy_all, hy_all, cw.reshape(HY_CONV, HY_ORDER + 1, HY_W).transpose(1, 0, 2),
      cb.reshape(HY_ORDER + 1, HY_W), kf, kn, skip, consts["fwd"].astype(BF16), consts["inv"].astype(BF16),
      consts["alt"])


def _pool_kernel(p_ref, w_ref, s_ref, o_ref, *, seq_len):
    shift = _RowShifter(seq_len)
    t = lax.broadcasted_iota(jnp.int32, (p_ref.shape[0], 1), 0) & (seq_len - 1)
    for gi, win in enumerate(POOL_WINDOWS):
        cols = slice(gi * POOL_GW, (gi + 1) * POOL_GW)
        x = p_ref[:, cols]
        half = win // 2
        back, fwd = x, x
        m = 1
        while m < half:
            back = back + shift(back, -m)
            fwd = fwd + shift(fwd, m)
            m *= 2
        s = shift(back, -1) + fwd
        cnt = (jnp.minimum(t + half, seq_len) - jnp.maximum(t - half, 0)).astype(F32)
        y = _dot((s / cnt - x).astype(BF16), w_ref[gi].astype(BF16))
        o_ref[:, cols] = (y * s_ref[:, cols]).astype(o_ref.dtype)


def _pool(p_all, w, scale, seq_len, n_seq, row0, group):
    rows = seq_len * group
    blk0 = row0 // rows
    return pl.pallas_call(
        functools.partial(_pool_kernel, seq_len=seq_len),
        grid=(n_seq // group,),
        in_specs=[
            pl.BlockSpec((rows, POOL_W), lambda b: (blk0 + b, 0)),
            pl.BlockSpec((len(POOL_WINDOWS), POOL_GW, POOL_GW), lambda b: (0, 0, 0)),
            pl.BlockSpec((1, POOL_W), lambda b: (0, 0)),
        ],
        out_specs=pl.BlockSpec((rows, POOL_W), lambda b: (b, 0)),
        out_shape=jax.ShapeDtypeStruct((n_seq * seq_len, POOL_W), BF16),
        compiler_params=_params(("arbitrary",)),
        name="multi_pool",
    )(p_all, w, scale)


def _rope_tables():
    rows = DEC_SEQ // GRID_W
    r = np.repeat(np.arange(rows), GRID_W).astype(np.float64)
    col = np.tile(np.arange(GRID_W), rows).astype(np.float64)
    half = HEAD_DIM // 2
    inv = ROPE_THETA ** (-np.arange(0, half, 2, dtype=np.float64) / half)
    ang = np.concatenate([r[:, None] * inv, col[:, None] * inv], axis=-1)
    cos = np.repeat(np.cos(ang), 2, axis=-1)
    sin = np.repeat(np.sin(ang), 2, axis=-1) * np.tile(np.array([-1.0, 1.0]), half)
    return jnp.asarray(cos, F32), jnp.asarray(sin, F32)


def _hyena_tables(n):
    idx = np.arange(n, dtype=np.float64)
    t = idx / max(n - 1, 1)
    bands = np.linspace(1e-4, HY_BANDS - 1, HY_BANDS)
    f = 2.0 * math.pi * idx[:, None] * bands[None, :] / n
    z = np.zeros((n, HY_EMB_PAD))
    z[:, 0] = t
    z[:, 1:1 + HY_BANDS] = np.cos(f)
    z[:, 1 + HY_BANDS:HY_EMB] = -np.sin(f)
    ang = math.pi * ((idx[:, None] * idx[None, :]) % (2 * n)) / n
    wgt_n = np.where(idx == 0, 1.0, 2.0)
    alt = np.where(idx % 2 == 0, 1.0, -1.0)
    fwd = np.concatenate([np.cos(ang), np.sin(ang)], axis=0)
    inv = np.concatenate([np.cos(ang) * wgt_n[None, :], np.sin(ang) * wgt_n[None, :]], axis=1) / (2 * n)
    return {
        "z": jnp.asarray(z, F32),
        "t": jnp.asarray(t[:, None], F32),
        "ck": jnp.asarray(np.cos(ang) * wgt_n[None, :], F32),
        "alt_w": jnp.asarray((alt * wgt_n)[:, None], F32),
        "alt": jnp.asarray(alt[:, None], F32),
        "fwd": jnp.asarray(fwd, F32),
        "inv": jnp.asarray(inv, F32),
    }


def kernel(x_prompt, x_sample, cache_k, cache_v, state_lru_fwd, state_lru_bwd, c, c_ctx, norm_g, w_mod, b_mod, ffn_w13, ffn_w2, ab_w_in, ab_q_norm, ab_k_norm, lru_conv_w, lru_conv_b, lru_gate_w, lru_gate_b, lru_lambda, ab_w_out, cd_w_in, hy_conv_w, hy_conv_b, hy_w1, hy_b1, hy_w2, hy_b2, hy_w3, hy_freq, hy_decay, hy_skip, pool_w, pool_scale, cd_w_out):
    xs = [x_prompt.reshape(N_CTX, D_MODEL), x_sample.reshape(N_LAT, D_MODEL)]
    cond = jnp.concatenate([c_ctx[None], c, jnp.zeros((COND_PAD - N_COND, D_MODEL), F32)], axis=0)
    mods = _modulation(cond, w_mod, b_mod)[:, :N_COND].reshape(DEPTH, N_COND, N_MOD, D_MODEL)
    rope_cos, rope_sin = _rope_tables()

    k_list, v_list, hf_list, hb_list = [], [], [], []
    for l in range(DEPTH):
        mod = mods[l]
        g = norm_g[l]
        (x,) = _ffn(xs, mod, g[0:1], ffn_w13, ffn_w2, l, 0, 0, [N_TOK])
        if l % 2 == 0:
            e = l // 2
            qkv, lxg = _inproj(x, mod, g[1:2], ab_w_in[e].astype(BF16), (QKV_W, 2 * LRU_W))
            qn, kn = ab_q_norm[e][None], ab_k_norm[e][None]
            attn_c, kc, vc = _attn_ctx(qkv, qn, kn)
            attn_l = _attn_lat(qkv, cache_k[:, e].reshape(DEC_BATCH, PAST_LEN, KV_W),
                               cache_v[:, e].reshape(DEC_BATCH, PAST_LEN, KV_W), qn, kn, rope_cos, rope_sin)
            gw = 0.5 * jnp.transpose(lru_gate_w[e], (2, 3, 0, 1, 4)).reshape(LRU_HEADS, LRU_BLK, 4 * LRU_BLK)
            gb = lru_gate_b[e].reshape(4, LRU_W)
            lru_args = (lru_conv_w[e], lru_conv_b[e][None], gw, gb, lru_lambda[e])
            zeros = jnp.zeros((BATCH, 1, LRU_W), F32)
            rec_c, hf, hb = _lru(lxg, *lru_args, zeros, zeros, SEQ, BATCH, 0, LRU_CTX_GROUP)
            rec_l, _, _ = _lru(lxg, *lru_args, state_lru_fwd[:, e][:, None], state_lru_bwd[:, e][:, None],
                               DEC_SEQ, DEC_BATCH, N_CTX, 1)
            x = _outproj(x, mod, attn_c, attn_l, rec_c, rec_l, ab_w_out[e].astype(BF16))
            k_list.append(kc.reshape(BATCH, SEQ, N_KV_HEADS, HEAD_DIM))
            v_list.append(vc.reshape(BATCH, SEQ, N_KV_HEADS, HEAD_DIM))
            hf_list.append(hf.reshape(BATCH, LRU_W))
            hb_list.append(hb.reshape(BATCH, LRU_W))
        else:
            o = l // 2
            hy, pw = _inproj(x, mod, g[1:2], cd_w_in[o].astype(BF16), ((HY_ORDER + 1) * HY_W, POOL_W))
            w1 = jnp.zeros((HY_EMB_PAD, HY_FH), F32).at[:HY_EMB].set(hy_w1[o])
            z_out, p_out = [], []
            for seq_len, n_seq, row0, hy_cfg, pool_group in (
                    (SEQ, BATCH, 0, (HY_CTX_GROUP, HY_W, HY_W), POOL_CTX_GROUP),
                    (DEC_SEQ, DEC_BATCH, N_CTX, (1, HY_LAT_BLOCK_W, HY_CHAIN_W), 1)):
                consts = _hyena_tables(seq_len)
                kf, kn = _hy_filter(consts, w1, hy_b1[o][None], hy_w2[o], hy_b2[o][None], hy_w3[o], hy_freq[o],
                                    hy_decay[o][None], seq_len)
                z_out.append(_hyena(hy, hy_conv_w[o], hy_conv_b[o], kf, kn, hy_skip[o], consts,
                                    seq_len, n_seq, row0, *hy_cfg))
                p_out.append(_pool(pw, pool_w[o], pool_scale[o][None], seq_len, n_seq, row0, pool_group))
            x = _outproj(x, mod, z_out[0], z_out[1], p_out[0], p_out[1], cd_w_out[o].astype(BF16))
        xs = _ffn([x], mod, g[2:3], ffn_w13, ffn_w2, l, 1, 2, [N_TOK] if l + 1 < DEPTH else [N_CTX, N_LAT])

    y_prompt = xs[0].reshape(BATCH, SEQ, D_MODEL)
    y_sample = xs[1].reshape(DEC_BATCH, DEC_SEQ, D_MODEL)
    return (y_prompt, y_sample, jnp.stack(k_list, axis=1), jnp.stack(v_list, axis=1),
            jnp.stack(hf_list, axis=1), jnp.stack(hb_list, axis=1))
```

```python
import functools
import math

import numpy as np
import jax
import jax.numpy as jnp
from jax import lax
from jax.experimental import pallas as pl
from jax.experimental.pallas import tpu as pltpu

F32 = jnp.float32
BF16 = jnp.bfloat16

D_MODEL = 2048
BATCH = 32
SEQ = 256
DEPTH = 2
DEC_BATCH = 2
DEC_SEQ = 1024
PAST_LEN = 512
GRID_W = 64
N_MOD = 9
NORM_EPS = 1e-6
D_FF = 5632
HEAD_DIM = 128
N_Q_HEADS = 8
N_KV_HEADS = 2
Q_PER_KV = N_Q_HEADS // N_KV_HEADS
ATTN_W = N_Q_HEADS * HEAD_DIM
KV_W = N_KV_HEADS * HEAD_DIM
QKV_W = ATTN_W + 2 * KV_W
ROPE_THETA = 10000.0
LRU_W = 1024
LRU_HEADS = 8
LRU_BLK = LRU_W // LRU_HEADS
LRU_CONV = 4
LRU_CONV_LEFT = 2
LRU_C = 8.0
HY_W = 1024
HY_ORDER = 2
HY_CONV = 3
HY_CONV_LEFT = 1
HY_EMB = 33
HY_EMB_PAD = 128
HY_BANDS = (HY_EMB - 1) // 2
HY_FH = 64
POOL_W = 1024
POOL_WINDOWS = (2, 4, 8, 16)
POOL_GW = POOL_W // len(POOL_WINDOWS)

N_CTX = BATCH * SEQ
N_LAT = DEC_BATCH * DEC_SEQ
N_TOK = N_CTX + N_LAT
N_COND = 1 + DEC_BATCH
COND_PAD = 8

VMEM_LIMIT_BYTES = 56 * 1024 * 1024

ROW_TILE = 512
FFN_ROW_TILE = 1024
FF_TILE = 512
FFN_NORM_ROWS = 32
MOD_TILE = 2048
Q_TILE = 256
ATTN_CTX_GROUP = 4
LRU_CTX_GROUP = 2
LRU_SCAN_ROWS = 8
HY_CTX_GROUP = 2
POOL_CTX_GROUP = 4
HY_CHAIN_W = 512
HY_LAT_BLOCK_W = 512


def _params(sem):
    return pltpu.CompilerParams(dimension_semantics=sem, vmem_limit_bytes=VMEM_LIMIT_BYTES)


def _cond_index(i, tile):
    return jnp.maximum((i * tile) // DEC_SEQ - (N_CTX // DEC_SEQ - 1), 0)


def _sigmoid(x):
    return 1.0 / (1.0 + jnp.exp(-x))


def _dot(a, b):
    return jnp.dot(a, b, preferred_element_type=F32)


def _dot_f32(a, b):
    return jnp.dot(a, b, preferred_element_type=F32, precision=lax.Precision.HIGHEST)


def _split_bf16(x):
    hi = x.astype(BF16)
    return hi, (x - hi.astype(F32)).astype(BF16)


def _dot3(a_hi, a_lo, b):
    b_hi, b_lo = _split_bf16(b)
    return _dot(a_hi, b_hi) + (_dot(a_hi, b_lo) + _dot(a_lo, b_hi))


def _modulated_norm(x, g, mod_ref, j):
    ms = jnp.mean(x * x, axis=-1, keepdims=True)
    y = x * lax.rsqrt(ms + NORM_EPS) * g
    return y * (1.0 + mod_ref[0, 3 * j + 1:3 * j + 2, :]) + mod_ref[0, 3 * j:3 * j + 1, :]


def _head_norm(x, g):
    ms = jnp.mean(x * x, axis=-1, keepdims=True)
    return x * lax.rsqrt(ms + NORM_EPS) * g


class _RowShifter:
    def __init__(self, period):
        self.period = period
        self._masks = {}

    def __call__(self, x, off):
        if off == 0:
            return x
        key = (x.shape, off)
        if key not in self._masks:
            t = lax.broadcasted_iota(jnp.int32, x.shape, 0) & (self.period - 1)
            self._masks[key] = t >= -off if off < 0 else t < self.period - off
        return jnp.where(self._masks[key], pltpu.roll(x, (-off) % x.shape[0], axis=0), 0.0)


def _dwconv(x, w, b, left, shift):
    acc = b + shift(x, -left) * w[0:1, :]
    for j in range(1, w.shape[0]):
        acc = acc + shift(x, j - left) * w[j:j + 1, :]
    return acc


def _mod_kernel(c_ref, w_ref, b_ref, o_ref):
    c = c_ref[...]
    s = (c * _sigmoid(c)).astype(BF16)
    o_ref[0] = _dot(s, w_ref[0].astype(BF16)) + b_ref[0]


def _modulation(cond, w_mod, b_mod):
    n = N_MOD * D_MODEL
    return pl.pallas_call(
        _mod_kernel,
        grid=(DEPTH, n // MOD_TILE),
        in_specs=[
            pl.BlockSpec((COND_PAD, D_MODEL), lambda l, j: (0, 0)),
            pl.BlockSpec((1, D_MODEL, MOD_TILE), lambda l, j: (l, 0, j)),
            pl.BlockSpec((1, 1, MOD_TILE), lambda l, j: (l, 0, j)),
        ],
        out_specs=pl.BlockSpec((1, COND_PAD, MOD_TILE), lambda l, j: (l, 0, j)),
        out_shape=jax.ShapeDtypeStruct((DEPTH, COND_PAD, n), F32),
        compiler_params=_params(("arbitrary", "arbitrary")),
        name="modulation",
    )(cond, w_mod, b_mod.reshape(DEPTH, 1, n))


def _ffn_kernel(*refs, j, n_in, n_out, in_split, out_split):
    x_hbm = refs[:n_in]
    mod_ref, g_ref, w1_ref, w3_ref, w2_ref = refs[n_in:n_in + 5]
    o_hbm = refs[n_in + 5:n_in + 5 + n_out]
    buf, h_ref, sem_in, sem_out = refs[n_in + 5 + n_out:]
    i, f = pl.program_id(0), pl.program_id(1)
    n, nf = pl.num_programs(0), pl.num_programs(1)
    slot = lax.rem(i, 2)
    acc = buf.at[slot]

    def tile_rows(t):
        return pl.ds(pl.multiple_of(t * FFN_ROW_TILE, FFN_ROW_TILE), FFN_ROW_TILE)

    def in_copy(src, t, s):
        return pltpu.make_async_copy(src.at[tile_rows(t), :], buf.at[s], sem_in.at[s])

    def out_copy(dst, t, s):
        return pltpu.make_async_copy(buf.at[s], dst.at[tile_rows(t), :], sem_out.at[s])

    def start_split(copy, refs_, split, t, s):
        if len(refs_) == 1:
            copy(refs_[0], t, s).start()
        else:
            @pl.when(t < split)
            def _():
                copy(refs_[0], t, s).start()

            @pl.when(t >= split)
            def _():
                copy(refs_[1], t - split, s).start()

    @pl.when(f == 0)
    def _():
        @pl.when(i == 0)
        def _():
            start_split(in_copy, x_hbm, in_split, i, slot)

        in_copy(x_hbm[0], 0, slot).wait()
        gain = g_ref[...] * (1.0 + mod_ref[0, 3 * j + 1:3 * j + 2, :])
        shift = mod_ref[0, 3 * j:3 * j + 1, :]

        def chunk(r, carry):
            rows = pl.ds(pl.multiple_of(r * FFN_NORM_ROWS, FFN_NORM_ROWS), FFN_NORM_ROWS)
            x = acc[rows, :]
            ms = jnp.mean(x * x, axis=-1, keepdims=True)
            h_ref[rows, :] = (x * lax.rsqrt(ms + NORM_EPS) * gain + shift).astype(BF16)
            return carry

        lax.fori_loop(0, FFN_ROW_TILE // FFN_NORM_ROWS, chunk, 0, unroll=4)

    @pl.when((f == 1) & (i + 1 < n))
    def _():
        @pl.when(i >= 1)
        def _():
            out_copy(o_hbm[0], 0, 1 - slot).wait()

        start_split(in_copy, x_hbm, in_split, i + 1, 1 - slot)

    h = h_ref[...]
    gt = _dot(h, w1_ref[...].astype(BF16))
    up = _dot(h, w3_ref[...].astype(BF16))
    a = (gt * _sigmoid(gt) * up).astype(BF16)
    half_gate = 0.5 * mod_ref[0, 3 * j + 2:3 * j + 3, :]
    acc[...] += half_gate * _dot(a, w2_ref[...].astype(BF16))

    @pl.when(f == nf - 1)
    def _():
        start_split(out_copy, o_hbm, out_split, i, slot)

        @pl.when(i == n - 1)
        def _():
            out_copy(o_hbm[0], 0, slot).wait()
            out_copy(o_hbm[0], 0, 1 - slot).wait()


def _ffn(xs, mod, g, w13, w2, l, sub, j, out_rows):
    nf = D_FF // FF_TILE
    n_tiles = N_TOK // FFN_ROW_TILE
    assert nf >= 2 and n_tiles >= 2 and sum(x.shape[0] for x in xs) == N_TOK and sum(out_rows) == N_TOK
    assert all(x.shape[0] % FFN_ROW_TILE == 0 for x in xs) and all(r % FFN_ROW_TILE == 0 for r in out_rows)
    any_spec = pl.BlockSpec(memory_space=pl.ANY)
    outs = pl.pallas_call(
        functools.partial(_ffn_kernel, j=j, n_in=len(xs), n_out=len(out_rows),
                          in_split=xs[0].shape[0] // FFN_ROW_TILE, out_split=out_rows[0] // FFN_ROW_TILE),
        grid=(n_tiles, nf),
        in_specs=[any_spec] * len(xs) + [
            pl.BlockSpec((1, N_MOD, D_MODEL), lambda i, f: (_cond_index(i, FFN_ROW_TILE), 0, 0)),
            pl.BlockSpec((1, D_MODEL), lambda i, f: (0, 0)),
            pl.BlockSpec((None, None, D_MODEL, FF_TILE), lambda i, f: (l, sub, 0, f)),
            pl.BlockSpec((None, None, D_MODEL, FF_TILE), lambda i, f: (l, sub, 0, nf + f)),
            pl.BlockSpec((None, None, FF_TILE, D_MODEL), lambda i, f: (l, sub, f, 0)),
        ],
        out_specs=[any_spec] * len(out_rows),
        out_shape=[jax.ShapeDtypeStruct((r, D_MODEL), F32) for r in out_rows],
        scratch_shapes=[
            pltpu.VMEM((2, FFN_ROW_TILE, D_MODEL), F32),
            pltpu.VMEM((FFN_ROW_TILE, D_MODEL), BF16),
            pltpu.SemaphoreType.DMA((2,)),
            pltpu.SemaphoreType.DMA((2,)),
        ],
        compiler_params=_params(("arbitrary", "arbitrary")),
        name="ffn",
    )(*xs, mod, g, w13, w13, w2)
    return outs


W_STAGE_ROWS = 256


def _stage_weight_bf16(w_hbm, w_scr, stage, sem):
    n_chunks = w_hbm.shape[0] // W_STAGE_ROWS

    def copy(c, s):
        return pltpu.make_async_copy(w_hbm.at[pl.ds(c * W_STAGE_ROWS, W_STAGE_ROWS), :], stage.at[s], sem.at[s])

    copy(0, 0).start()
    for c in range(n_chunks):
        s = c % 2
        if c + 1 < n_chunks:
            copy(c + 1, 1 - s).start()
        copy(c, s).wait()
        w_scr[c * W_STAGE_ROWS:(c + 1) * W_STAGE_ROWS, :] = stage[s].astype(BF16)


def _weight_scratch(k, n):
    return [pltpu.VMEM((k, n), BF16), pltpu.VMEM((2, W_STAGE_ROWS, n), F32), pltpu.SemaphoreType.DMA((2,))]


IN_CHUNK = 512


def _inproj_kernel(x_ref, mod_ref, g_ref, w_hbm, *rest, widths):
    o_refs, (w_ref, stage, sem) = rest[:len(widths)], rest[len(widths):]

    @pl.when(pl.program_id(0) == 0)
    def _():
        _stage_weight_bf16(w_hbm, w_ref, stage, sem)

    h = _modulated_norm(x_ref[...], g_ref[...], mod_ref, 1).astype(BF16)
    off = 0
    for o_ref, wd in zip(o_refs, widths):
        for c0 in range(0, wd, IN_CHUNK):
            o_ref[:, c0:c0 + IN_CHUNK] = _dot(h, w_ref[:, off + c0:off + c0 + IN_CHUNK])
        off += wd


def _inproj(x, mod, g, w, widths):
    n = sum(widths)
    return pl.pallas_call(
        functools.partial(_inproj_kernel, widths=widths),
        grid=(N_TOK // ROW_TILE,),
        in_specs=[
            pl.BlockSpec((ROW_TILE, D_MODEL), lambda i: (i, 0)),
            pl.BlockSpec((1, N_MOD, D_MODEL), lambda i: (_cond_index(i, ROW_TILE), 0, 0)),
            pl.BlockSpec((1, D_MODEL), lambda i: (0, 0)),
            pl.BlockSpec(memory_space=pl.ANY),
        ],
        out_specs=[pl.BlockSpec((ROW_TILE, wd), lambda i: (i, 0)) for wd in widths],
        out_shape=[jax.ShapeDtypeStruct((N_TOK, wd), F32) for wd in widths],
        scratch_shapes=_weight_scratch(D_MODEL, n),
        compiler_params=_params(("arbitrary",)),
        name="mixer_in_proj",
    )(x, mod, g, w)


def _outproj_kernel(x_ref, mod_ref, ac_ref, al_ref, bc_ref, bl_ref, w_hbm, o_ref, w_ref, stage, sem):
    @pl.when(pl.program_id(0) == 0)
    def _():
        _stage_weight_bf16(w_hbm, w_ref, stage, sem)

    wa = w_ref.shape[0] // 2
    is_ctx = pl.program_id(0) < N_CTX // ROW_TILE
    a = jnp.where(is_ctx, ac_ref[...], al_ref[...])
    b = jnp.where(is_ctx, bc_ref[...], bl_ref[...])
    y = _dot(a, w_ref[0:wa, :]) + _dot(b, w_ref[wa:, :])
    o_ref[...] = x_ref[...] + mod_ref[0, 5:6, :] * y


def _outproj(x, mod, a_ctx, a_lat, b_ctx, b_lat, w):
    wa = a_ctx.shape[1]
    nc = N_CTX // ROW_TILE
    ctx_spec = pl.BlockSpec((ROW_TILE, wa), lambda i: (jnp.minimum(i, nc - 1), 0))
    lat_spec = pl.BlockSpec((ROW_TILE, wa), lambda i: (jnp.maximum(i - nc, 0), 0))
    return pl.pallas_call(
        _outproj_kernel,
        grid=(N_TOK // ROW_TILE,),
        in_specs=[
            pl.BlockSpec((ROW_TILE, D_MODEL), lambda i: (i, 0)),
            pl.BlockSpec((1, N_MOD, D_MODEL), lambda i: (_cond_index(i, ROW_TILE), 0, 0)),
            ctx_spec, lat_spec, ctx_spec, lat_spec,
            pl.BlockSpec(memory_space=pl.ANY),
        ],
        out_specs=pl.BlockSpec((ROW_TILE, D_MODEL), lambda i: (i, 0)),
        out_shape=jax.ShapeDtypeStruct((N_TOK, D_MODEL), F32),
        scratch_shapes=_weight_scratch(2 * wa, D_MODEL),
        compiler_params=_params(("arbitrary",)),
        name="mixer_out_proj",
    )(x, mod, a_ctx, a_lat, b_ctx, b_lat, w)


def _rope(x, cos, sin):
    lane = lax.broadcasted_iota(jnp.int32, x.shape, 1)
    swapped = jnp.where((lane & 1) == 0, pltpu.roll(x, HEAD_DIM - 1, axis=1), pltpu.roll(x, 1, axis=1))
    return x * cos + swapped * sin


def _attend(q_heads, k, v_ones, o_ref, rows, col0):
    t = q_heads[0].shape[0]
    q = jnp.concatenate(q_heads, axis=0)
    s = lax.dot_general(q, k, (((1,), (1,)), ((), ())), preferred_element_type=F32) * (HEAD_DIM ** -0.5)
    e = jnp.exp(s - jnp.max(s, axis=-1, keepdims=True)).astype(BF16)
    oe = _dot(e, v_ones)
    o = oe[:, 0:HEAD_DIM] / oe[:, HEAD_DIM:]
    for i in range(len(q_heads)):
        o_ref[rows, col0 + i * HEAD_DIM:col0 + (i + 1) * HEAD_DIM] = o[i * t:(i + 1) * t].astype(o_ref.dtype)


def _attn_ctx_kernel(qkv_ref, qn_ref, kn_ref, o_ref, kc_ref, vc_ref):
    ones = jnp.ones((SEQ, HEAD_DIM), BF16)
    for s in range(ATTN_CTX_GROUP):
        rows = slice(s * SEQ, (s + 1) * SEQ)
        for kv in range(N_KV_HEADS):
            kcol = ATTN_W + kv * HEAD_DIM
            vcol = ATTN_W + KV_W + kv * HEAD_DIM
            k = _head_norm(qkv_ref[rows, kcol:kcol + HEAD_DIM], kn_ref[...])
            v = qkv_ref[rows, vcol:vcol + HEAD_DIM]
            kc_ref[rows, kv * HEAD_DIM:(kv + 1) * HEAD_DIM] = k
            vc_ref[rows, kv * HEAD_DIM:(kv + 1) * HEAD_DIM] = v
            qs = []
            for i in range(Q_PER_KV):
                qcol = (kv * Q_PER_KV + i) * HEAD_DIM
                qs.append(_head_norm(qkv_ref[rows, qcol:qcol + HEAD_DIM], qn_ref[...]).astype(BF16))
            _attend(qs, k.astype(BF16), jnp.concatenate([v.astype(BF16), ones], axis=1), o_ref, rows,
                    kv * Q_PER_KV * HEAD_DIM)


def _attn_ctx(qkv, qn, kn):
    rows = ATTN_CTX_GROUP * SEQ
    return pl.pallas_call(
        _attn_ctx_kernel,
        grid=(BATCH // ATTN_CTX_GROUP,),
        in_specs=[
            pl.BlockSpec((rows, QKV_W), lambda b: (b, 0)),
            pl.BlockSpec((1, HEAD_DIM), lambda b: (0, 0)),
            pl.BlockSpec((1, HEAD_DIM), lambda b: (0, 0)),
        ],
        out_specs=[
            pl.BlockSpec((rows, ATTN_W), lambda b: (b, 0)),
            pl.BlockSpec((rows, KV_W), lambda b: (b, 0)),
            pl.BlockSpec((rows, KV_W), lambda b: (b, 0)),
        ],
        out_shape=[
            jax.ShapeDtypeStruct((N_CTX, ATTN_W), BF16),
            jax.ShapeDtypeStruct((N_CTX, KV_W), F32),
            jax.ShapeDtypeStruct((N_CTX, KV_W), F32),
        ],
        compiler_params=_params(("arbitrary",)),
        name="attention_context",
    )(qkv, qn, kn)


def _attn_lat_kernel(qkv_ref, ck_ref, cv_ref, qn_ref, kn_ref, cos_ref, sin_ref, o_ref, k_scr, v_scr):
    qi = pl.program_id(1)

    @pl.when(qi == 0)
    def _():
        k_scr[0:PAST_LEN, :] = ck_ref[0].astype(BF16)
        for kv in range(N_KV_HEADS):
            kcol = ATTN_W + kv * HEAD_DIM
            vcol = ATTN_W + KV_W + kv * HEAD_DIM
            k = _head_norm(qkv_ref[:, kcol:kcol + HEAD_DIM], kn_ref[...])
            k = _rope(k, cos_ref[...], sin_ref[...])
            k_scr[PAST_LEN:, kv * HEAD_DIM:(kv + 1) * HEAD_DIM] = k.astype(BF16)
            v0 = 2 * kv * HEAD_DIM
            v_scr[0:PAST_LEN, v0:v0 + HEAD_DIM] = cv_ref[0, :, kv * HEAD_DIM:(kv + 1) * HEAD_DIM].astype(BF16)
            v_scr[PAST_LEN:, v0:v0 + HEAD_DIM] = qkv_ref[:, vcol:vcol + HEAD_DIM].astype(BF16)
            v_scr[:, v0 + HEAD_DIM:v0 + 2 * HEAD_DIM] = jnp.ones((PAST_LEN + DEC_SEQ, HEAD_DIM), BF16)

    r0 = pl.multiple_of(qi * Q_TILE, Q_TILE)
    cos = cos_ref[pl.ds(r0, Q_TILE), :]
    sin = sin_ref[pl.ds(r0, Q_TILE), :]
    for kv in range(N_KV_HEADS):
        qs = []
        for i in range(Q_PER_KV):
            qcol = (kv * Q_PER_KV + i) * HEAD_DIM
            q = _head_norm(qkv_ref[pl.ds(r0, Q_TILE), qcol:qcol + HEAD_DIM], qn_ref[...])
            qs.append(_rope(q, cos, sin).astype(BF16))
        _attend(qs, k_scr[:, kv * HEAD_DIM:(kv + 1) * HEAD_DIM], v_scr[:, 2 * kv * HEAD_DIM:2 * (kv + 1) * HEAD_DIM],
                o_ref, slice(None), kv * Q_PER_KV * HEAD_DIM)


def _attn_lat(qkv, cache_k, cache_v, qn, kn, cos, sin):
    row0 = N_CTX // DEC_SEQ
    qt = DEC_SEQ // Q_TILE
    return pl.pallas_call(
        _attn_lat_kernel,
        grid=(DEC_BATCH, qt),
        in_specs=[
            pl.BlockSpec((DEC_SEQ, QKV_W), lambda b, q: (row0 + b, 0)),
            pl.BlockSpec((1, PAST_LEN, KV_W), lambda b, q: (b, 0, 0)),
            pl.BlockSpec((1, PAST_LEN, KV_W), lambda b, q: (b, 0, 0)),
            pl.BlockSpec((1, HEAD_DIM), lambda b, q: (0, 0)),
            pl.BlockSpec((1, HEAD_DIM), lambda b, q: (0, 0)),
            pl.BlockSpec((DEC_SEQ, HEAD_DIM), lambda b, q: (0, 0)),
            pl.BlockSpec((DEC_SEQ, HEAD_DIM), lambda b, q: (0, 0)),
        ],
        out_specs=pl.BlockSpec((Q_TILE, ATTN_W), lambda b, q: (b * qt + q, 0)),
        out_shape=jax.ShapeDtypeStruct((N_LAT, ATTN_W), BF16),
        scratch_shapes=[pltpu.VMEM((PAST_LEN + DEC_SEQ, KV_W), BF16),
                        pltpu.VMEM((PAST_LEN + DEC_SEQ, 2 * KV_W), BF16)],
        compiler_params=_params(("arbitrary", "arbitrary")),
        name="attention_latent",
    )(qkv, cache_k, cache_v, qn, kn, cos, sin)


def _lru_kernel(l_ref, cw_ref, cb_ref, gw_ref, gb_ref, lam_ref, h0f_ref, h0b_ref, o_ref, hf_ref, hb_ref,
                af_scr, bf_scr, ab_scr, bb_scr, yf_scr, yb_scr, *, seq_len):
    group = l_ref.shape[0] // seq_len
    xc = _dwconv(l_ref[:, 0:LRU_W], cw_ref[...], cb_ref[...], LRU_CONV_LEFT, _RowShifter(seq_len))
    xcb = xc.astype(BF16)
    half_xc = 0.5 * xc
    half_rate = []
    for d in range(2):
        z = -lam_ref[d:d + 1, :]
        half_rate.append((0.5 * LRU_C) * (jnp.maximum(z, 0.0) + jnp.log1p(jnp.exp(-jnp.abs(z)))))
    half_gb = 0.5 * gb_ref[...]
    for hd in range(LRU_HEADS):
        cols = slice(hd * LRU_BLK, (hd + 1) * LRU_BLK)
        pre = _dot(xcb[:, cols], gw_ref[hd].astype(BF16))
        for d, (a_scr, b_scr) in enumerate(((af_scr, bf_scr), (ab_scr, bb_scr))):
            base = d * 2 * LRU_BLK
            tr = jnp.tanh(pre[:, base:base + LRU_BLK] + half_gb[2 * d:2 * d + 1, cols])
            ti = jnp.tanh(pre[:, base + LRU_BLK:base + 2 * LRU_BLK] + half_gb[2 * d + 1:2 * d + 2, cols])
            hr = half_rate[d][:, cols]
            neg_log_a = tr * hr + hr
            a = jnp.exp(-neg_log_a)
            a_scr[:, cols] = a
            var = jnp.tanh(neg_log_a) * (1.0 + a * a)
            b_scr[:, cols] = jnp.where(var > 0.0, var * lax.rsqrt(var), 0.0) * ((ti + 1.0) * half_xc[:, cols])

    n_tiles = seq_len // LRU_SCAN_ROWS

    def tile_step(k, carry):
        out = list(carry)
        base_f = pl.multiple_of(k * LRU_SCAN_ROWS, LRU_SCAN_ROWS)
        base_b = pl.multiple_of((n_tiles - 1 - k) * LRU_SCAN_ROWS, LRU_SCAN_ROWS)
        for g in range(group):
            rows_f = pl.ds(g * seq_len + base_f, LRU_SCAN_ROWS)
            rows_b = pl.ds(g * seq_len + base_b, LRU_SCAN_ROWS)
            af, bf, yf = af_scr.at[rows_f, :], bf_scr.at[rows_f, :], yf_scr.at[rows_f, :]
            ab, bb, yb = ab_scr.at[rows_b, :], bb_scr.at[rows_b, :], yb_scr.at[rows_b, :]
            hf, hb = out[2 * g], out[2 * g + 1]
            for r in range(LRU_SCAN_ROWS):
                rb = LRU_SCAN_ROWS - 1 - r
                hf = af[r:r + 1, :] * hf + bf[r:r + 1, :]
                hb = ab[rb:rb + 1, :] * hb + bb[rb:rb + 1, :]
                yf[r:r + 1, :] = hf
                yb[rb:rb + 1, :] = hb
            out[2 * g], out[2 * g + 1] = hf, hb
        return tuple(out)

    init = []
    for g in range(group):
        init += [h0f_ref[g], h0b_ref[g]]
    final = lax.fori_loop(0, n_tiles, tile_step, tuple(init))
    for g in range(group):
        hf_ref[g] = final[2 * g]
        hb_ref[g] = final[2 * g + 1]
    lg = l_ref[:, LRU_W:]
    gelu = 0.5 * lg * (1.0 + jnp.tanh(math.sqrt(2.0 / math.pi) * (lg + 0.044715 * (lg * lg * lg))))
    o_ref[...] = (gelu * (yf_scr[...] + yb_scr[...])).astype(o_ref.dtype)


def _lru(l_all, cw, cb, gw, gb, lam, h0f, h0b, seq_len, n_seq, row0, group):
    rows = seq_len * group
    blk0 = row0 // rows
    vec = lambda b: (0, 0)
    return pl.pallas_call(
        functools.partial(_lru_kernel, seq_len=seq_len),
        grid=(n_seq // group,),
        in_specs=[
            pl.BlockSpec((rows, 2 * LRU_W), lambda b: (blk0 + b, 0)),
            pl.BlockSpec((LRU_CONV, LRU_W), vec),
            pl.BlockSpec((1, LRU_W), vec),
            pl.BlockSpec((LRU_HEADS, LRU_BLK, 4 * LRU_BLK), lambda b: (0, 0, 0)),
            pl.BlockSpec((4, LRU_W), vec),
            pl.BlockSpec((2, LRU_W), vec),
            pl.BlockSpec((group, 1, LRU_W), lambda b: (b, 0, 0)),
            pl.BlockSpec((group, 1, LRU_W), lambda b: (b, 0, 0)),
        ],
        out_specs=[
            pl.BlockSpec((rows, LRU_W), lambda b: (b, 0)),
            pl.BlockSpec((group, 1, LRU_W), lambda b: (b, 0, 0)),
            pl.BlockSpec((group, 1, LRU_W), lambda b: (b, 0, 0)),
        ],
        out_shape=[
            jax.ShapeDtypeStruct((n_seq * seq_len, LRU_W), BF16),
            jax.ShapeDtypeStruct((n_seq, 1, LRU_W), F32),
            jax.ShapeDtypeStruct((n_seq, 1, LRU_W), F32),
        ],
        scratch_shapes=[pltpu.VMEM((rows, LRU_W), F32)] * 6,
        compiler_params=_params(("arbitrary",)),
        name="rg_lru",
    )(l_all, cw, cb, gw, gb, lam, h0f, h0b)


def _hy_filter_kernel(z_ref, t_ref, ckh_ref, ckl_ref, alt_ref, w1_ref, b1_ref, w2_ref, b2_ref, w3_ref, fr_ref,
                      dec_ref, kf_ref, kn_ref):
    z = jnp.sin(fr_ref[0:1, :] * (_dot_f32(z_ref[...], w1_ref[...]) + b1_ref[...]))
    z = jnp.sin(fr_ref[1:2, :] * (_dot_f32(z, w2_ref[...]) + b2_ref[...]))
    zh, zl = _split_bf16(z)
    filt = _dot3(zh, zl, w3_ref[...]) * jnp.exp(-t_ref[...] * jnp.abs(dec_ref[...]))
    filt = filt / jnp.sum(jnp.abs(filt), axis=0, keepdims=True)
    kf = _dot3(ckh_ref[...], ckl_ref[...], filt)
    kn = jnp.sum(alt_ref[...] * filt, axis=0, keepdims=True)
    for o in range(HY_ORDER):
        kf_ref[o] = kf[:, o * HY_W:(o + 1) * HY_W]
        kn_ref[o:o + 1, :] = kn[:, o * HY_W:(o + 1) * HY_W]


def _hy_filter(consts, w1, b1, w2, b2, w3, freq, decay, seq_len):
    ckh, ckl = _split_bf16(consts["ck"])
    return pl.pallas_call(
        _hy_filter_kernel,
        out_shape=[jax.ShapeDtypeStruct((HY_ORDER, seq_len, HY_W), F32), jax.ShapeDtypeStruct((HY_ORDER, HY_W), F32)],
        compiler_params=pltpu.CompilerParams(vmem_limit_bytes=VMEM_LIMIT_BYTES),
        name="hyena_filter",
    )(consts["z"], consts["t"], ckh, ckl, consts["alt_w"], w1, b1, w2, b2, w3, freq, decay)


def _longconv(u, fwd_ref, inv_ref, kf2, kn, alt, skip):
    n = u.shape[0]
    spec = _dot(fwd_ref[...], u.astype(BF16)) * kf2
    y = _dot(inv_ref[...], spec.astype(BF16))
    nyq = jnp.sum(alt * u, axis=0, keepdims=True) * kn * (1.0 / (2 * n))
    return y + alt * nyq + u * skip


def _hyena_kernel(x1_ref, x2_ref, v_ref, cw_ref, cb_ref, kf_ref, kn_ref, skip_ref, fwd_ref, inv_ref, alt_ref, o_ref,
                  *, seq_len, chain_w):
    shift = _RowShifter(seq_len)
    alt = alt_ref[...]
    v = _dwconv(v_ref[...], cw_ref[2], cb_ref[2:3, :], HY_CONV_LEFT, shift)
    gates = [_dwconv(g_ref[...], cw_ref[o], cb_ref[o:o + 1, :], HY_CONV_LEFT, shift)
             for o, g_ref in enumerate((x1_ref, x2_ref))]
    for s in range(v.shape[0] // seq_len):
        rows = slice(s * seq_len, (s + 1) * seq_len)
        for c0 in range(0, v.shape[1], chain_w):
            cols = slice(c0, c0 + chain_w)
            z = v[rows, cols]
            for o in range(HY_ORDER):
                kfo = kf_ref[o, :, cols]
                z = gates[o][rows, cols] * _longconv(z, fwd_ref, inv_ref, jnp.concatenate([kfo, kfo], axis=0),
                                                     kn_ref[o:o + 1, cols], alt, skip_ref[o:o + 1, cols])
            o_ref[rows, cols] = z.astype(o_ref.dtype)


def _hyena(hy_all, cw, cb, kf, kn, skip, consts, seq_len, n_seq, row0, group, block_w, chain_w):
    rows = seq_len * group
    blk0 = row0 // rows
    nc = HY_W // block_w
    const = dict(pipeline_mode=pl.Buffered(1))
    in_specs = [pl.BlockSpec((rows, block_w), lambda b, c, g=g: (blk0 + b, g * nc + c)) for g in range(3)]
    in_specs += [
        pl.BlockSpec((HY_ORDER + 1, HY_CONV, block_w), lambda b, c: (0, 0, c)),
        pl.BlockSpec((HY_ORDER + 1, block_w), lambda b, c: (0, c)),
        pl.BlockSpec((HY_ORDER, seq_len, block_w), lambda b, c: (0, 0, c)),
        pl.BlockSpec((HY_ORDER, block_w), lambda b, c: (0, c)),
        pl.BlockSpec((HY_ORDER, block_w), lambda b, c: (0, c)),
        pl.BlockSpec((2 * seq_len, seq_len), lambda b, c: (0, 0), **const),
        pl.BlockSpec((seq_len, 2 * seq_len), lambda b, c: (0, 0), **const),
        pl.BlockSpec((seq_len, 1), lambda b, c: (0, 0)),
    ]
    return pl.pallas_call(
        functools.partial(_hyena_kernel, seq_len=seq_len, chain_w=chain_w),
        grid=(n_seq // group, nc),
        in_specs=in_specs,
        out_specs=pl.BlockSpec((rows, block_w), lambda b, c: (b, c)),
        out_shape=jax.ShapeDtypeStruct((n_seq * seq_len, HY_W), BF16),
        compiler_params=_params(("arbitrary", "arbitrary")),
        name="hyena",
    )(hy_all, hy_all, hy_all, cw.reshape(HY_CONV, HY_ORDER + 1, HY_W).transpose(1, 0, 2),
      cb.reshape(HY_ORDER + 1, HY_W), kf, kn, skip, consts["fwd"].astype(BF16), consts["inv"].astype(BF16),
      consts["alt"])


def _pool_kernel(p_ref, w_ref, s_ref, o_ref, *, seq_len):
    shift = _RowShifter(seq_len)
    t = lax.broadcasted_iota(jnp.int32, (p_ref.shape[0], 1), 0) & (seq_len - 1)
    for gi, win in enumerate(POOL_WINDOWS):
        cols = slice(gi * POOL_GW, (gi + 1) * POOL_GW)
        x = p_ref[:, cols]
        half = win // 2
        back, fwd = x, x
        m = 1
        while m < half:
            back = back + shift(back, -m)
            fwd = fwd + shift(fwd, m)
            m *= 2
        s = shift(back, -1) + fwd
        cnt = (jnp.minimum(t + half, seq_len) - jnp.maximum(t - half, 0)).astype(F32)
        y = _dot((s / cnt - x).astype(BF16), w_ref[gi].astype(BF16))
        o_ref[:, cols] = (y * s_ref[:, cols]).astype(o_ref.dtype)


def _pool(p_all, w, scale, seq_len, n_seq, row0, group):
    rows = seq_len * group
    blk0 = row0 // rows
    return pl.pallas_call(
        functools.partial(_pool_kernel, seq_len=seq_len),
        grid=(n_seq // group,),
        in_specs=[
            pl.BlockSpec((rows, POOL_W), lambda b: (blk0 + b, 0)),
            pl.BlockSpec((len(POOL_WINDOWS), POOL_GW, POOL_GW), lambda b: (0, 0, 0)),
            pl.BlockSpec((1, POOL_W), lambda b: (0, 0)),
        ],
        out_specs=pl.BlockSpec((rows, POOL_W), lambda b: (b, 0)),
        out_shape=jax.ShapeDtypeStruct((n_seq * seq_len, POOL_W), BF16),
        compiler_params=_params(("arbitrary",)),
        name="multi_pool",
    )(p_all, w, scale)


def _rope_tables():
    rows = DEC_SEQ // GRID_W
    r = np.repeat(np.arange(rows), GRID_W).astype(np.float64)
    col = np.tile(np.arange(GRID_W), rows).astype(np.float64)
    half = HEAD_DIM // 2
    inv = ROPE_THETA ** (-np.arange(0, half, 2, dtype=np.float64) / half)
    ang = np.concatenate([r[:, None] * inv, col[:, None] * inv], axis=-1)
    cos = np.repeat(np.cos(ang), 2, axis=-1)
    sin = np.repeat(np.sin(ang), 2, axis=-1) * np.tile(np.array([-1.0, 1.0]), half)
    return jnp.asarray(cos, F32), jnp.asarray(sin, F32)


def _hyena_tables(n):
    idx = np.arange(n, dtype=np.float64)
    t = idx / max(n - 1, 1)
    bands = np.linspace(1e-4, HY_BANDS - 1, HY_BANDS)
    f = 2.0 * math.pi * idx[:, None] * bands[None, :] / n
    z = np.zeros((n, HY_EMB_PAD))
    z[:, 0] = t
    z[:, 1:1 + HY_BANDS] = np.cos(f)
    z[:, 1 + HY_BANDS:HY_EMB] = -np.sin(f)
    ang = math.pi * ((idx[:, None] * idx[None, :]) % (2 * n)) / n
    wgt_n = np.where(idx == 0, 1.0, 2.0)
    alt = np.where(idx % 2 == 0, 1.0, -1.0)
    fwd = np.concatenate([np.cos(ang), np.sin(ang)], axis=0)
    inv = np.concatenate([np.cos(ang) * wgt_n[None, :], np.sin(ang) * wgt_n[None, :]], axis=1) / (2 * n)
    return {
        "z": jnp.asarray(z, F32),
        "t": jnp.asarray(t[:, None], F32),
        "ck": jnp.asarray(np.cos(ang) * wgt_n[None, :], F32),
        "alt_w": jnp.asarray((alt * wgt_n)[:, None], F32),
        "alt": jnp.asarray(alt[:, None], F32),
        "fwd": jnp.asarray(fwd, F32),
        "inv": jnp.asarray(inv, F32),
    }


def kernel(x_prompt, x_sample, cache_k, cache_v, state_lru_fwd, state_lru_bwd, c, c_ctx, norm_g, w_mod, b_mod, ffn_w13, ffn_w2, ab_w_in, ab_q_norm, ab_k_norm, lru_conv_w, lru_conv_b, lru_gate_w, lru_gate_b, lru_lambda, ab_w_out, cd_w_in, hy_conv_w, hy_conv_b, hy_w1, hy_b1, hy_w2, hy_b2, hy_w3, hy_freq, hy_decay, hy_skip, pool_w, pool_scale, cd_w_out):
    xs = [x_prompt.reshape(N_CTX, D_MODEL), x_sample.reshape(N_LAT, D_MODEL)]
    cond = jnp.concatenate([c_ctx[None], c, jnp.zeros((COND_PAD - N_COND, D_MODEL), F32)], axis=0)
    mods = _modulation(cond, w_mod, b_mod)[:, :N_COND].reshape(DEPTH, N_COND, N_MOD, D_MODEL)
    rope_cos, rope_sin = _rope_tables()

    k_list, v_list, hf_list, hb_list = [], [], [], []
    for l in range(DEPTH):
        mod = mods[l]
        g = norm_g[l]
        (x,) = _ffn(xs, mod, g[0:1], ffn_w13, ffn_w2, l, 0, 0, [N_TOK])
        if l % 2 == 0:
            e = l // 2
            qkv, lxg = _inproj(x, mod, g[1:2], ab_w_in[e], (QKV_W, 2 * LRU_W))
            qn, kn = ab_q_norm[e][None], ab_k_norm[e][None]
            attn_c, kc, vc = _attn_ctx(qkv, qn, kn)
            attn_l = _attn_lat(qkv, cache_k[:, e].reshape(DEC_BATCH, PAST_LEN, KV_W),
                               cache_v[:, e].reshape(DEC_BATCH, PAST_LEN, KV_W), qn, kn, rope_cos, rope_sin)
            gw = 0.5 * jnp.transpose(lru_gate_w[e], (2, 3, 0, 1, 4)).reshape(LRU_HEADS, LRU_BLK, 4 * LRU_BLK)
            gb = lru_gate_b[e].reshape(4, LRU_W)
            lru_args = (lru_conv_w[e], lru_conv_b[e][None], gw, gb, lru_lambda[e])
            zeros = jnp.zeros((BATCH, 1, LRU_W), F32)
            rec_c, hf, hb = _lru(lxg, *lru_args, zeros, zeros, SEQ, BATCH, 0, LRU_CTX_GROUP)
            rec_l, _, _ = _lru(lxg, *lru_args, state_lru_fwd[:, e][:, None], state_lru_bwd[:, e][:, None],
                               DEC_SEQ, DEC_BATCH, N_CTX, 1)
            x = _outproj(x, mod, attn_c, attn_l, rec_c, rec_l, ab_w_out[e])
            k_list.append(kc.reshape(BATCH, SEQ, N_KV_HEADS, HEAD_DIM))
            v_list.append(vc.reshape(BATCH, SEQ, N_KV_HEADS, HEAD_DIM))
            hf_list.append(hf.reshape(BATCH, LRU_W))
            hb_list.append(hb.reshape(BATCH, LRU_W))
        else:
            o = l // 2
            hy, pw = _inproj(x, mod, g[1:2], cd_w_in[o], ((HY_ORDER + 1) * HY_W, POOL_W))
            w1 = jnp.zeros((HY_EMB_PAD, HY_FH), F32).at[:HY_EMB].set(hy_w1[o])
            z_out, p_out = [], []
            for seq_len, n_seq, row0, hy_cfg, pool_group in (
                    (SEQ, BATCH, 0, (HY_CTX_GROUP, HY_W, HY_W), POOL_CTX_GROUP),
                    (DEC_SEQ, DEC_BATCH, N_CTX, (1, HY_LAT_BLOCK_W, HY_CHAIN_W), 1)):
                consts = _hyena_tables(seq_len)
                kf, kn = _hy_filter(consts, w1, hy_b1[o][None], hy_w2[o], hy_b2[o][None], hy_w3[o], hy_freq[o],
                                    hy_decay[o][None], seq_len)
                z_out.append(_hyena(hy, hy_conv_w[o], hy_conv_b[o], kf, kn, hy_skip[o], consts,
                                    seq_len, n_seq, row0, *hy_cfg))
                p_out.append(_pool(pw, pool_w[o], pool_scale[o][None], seq_len, n_seq, row0, pool_group))
            x = _outproj(x, mod, z_out[0], z_out[1], p_out[0], p_out[1], cd_w_out[o])
        xs = _ffn([x], mod, g[2:3], ffn_w13, ffn_w2, l, 1, 2, [N_TOK] if l + 1 < DEPTH else [N_CTX, N_LAT])

    y_prompt = xs[0].reshape(BATCH, SEQ, D_MODEL)
    y_sample = xs[1].reshape(DEC_BATCH, DEC_SEQ, D_MODEL)
    return (y_prompt, y_sample, jnp.stack(k_list, axis=1), jnp.stack(v_list, axis=1),
            jnp.stack(hf_list, axis=1), jnp.stack(hb_list, axis=1))
```

```python
import functools
import math

import numpy as np
import jax
import jax.numpy as jnp
from jax import lax
from jax.experimental import pallas as pl
from jax.experimental.pallas import tpu as pltpu

F32 = jnp.float32
BF16 = jnp.bfloat16

D_MODEL = 2048
BATCH = 32
SEQ = 256
DEPTH = 2
DEC_BATCH = 2
DEC_SEQ = 1024
PAST_LEN = 512
GRID_W = 64
N_MOD = 9
NORM_EPS = 1e-6
D_FF = 5632
HEAD_DIM = 128
N_Q_HEADS = 8
N_KV_HEADS = 2
Q_PER_KV = N_Q_HEADS // N_KV_HEADS
ATTN_W = N_Q_HEADS * HEAD_DIM
KV_W = N_KV_HEADS * HEAD_DIM
QKV_W = ATTN_W + 2 * KV_W
ROPE_THETA = 10000.0
LRU_W = 1024
LRU_HEADS = 8
LRU_BLK = LRU_W // LRU_HEADS
LRU_CONV = 4
LRU_CONV_LEFT = 2
LRU_C = 8.0
HY_W = 1024
HY_ORDER = 2
HY_CONV = 3
HY_CONV_LEFT = 1
HY_EMB = 33
HY_EMB_PAD = 128
HY_BANDS = (HY_EMB - 1) // 2
HY_FH = 64
POOL_W = 1024
POOL_WINDOWS = (2, 4, 8, 16)
POOL_GW = POOL_W // len(POOL_WINDOWS)

N_CTX = BATCH * SEQ
N_LAT = DEC_BATCH * DEC_SEQ
N_TOK = N_CTX + N_LAT
N_COND = 1 + DEC_BATCH
COND_PAD = 8

VMEM_LIMIT_BYTES = 56 * 1024 * 1024

ROW_TILE = 512
FFN_ROW_TILE = 1024
FF_TILE = 512
FFN_NORM_ROWS = 32
MOD_TILE = 1024
Q_TILE = 256
ATTN_CTX_GROUP = 4
LRU_CTX_GROUP = 2
LRU_SCAN_ROWS = 8
HY_CTX_GROUP = 2
POOL_CTX_GROUP = 4
HY_CHAIN_W = 512
HY_LAT_BLOCK_W = 512


def _params(sem):
    return pltpu.CompilerParams(dimension_semantics=sem, vmem_limit_bytes=VMEM_LIMIT_BYTES)


def _cond_index(i, tile):
    return jnp.maximum((i * tile) // DEC_SEQ - (N_CTX // DEC_SEQ - 1), 0)


def _sigmoid(x):
    return 1.0 / (1.0 + jnp.exp(-x))


def _dot(a, b):
    return jnp.dot(a, b, preferred_element_type=F32)


def _dot_f32(a, b):
    return jnp.dot(a, b, preferred_element_type=F32, precision=lax.Precision.HIGHEST)


def _split_bf16(x):
    hi = x.astype(BF16)
    return hi, (x - hi.astype(F32)).astype(BF16)


def _dot3(a_hi, a_lo, b):
    b_hi, b_lo = _split_bf16(b)
    return _dot(a_hi, b_hi) + (_dot(a_hi, b_lo) + _dot(a_lo, b_hi))


def _modulated_norm(x, g, mod_ref, j):
    ms = jnp.mean(x * x, axis=-1, keepdims=True)
    y = x * lax.rsqrt(ms + NORM_EPS) * g
    return y * (1.0 + mod_ref[0, 3 * j + 1:3 * j + 2, :]) + mod_ref[0, 3 * j:3 * j + 1, :]


def _head_norm(x, g):
    ms = jnp.mean(x * x, axis=-1, keepdims=True)
    return x * lax.rsqrt(ms + NORM_EPS) * g


class _RowShifter:
    def __init__(self, period):
        self.period = period
        self._masks = {}

    def __call__(self, x, off):
        if off == 0:
            return x
        key = (x.shape, off)
        if key not in self._masks:
            t = lax.broadcasted_iota(jnp.int32, x.shape, 0) & (self.period - 1)
            self._masks[key] = t >= -off if off < 0 else t < self.period - off
        return jnp.where(self._masks[key], pltpu.roll(x, (-off) % x.shape[0], axis=0), 0.0)


def _dwconv(x, w, b, left, shift):
    acc = b + shift(x, -left) * w[0:1, :]
    for j in range(1, w.shape[0]):
        acc = acc + shift(x, j - left) * w[j:j + 1, :]
    return acc


def _mod_kernel(c_ref, w_ref, b_ref, o_ref):
    c = c_ref[...]
    s = (c * _sigmoid(c)).astype(BF16)
    o_ref[0] = _dot(s, w_ref[0].astype(BF16)) + b_ref[0]


def _modulation(cond, w_mod, b_mod):
    n = N_MOD * D_MODEL
    return pl.pallas_call(
        _mod_kernel,
        grid=(DEPTH, n // MOD_TILE),
        in_specs=[
            pl.BlockSpec((COND_PAD, D_MODEL), lambda l, j: (0, 0)),
            pl.BlockSpec((1, D_MODEL, MOD_TILE), lambda l, j: (l, 0, j)),
            pl.BlockSpec((1, 1, MOD_TILE), lambda l, j: (l, 0, j)),
        ],
        out_specs=pl.BlockSpec((1, COND_PAD, MOD_TILE), lambda l, j: (l, 0, j)),
        out_shape=jax.ShapeDtypeStruct((DEPTH, COND_PAD, n), F32),
        compiler_params=_params(("arbitrary", "arbitrary")),
        name="modulation",
    )(cond, w_mod, b_mod.reshape(DEPTH, 1, n))


def _ffn_kernel(*refs, j, n_in, n_out, in_split, out_split):
    x_hbm = refs[:n_in]
    mod_ref, g_ref, w1_ref, w3_ref, w2_ref = refs[n_in:n_in + 5]
    o_hbm = refs[n_in + 5:n_in + 5 + n_out]
    buf, h_ref, sem_in, sem_out = refs[n_in + 5 + n_out:]
    i, f = pl.program_id(0), pl.program_id(1)
    n, nf = pl.num_programs(0), pl.num_programs(1)
    slot = lax.rem(i, 2)
    acc = buf.at[slot]

    def tile_rows(t):
        return pl.ds(pl.multiple_of(t * FFN_ROW_TILE, FFN_ROW_TILE), FFN_ROW_TILE)

    def in_copy(src, t, s):
        return pltpu.make_async_copy(src.at[tile_rows(t), :], buf.at[s], sem_in.at[s])

    def out_copy(dst, t, s):
        return pltpu.make_async_copy(buf.at[s], dst.at[tile_rows(t), :], sem_out.at[s])

    def start_split(copy, refs_, split, t, s):
        if len(refs_) == 1:
            copy(refs_[0], t, s).start()
        else:
            @pl.when(t < split)
            def _():
                copy(refs_[0], t, s).start()

            @pl.when(t >= split)
            def _():
                copy(refs_[1], t - split, s).start()

    @pl.when(f == 0)
    def _():
        @pl.when(i == 0)
        def _():
            start_split(in_copy, x_hbm, in_split, i, slot)

        in_copy(x_hbm[0], 0, slot).wait()
        gain = g_ref[...] * (1.0 + mod_ref[0, 3 * j + 1:3 * j + 2, :])
        shift = mod_ref[0, 3 * j:3 * j + 1, :]

        def chunk(r, carry):
            rows = pl.ds(pl.multiple_of(r * FFN_NORM_ROWS, FFN_NORM_ROWS), FFN_NORM_ROWS)
            x = acc[rows, :]
            ms = jnp.mean(x * x, axis=-1, keepdims=True)
            h_ref[rows, :] = (x * lax.rsqrt(ms + NORM_EPS) * gain + shift).astype(BF16)
            return carry

        lax.fori_loop(0, FFN_ROW_TILE // FFN_NORM_ROWS, chunk, 0, unroll=4)

    @pl.when((f == 1) & (i + 1 < n))
    def _():
        @pl.when(i >= 1)
        def _():
            out_copy(o_hbm[0], 0, 1 - slot).wait()

        start_split(in_copy, x_hbm, in_split, i + 1, 1 - slot)

    h = h_ref[...]
    gt = _dot(h, w1_ref[...].astype(BF16))
    up = _dot(h, w3_ref[...].astype(BF16))
    a = (gt * _sigmoid(gt) * up).astype(BF16)
    half_gate = 0.5 * mod_ref[0, 3 * j + 2:3 * j + 3, :]
    acc[...] += half_gate * _dot(a, w2_ref[...].astype(BF16))

    @pl.when(f == nf - 1)
    def _():
        start_split(out_copy, o_hbm, out_split, i, slot)

        @pl.when(i == n - 1)
        def _():
            out_copy(o_hbm[0], 0, slot).wait()
            out_copy(o_hbm[0], 0, 1 - slot).wait()


def _ffn(xs, mod, g, w13, w2, l, sub, j, out_rows):
    nf = D_FF // FF_TILE
    n_tiles = N_TOK // FFN_ROW_TILE
    assert nf >= 2 and n_tiles >= 2 and sum(x.shape[0] for x in xs) == N_TOK and sum(out_rows) == N_TOK
    assert all(x.shape[0] % FFN_ROW_TILE == 0 for x in xs) and all(r % FFN_ROW_TILE == 0 for r in out_rows)
    any_spec = pl.BlockSpec(memory_space=pl.ANY)
    outs = pl.pallas_call(
        functools.partial(_ffn_kernel, j=j, n_in=len(xs), n_out=len(out_rows),
                          in_split=xs[0].shape[0] // FFN_ROW_TILE, out_split=out_rows[0] // FFN_ROW_TILE),
        grid=(n_tiles, nf),
        in_specs=[any_spec] * len(xs) + [
            pl.BlockSpec((1, N_MOD, D_MODEL), lambda i, f: (_cond_index(i, FFN_ROW_TILE), 0, 0)),
            pl.BlockSpec((1, D_MODEL), lambda i, f: (0, 0)),
            pl.BlockSpec((None, None, D_MODEL, FF_TILE), lambda i, f: (l, sub, 0, f)),
            pl.BlockSpec((None, None, D_MODEL, FF_TILE), lambda i, f: (l, sub, 0, nf + f)),
            pl.BlockSpec((None, None, FF_TILE, D_MODEL), lambda i, f: (l, sub, f, 0)),
        ],
        out_specs=[any_spec] * len(out_rows),
        out_shape=[jax.ShapeDtypeStruct((r, D_MODEL), F32) for r in out_rows],
        scratch_shapes=[
            pltpu.VMEM((2, FFN_ROW_TILE, D_MODEL), F32),
            pltpu.VMEM((FFN_ROW_TILE, D_MODEL), BF16),
            pltpu.SemaphoreType.DMA((2,)),
            pltpu.SemaphoreType.DMA((2,)),
        ],
        compiler_params=_params(("arbitrary", "arbitrary")),
        name="ffn",
    )(*xs, mod, g, w13, w13, w2)
    return outs


W_STAGE_ROWS = 256


def _stage_weight_bf16(w_hbm, w_scr, stage, sem):
    n_chunks = w_hbm.shape[0] // W_STAGE_ROWS

    def copy(c, s):
        return pltpu.make_async_copy(w_hbm.at[pl.ds(c * W_STAGE_ROWS, W_STAGE_ROWS), :], stage.at[s], sem.at[s])

    copy(0, 0).start()
    for c in range(n_chunks):
        s = c % 2
        if c + 1 < n_chunks:
            copy(c + 1, 1 - s).start()
        copy(c, s).wait()
        w_scr[c * W_STAGE_ROWS:(c + 1) * W_STAGE_ROWS, :] = stage[s].astype(BF16)


def _weight_scratch(k, n):
    return [pltpu.VMEM((k, n), BF16), pltpu.VMEM((2, W_STAGE_ROWS, n), F32), pltpu.SemaphoreType.DMA((2,))]


IN_CHUNK = 512


def _inproj_kernel(x_ref, mod_ref, g_ref, w_hbm, *rest, widths):
    o_refs, (w_ref, stage, sem) = rest[:len(widths)], rest[len(widths):]

    @pl.when(pl.program_id(0) == 0)
    def _():
        _stage_weight_bf16(w_hbm, w_ref, stage, sem)

    h = _modulated_norm(x_ref[...], g_ref[...], mod_ref, 1).astype(BF16)
    off = 0
    for o_ref, wd in zip(o_refs, widths):
        for c0 in range(0, wd, IN_CHUNK):
            o_ref[:, c0:c0 + IN_CHUNK] = _dot(h, w_ref[:, off + c0:off + c0 + IN_CHUNK])
        off += wd


def _inproj(x, mod, g, w, widths):
    n = sum(widths)
    return pl.pallas_call(
        functools.partial(_inproj_kernel, widths=widths),
        grid=(N_TOK // ROW_TILE,),
        in_specs=[
            pl.BlockSpec((ROW_TILE, D_MODEL), lambda i: (i, 0)),
            pl.BlockSpec((1, N_MOD, D_MODEL), lambda i: (_cond_index(i, ROW_TILE), 0, 0)),
            pl.BlockSpec((1, D_MODEL), lambda i: (0, 0)),
            pl.BlockSpec(memory_space=pl.ANY),
        ],
        out_specs=[pl.BlockSpec((ROW_TILE, wd), lambda i: (i, 0)) for wd in widths],
        out_shape=[jax.ShapeDtypeStruct((N_TOK, wd), F32) for wd in widths],
        scratch_shapes=_weight_scratch(D_MODEL, n),
        compiler_params=_params(("arbitrary",)),
        name="mixer_in_proj",
    )(x, mod, g, w)


def _outproj_kernel(x_ref, mod_ref, ac_ref, al_ref, bc_ref, bl_ref, w_hbm, o_ref, w_ref, stage, sem):
    @pl.when(pl.program_id(0) == 0)
    def _():
        _stage_weight_bf16(w_hbm, w_ref, stage, sem)

    wa = w_ref.shape[0] // 2
    is_ctx = pl.program_id(0) < N_CTX // ROW_TILE
    a = jnp.where(is_ctx, ac_ref[...], al_ref[...])
    b = jnp.where(is_ctx, bc_ref[...], bl_ref[...])
    y = _dot(a, w_ref[0:wa, :]) + _dot(b, w_ref[wa:, :])
    o_ref[...] = x_ref[...] + mod_ref[0, 5:6, :] * y


def _outproj(x, mod, a_ctx, a_lat, b_ctx, b_lat, w):
    wa = a_ctx.shape[1]
    nc = N_CTX // ROW_TILE
    ctx_spec = pl.BlockSpec((ROW_TILE, wa), lambda i: (jnp.minimum(i, nc - 1), 0))
    lat_spec = pl.BlockSpec((ROW_TILE, wa), lambda i: (jnp.maximum(i - nc, 0), 0))
    return pl.pallas_call(
        _outproj_kernel,
        grid=(N_TOK // ROW_TILE,),
        in_specs=[
            pl.BlockSpec((ROW_TILE, D_MODEL), lambda i: (i, 0)),
            pl.BlockSpec((1, N_MOD, D_MODEL), lambda i: (_cond_index(i, ROW_TILE), 0, 0)),
            ctx_spec, lat_spec, ctx_spec, lat_spec,
            pl.BlockSpec(memory_space=pl.ANY),
        ],
        out_specs=pl.BlockSpec((ROW_TILE, D_MODEL), lambda i: (i, 0)),
        out_shape=jax.ShapeDtypeStruct((N_TOK, D_MODEL), F32),
        scratch_shapes=_weight_scratch(2 * wa, D_MODEL),
        compiler_params=_params(("arbitrary",)),
        name="mixer_out_proj",
    )(x, mod, a_ctx, a_lat, b_ctx, b_lat, w)


def _rope(x, cos, sin):
    lane = lax.broadcasted_iota(jnp.int32, x.shape, 1)
    swapped = jnp.where((lane & 1) == 0, pltpu.roll(x, HEAD_DIM - 1, axis=1), pltpu.roll(x, 1, axis=1))
    return x * cos + swapped * sin


def _attend(q_heads, k, v_ones, o_ref, rows, col0):
    t = q_heads[0].shape[0]
    q = jnp.concatenate(q_heads, axis=0)
    s = lax.dot_general(q, k, (((1,), (1,)), ((), ())), preferred_element_type=F32) * (HEAD_DIM ** -0.5)
    e = jnp.exp(s - jnp.max(s, axis=-1, keepdims=True)).astype(BF16)
    oe = _dot(e, v_ones)
    o = oe[:, 0:HEAD_DIM] / oe[:, HEAD_DIM:]
    for i in range(len(q_heads)):
        o_ref[rows, col0 + i * HEAD_DIM:col0 + (i + 1) * HEAD_DIM] = o[i * t:(i + 1) * t].astype(o_ref.dtype)


def _attn_ctx_kernel(qkv_ref, qn_ref, kn_ref, o_ref, kc_ref, vc_ref):
    ones = jnp.ones((SEQ, HEAD_DIM), BF16)
    for s in range(ATTN_CTX_GROUP):
        rows = slice(s * SEQ, (s + 1) * SEQ)
        for kv in range(N_KV_HEADS):
            kcol = ATTN_W + kv * HEAD_DIM
            vcol = ATTN_W + KV_W + kv * HEAD_DIM
            k = _head_norm(qkv_ref[rows, kcol:kcol + HEAD_DIM], kn_ref[...])
            v = qkv_ref[rows, vcol:vcol + HEAD_DIM]
            kc_ref[rows, kv * HEAD_DIM:(kv + 1) * HEAD_DIM] = k
            vc_ref[rows, kv * HEAD_DIM:(kv + 1) * HEAD_DIM] = v
            qs = []
            for i in range(Q_PER_KV):
                qcol = (kv * Q_PER_KV + i) * HEAD_DIM
                qs.append(_head_norm(qkv_ref[rows, qcol:qcol + HEAD_DIM], qn_ref[...]).astype(BF16))
            _attend(qs, k.astype(BF16), jnp.concatenate([v.astype(BF16), ones], axis=1), o_ref, rows,
                    kv * Q_PER_KV * HEAD_DIM)


def _attn_ctx(qkv, qn, kn):
    rows = ATTN_CTX_GROUP * SEQ
    return pl.pallas_call(
        _attn_ctx_kernel,
        grid=(BATCH // ATTN_CTX_GROUP,),
        in_specs=[
            pl.BlockSpec((rows, QKV_W), lambda b: (b, 0)),
            pl.BlockSpec((1, HEAD_DIM), lambda b: (0, 0)),
            pl.BlockSpec((1, HEAD_DIM), lambda b: (0, 0)),
        ],
        out_specs=[
            pl.BlockSpec((rows, ATTN_W), lambda b: (b, 0)),
            pl.BlockSpec((rows, KV_W), lambda b: (b, 0)),
            pl.BlockSpec((rows, KV_W), lambda b: (b, 0)),
        ],
        out_shape=[
            jax.ShapeDtypeStruct((N_CTX, ATTN_W), BF16),
            jax.ShapeDtypeStruct((N_CTX, KV_W), F32),
            jax.ShapeDtypeStruct((N_CTX, KV_W), F32),
        ],
        compiler_params=_params(("arbitrary",)),
        name="attention_context",
    )(qkv, qn, kn)


def _attn_lat_kernel(qkv_ref, ck_ref, cv_ref, qn_ref, kn_ref, cos_ref, sin_ref, o_ref, k_scr, v_scr):
    qi = pl.program_id(1)

    @pl.when(qi == 0)
    def _():
        k_scr[0:PAST_LEN, :] = ck_ref[0].astype(BF16)
        for kv in range(N_KV_HEADS):
            kcol = ATTN_W + kv * HEAD_DIM
            vcol = ATTN_W + KV_W + kv * HEAD_DIM
            k = _head_norm(qkv_ref[:, kcol:kcol + HEAD_DIM], kn_ref[...])
            k = _rope(k, cos_ref[...], sin_ref[...])
            k_scr[PAST_LEN:, kv * HEAD_DIM:(kv + 1) * HEAD_DIM] = k.astype(BF16)
            v0 = 2 * kv * HEAD_DIM
            v_scr[0:PAST_LEN, v0:v0 + HEAD_DIM] = cv_ref[0, :, kv * HEAD_DIM:(kv + 1) * HEAD_DIM].astype(BF16)
            v_scr[PAST_LEN:, v0:v0 + HEAD_DIM] = qkv_ref[:, vcol:vcol + HEAD_DIM].astype(BF16)
            v_scr[:, v0 + HEAD_DIM:v0 + 2 * HEAD_DIM] = jnp.ones((PAST_LEN + DEC_SEQ, HEAD_DIM), BF16)

    r0 = pl.multiple_of(qi * Q_TILE, Q_TILE)
    cos = cos_ref[pl.ds(r0, Q_TILE), :]
    sin = sin_ref[pl.ds(r0, Q_TILE), :]
    for kv in range(N_KV_HEADS):
        qs = []
        for i in range(Q_PER_KV):
            qcol = (kv * Q_PER_KV + i) * HEAD_DIM
            q = _head_norm(qkv_ref[pl.ds(r0, Q_TILE), qcol:qcol + HEAD_DIM], qn_ref[...])
            qs.append(_rope(q, cos, sin).astype(BF16))
        _attend(qs, k_scr[:, kv * HEAD_DIM:(kv + 1) * HEAD_DIM], v_scr[:, 2 * kv * HEAD_DIM:2 * (kv + 1) * HEAD_DIM],
                o_ref, slice(None), kv * Q_PER_KV * HEAD_DIM)


def _attn_lat(qkv, cache_k, cache_v, qn, kn, cos, sin):
    row0 = N_CTX // DEC_SEQ
    qt = DEC_SEQ // Q_TILE
    return pl.pallas_call(
        _attn_lat_kernel,
        grid=(DEC_BATCH, qt),
        in_specs=[
            pl.BlockSpec((DEC_SEQ, QKV_W), lambda b, q: (row0 + b, 0)),
            pl.BlockSpec((1, PAST_LEN, KV_W), lambda b, q: (b, 0, 0)),
            pl.BlockSpec((1, PAST_LEN, KV_W), lambda b, q: (b, 0, 0)),
            pl.BlockSpec((1, HEAD_DIM), lambda b, q: (0, 0)),
            pl.BlockSpec((1, HEAD_DIM), lambda b, q: (0, 0)),
            pl.BlockSpec((DEC_SEQ, HEAD_DIM), lambda b, q: (0, 0)),
            pl.BlockSpec((DEC_SEQ, HEAD_DIM), lambda b, q: (0, 0)),
        ],
        out_specs=pl.BlockSpec((Q_TILE, ATTN_W), lambda b, q: (b * qt + q, 0)),
        out_shape=jax.ShapeDtypeStruct((N_LAT, ATTN_W), BF16),
        scratch_shapes=[pltpu.VMEM((PAST_LEN + DEC_SEQ, KV_W), BF16),
                        pltpu.VMEM((PAST_LEN + DEC_SEQ, 2 * KV_W), BF16)],
        compiler_params=_params(("arbitrary", "arbitrary")),
        name="attention_latent",
    )(qkv, cache_k, cache_v, qn, kn, cos, sin)


def _lru_kernel(l_ref, cw_ref, cb_ref, gw_ref, gb_ref, lam_ref, h0f_ref, h0b_ref, o_ref, hf_ref, hb_ref,
                af_scr, bf_scr, ab_scr, bb_scr, yf_scr, yb_scr, *, seq_len):
    group = l_ref.shape[0] // seq_len
    xc = _dwconv(l_ref[:, 0:LRU_W], cw_ref[...], cb_ref[...], LRU_CONV_LEFT, _RowShifter(seq_len))
    xcb = xc.astype(BF16)
    half_xc = 0.5 * xc
    half_rate = []
    for d in range(2):
        z = -lam_ref[d:d + 1, :]
        half_rate.append((0.5 * LRU_C) * (jnp.maximum(z, 0.0) + jnp.log1p(jnp.exp(-jnp.abs(z)))))
    half_gb = 0.5 * gb_ref[...]
    for hd in range(LRU_HEADS):
        cols = slice(hd * LRU_BLK, (hd + 1) * LRU_BLK)
        pre = _dot(xcb[:, cols], gw_ref[hd].astype(BF16))
        for d, (a_scr, b_scr) in enumerate(((af_scr, bf_scr), (ab_scr, bb_scr))):
            base = d * 2 * LRU_BLK
            tr = jnp.tanh(pre[:, base:base + LRU_BLK] + half_gb[2 * d:2 * d + 1, cols])
            ti = jnp.tanh(pre[:, base + LRU_BLK:base + 2 * LRU_BLK] + half_gb[2 * d + 1:2 * d + 2, cols])
            hr = half_rate[d][:, cols]
            neg_log_a = tr * hr + hr
            a = jnp.exp(-neg_log_a)
            a_scr[:, cols] = a
            var = jnp.tanh(neg_log_a) * (1.0 + a * a)
            b_scr[:, cols] = jnp.where(var > 0.0, var * lax.rsqrt(var), 0.0) * ((ti + 1.0) * half_xc[:, cols])

    n_tiles = seq_len // LRU_SCAN_ROWS

    def tile_step(k, carry):
        out = list(carry)
        base_f = pl.multiple_of(k * LRU_SCAN_ROWS, LRU_SCAN_ROWS)
        base_b = pl.multiple_of((n_tiles - 1 - k) * LRU_SCAN_ROWS, LRU_SCAN_ROWS)
        for g in range(group):
            rows_f = pl.ds(g * seq_len + base_f, LRU_SCAN_ROWS)
            rows_b = pl.ds(g * seq_len + base_b, LRU_SCAN_ROWS)
            af, bf, yf = af_scr.at[rows_f, :], bf_scr.at[rows_f, :], yf_scr.at[rows_f, :]
            ab, bb, yb = ab_scr.at[rows_b, :], bb_scr.at[rows_b, :], yb_scr.at[rows_b, :]
            hf, hb = out[2 * g], out[2 * g + 1]
            for r in range(LRU_SCAN_ROWS):
                rb = LRU_SCAN_ROWS - 1 - r
                hf = af[r:r + 1, :] * hf + bf[r:r + 1, :]
                hb = ab[rb:rb + 1, :] * hb + bb[rb:rb + 1, :]
                yf[r:r + 1, :] = hf
                yb[rb:rb + 1, :] = hb
            out[2 * g], out[2 * g + 1] = hf, hb
        return tuple(out)

    init = []
    for g in range(group):
        init += [h0f_ref[g], h0b_ref[g]]
    final = lax.fori_loop(0, n_tiles, tile_step, tuple(init))
    for g in range(group):
        hf_ref[g] = final[2 * g]
        hb_ref[g] = final[2 * g + 1]
    lg = l_ref[:, LRU_W:]
    gelu = 0.5 * lg * (1.0 + jnp.tanh(math.sqrt(2.0 / math.pi) * (lg + 0.044715 * (lg * lg * lg))))
    o_ref[...] = (gelu * (yf_scr[...] + yb_scr[...])).astype(o_ref.dtype)


def _lru(l_all, cw, cb, gw, gb, lam, h0f, h0b, seq_len, n_seq, row0, group):
    rows = seq_len * group
    blk0 = row0 // rows
    vec = lambda b: (0, 0)
    return pl.pallas_call(
        functools.partial(_lru_kernel, seq_len=seq_len),
        grid=(n_seq // group,),
        in_specs=[
            pl.BlockSpec((rows, 2 * LRU_W), lambda b: (blk0 + b, 0)),
            pl.BlockSpec((LRU_CONV, LRU_W), vec),
            pl.BlockSpec((1, LRU_W), vec),
            pl.BlockSpec((LRU_HEADS, LRU_BLK, 4 * LRU_BLK), lambda b: (0, 0, 0)),
            pl.BlockSpec((4, LRU_W), vec),
            pl.BlockSpec((2, LRU_W), vec),
            pl.BlockSpec((group, 1, LRU_W), lambda b: (b, 0, 0)),
            pl.BlockSpec((group, 1, LRU_W), lambda b: (b, 0, 0)),
        ],
        out_specs=[
            pl.BlockSpec((rows, LRU_W), lambda b: (b, 0)),
            pl.BlockSpec((group, 1, LRU_W), lambda b: (b, 0, 0)),
            pl.BlockSpec((group, 1, LRU_W), lambda b: (b, 0, 0)),
        ],
        out_shape=[
            jax.ShapeDtypeStruct((n_seq * seq_len, LRU_W), BF16),
            jax.ShapeDtypeStruct((n_seq, 1, LRU_W), F32),
            jax.ShapeDtypeStruct((n_seq, 1, LRU_W), F32),
        ],
        scratch_shapes=[pltpu.VMEM((rows, LRU_W), F32)] * 6,
        compiler_params=_params(("arbitrary",)),
        name="rg_lru",
    )(l_all, cw, cb, gw, gb, lam, h0f, h0b)


def _hy_filter_kernel(z_ref, t_ref, ckh_ref, ckl_ref, alt_ref, w1_ref, b1_ref, w2_ref, b2_ref, w3_ref, fr_ref,
                      dec_ref, skip_ref, kf_ref):
    z = jnp.sin(fr_ref[0:1, :] * (_dot_f32(z_ref[...], w1_ref[...]) + b1_ref[...]))
    z = jnp.sin(fr_ref[1:2, :] * (_dot_f32(z, w2_ref[...]) + b2_ref[...]))
    zh, zl = _split_bf16(z)
    filt = _dot3(zh, zl, w3_ref[...]) * jnp.exp(-t_ref[...] * jnp.abs(dec_ref[...]))
    filt = filt / jnp.sum(jnp.abs(filt), axis=0, keepdims=True)
    n = t_ref.shape[0]
    kf = _dot3(ckh_ref[...], ckl_ref[...], filt)
    kn = jnp.sum(alt_ref[...] * filt, axis=0, keepdims=True)
    for o in range(HY_ORDER):
        cols = slice(o * HY_W, (o + 1) * HY_W)
        skip = skip_ref[o:o + 1, :]
        kf_ref[o, 0:n, :] = kf[:, cols] + skip
        kf_ref[o, n:, :] = kf[:, cols] + skip
        kf_ref[o, n:n + 1, :] = kn[:, cols] + skip


def _hy_filter(consts, w1, b1, w2, b2, w3, freq, decay, skip, seq_len):
    ckh, ckl = _split_bf16(consts["ck"])
    return pl.pallas_call(
        _hy_filter_kernel,
        out_shape=jax.ShapeDtypeStruct((HY_ORDER, 2 * seq_len, HY_W), F32),
        compiler_params=pltpu.CompilerParams(vmem_limit_bytes=VMEM_LIMIT_BYTES),
        name="hyena_filter",
    )(consts["z"], consts["t"], ckh, ckl, consts["alt_w"], w1, b1, w2, b2, w3, freq, decay, skip)


def _longconv(u, fwd_ref, inv_ref, kf2):
    spec = _dot(fwd_ref[...], u.astype(BF16)) * kf2
    return _dot(inv_ref[...], spec.astype(BF16))


def _hyena_kernel(x1_ref, x2_ref, v_ref, cw_ref, cb_ref, kf_ref, fwd_ref, inv_ref, o_ref, *, seq_len, chain_w):
    shift = _RowShifter(seq_len)
    v = _dwconv(v_ref[...], cw_ref[2], cb_ref[2:3, :], HY_CONV_LEFT, shift)
    gates = [_dwconv(g_ref[...], cw_ref[o], cb_ref[o:o + 1, :], HY_CONV_LEFT, shift)
             for o, g_ref in enumerate((x1_ref, x2_ref))]
    for s in range(v.shape[0] // seq_len):
        rows = slice(s * seq_len, (s + 1) * seq_len)
        for c0 in range(0, v.shape[1], chain_w):
            cols = slice(c0, c0 + chain_w)
            z = v[rows, cols]
            for o in range(HY_ORDER):
                z = gates[o][rows, cols] * _longconv(z, fwd_ref, inv_ref, kf_ref[o, :, cols])
            o_ref[rows, cols] = z.astype(o_ref.dtype)


def _hyena(hy_all, cw, cb, kf, consts, seq_len, n_seq, row0, group, block_w, chain_w):
    rows = seq_len * group
    blk0 = row0 // rows
    nc = HY_W // block_w
    const = dict(pipeline_mode=pl.Buffered(1))
    in_specs = [pl.BlockSpec((rows, block_w), lambda b, c, g=g: (blk0 + b, g * nc + c)) for g in range(3)]
    in_specs += [
        pl.BlockSpec((HY_ORDER + 1, HY_CONV, block_w), lambda b, c: (0, 0, c)),
        pl.BlockSpec((HY_ORDER + 1, block_w), lambda b, c: (0, c)),
        pl.BlockSpec((HY_ORDER, 2 * seq_len, block_w), lambda b, c: (0, 0, c)),
        pl.BlockSpec((2 * seq_len, seq_len), lambda b, c: (0, 0), **const),
        pl.BlockSpec((seq_len, 2 * seq_len), lambda b, c: (0, 0), **const),
    ]
    return pl.pallas_call(
        functools.partial(_hyena_kernel, seq_len=seq_len, chain_w=chain_w),
        grid=(n_seq // group, nc),
        in_specs=in_specs,
        out_specs=pl.BlockSpec((rows, block_w), lambda b, c: (b, c)),
        out_shape=jax.ShapeDtypeStruct((n_seq * seq_len, HY_W), BF16),
        compiler_params=_params(("arbitrary", "arbitrary")),
        name="hyena",
    )(hy_all, hy_all, hy_all, cw.reshape(HY_CONV, HY_ORDER + 1, HY_W).transpose(1, 0, 2),
      cb.reshape(HY_ORDER + 1, HY_W), kf, consts["fwd"].astype(BF16), consts["inv"].astype(BF16))


def _pool_kernel(p_ref, w_ref, s_ref, o_ref, *, seq_len):
    shift = _RowShifter(seq_len)
    t = lax.broadcasted_iota(jnp.int32, (p_ref.shape[0], 1), 0) & (seq_len - 1)
    for gi, win in enumerate(POOL_WINDOWS):
        cols = slice(gi * POOL_GW, (gi + 1) * POOL_GW)
        x = p_ref[:, cols]
        half = win // 2
        back, fwd = x, x
        m = 1
        while m < half:
            back = back + shift(back, -m)
            fwd = fwd + shift(fwd, m)
            m *= 2
        s = shift(back, -1) + fwd
        cnt = (jnp.minimum(t + half, seq_len) - jnp.maximum(t - half, 0)).astype(F32)
        y = _dot((s / cnt - x).astype(BF16), w_ref[gi].astype(BF16))
        o_ref[:, cols] = (y * s_ref[:, cols]).astype(o_ref.dtype)


def _pool(p_all, w, scale, seq_len, n_seq, row0, group):
    rows = seq_len * group
    blk0 = row0 // rows
    return pl.pallas_call(
        functools.partial(_pool_kernel, seq_len=seq_len),
        grid=(n_seq // group,),
        in_specs=[
            pl.BlockSpec((rows, POOL_W), lambda b: (blk0 + b, 0)),
            pl.BlockSpec((len(POOL_WINDOWS), POOL_GW, POOL_GW), lambda b: (0, 0, 0)),
            pl.BlockSpec((1, POOL_W), lambda b: (0, 0)),
        ],
        out_specs=pl.BlockSpec((rows, POOL_W), lambda b: (b, 0)),
        out_shape=jax.ShapeDtypeStruct((n_seq * seq_len, POOL_W), BF16),
        compiler_params=_params(("arbitrary",)),
        name="multi_pool",
    )(p_all, w, scale)


def _rope_tables():
    rows = DEC_SEQ // GRID_W
    r = np.repeat(np.arange(rows), GRID_W).astype(np.float64)
    col = np.tile(np.arange(GRID_W), rows).astype(np.float64)
    half = HEAD_DIM // 2
    inv = ROPE_THETA ** (-np.arange(0, half, 2, dtype=np.float64) / half)
    ang = np.concatenate([r[:, None] * inv, col[:, None] * inv], axis=-1)
    cos = np.repeat(np.cos(ang), 2, axis=-1)
    sin = np.repeat(np.sin(ang), 2, axis=-1) * np.tile(np.array([-1.0, 1.0]), half)
    return jnp.asarray(cos, F32), jnp.asarray(sin, F32)


def _hyena_tables(n):
    idx = np.arange(n, dtype=np.float64)
    t = idx / max(n - 1, 1)
    bands = np.linspace(1e-4, HY_BANDS - 1, HY_BANDS)
    f = 2.0 * math.pi * idx[:, None] * bands[None, :] / n
    z = np.zeros((n, HY_EMB_PAD))
    z[:, 0] = t
    z[:, 1:1 + HY_BANDS] = np.cos(f)
    z[:, 1 + HY_BANDS:HY_EMB] = -np.sin(f)
    ang = math.pi * ((idx[:, None] * idx[None, :]) % (2 * n)) / n
    wgt_n = np.where(idx == 0, 1.0, 2.0)
    alt = np.where(idx % 2 == 0, 1.0, -1.0)
    fwd = np.concatenate([np.cos(ang), np.sin(ang)], axis=0)
    inv = np.concatenate([np.cos(ang) * wgt_n[None, :], np.sin(ang) * wgt_n[None, :]], axis=1) / (2 * n)
    fwd[n, :] = alt
    inv[:, n] = alt / (2 * n)
    return {
        "z": jnp.asarray(z, F32),
        "t": jnp.asarray(t[:, None], F32),
        "ck": jnp.asarray(np.cos(ang) * wgt_n[None, :], F32),
        "alt_w": jnp.asarray((alt * wgt_n)[:, None], F32),
        "fwd": jnp.asarray(fwd, F32),
        "inv": jnp.asarray(inv, F32),
    }


def kernel(x_prompt, x_sample, cache_k, cache_v, state_lru_fwd, state_lru_bwd, c, c_ctx, norm_g, w_mod, b_mod, ffn_w13, ffn_w2, ab_w_in, ab_q_norm, ab_k_norm, lru_conv_w, lru_conv_b, lru_gate_w, lru_gate_b, lru_lambda, ab_w_out, cd_w_in, hy_conv_w, hy_conv_b, hy_w1, hy_b1, hy_w2, hy_b2, hy_w3, hy_freq, hy_decay, hy_skip, pool_w, pool_scale, cd_w_out):
    xs = [x_prompt.reshape(N_CTX, D_MODEL), x_sample.reshape(N_LAT, D_MODEL)]
    cond = jnp.concatenate([c_ctx[None], c, jnp.zeros((COND_PAD - N_COND, D_MODEL), F32)], axis=0)
    mods = _modulation(cond, w_mod, b_mod)[:, :N_COND].reshape(DEPTH, N_COND, N_MOD, D_MODEL)
    rope_cos, rope_sin = _rope_tables()

    k_list, v_list, hf_list, hb_list = [], [], [], []
    for l in range(DEPTH):
        mod = mods[l]
        g = norm_g[l]
        (x,) = _ffn(xs, mod, g[0:1], ffn_w13, ffn_w2, l, 0, 0, [N_TOK])
        if l % 2 == 0:
            e = l // 2
            qkv, lxg = _inproj(x, mod, g[1:2], ab_w_in[e], (QKV_W, 2 * LRU_W))
            qn, kn = ab_q_norm[e][None], ab_k_norm[e][None]
            attn_c, kc, vc = _attn_ctx(qkv, qn, kn)
            attn_l = _attn_lat(qkv, cache_k[:, e].reshape(DEC_BATCH, PAST_LEN, KV_W),
                               cache_v[:, e].reshape(DEC_BATCH, PAST_LEN, KV_W), qn, kn, rope_cos, rope_sin)
            gw = 0.5 * jnp.transpose(lru_gate_w[e], (2, 3, 0, 1, 4)).reshape(LRU_HEADS, LRU_BLK, 4 * LRU_BLK)
            gb = lru_gate_b[e].reshape(4, LRU_W)
            lru_args = (lru_conv_w[e], lru_conv_b[e][None], gw, gb, lru_lambda[e])
            zeros = jnp.zeros((BATCH, 1, LRU_W), F32)
            rec_c, hf, hb = _lru(lxg, *lru_args, zeros, zeros, SEQ, BATCH, 0, LRU_CTX_GROUP)
            rec_l, _, _ = _lru(lxg, *lru_args, state_lru_fwd[:, e][:, None], state_lru_bwd[:, e][:, None],
                               DEC_SEQ, DEC_BATCH, N_CTX, 1)
            x = _outproj(x, mod, attn_c, attn_l, rec_c, rec_l, ab_w_out[e])
            k_list.append(kc.reshape(BATCH, SEQ, N_KV_HEADS, HEAD_DIM))
            v_list.append(vc.reshape(BATCH, SEQ, N_KV_HEADS, HEAD_DIM))
            hf_list.append(hf.reshape(BATCH, LRU_W))
            hb_list.append(hb.reshape(BATCH, LRU_W))
        else:
            o = l // 2
            hy, pw = _inproj(x, mod, g[1:2], cd_w_in[o], ((HY_ORDER + 1) * HY_W, POOL_W))
            w1 = jnp.zeros((HY_EMB_PAD, HY_FH), F32).at[:HY_EMB].set(hy_w1[o])
            z_out, p_out = [], []
            for seq_len, n_seq, row0, hy_cfg, pool_group in (
                    (SEQ, BATCH, 0, (HY_CTX_GROUP, HY_W, HY_W), POOL_CTX_GROUP),
                    (DEC_SEQ, DEC_BATCH, N_CTX, (1, HY_LAT_BLOCK_W, HY_CHAIN_W), 1)):
                consts = _hyena_tables(seq_len)
                kf = _hy_filter(consts, w1, hy_b1[o][None], hy_w2[o], hy_b2[o][None], hy_w3[o], hy_freq[o],
                                hy_decay[o][None], hy_skip[o], seq_len)
                z_out.append(_hyena(hy, hy_conv_w[o], hy_conv_b[o], kf, consts, seq_len, n_seq, row0, *hy_cfg))
                p_out.append(_pool(pw, pool_w[o], pool_scale[o][None], seq_len, n_seq, row0, pool_group))
            x = _outproj(x, mod, z_out[0], z_out[1], p_out[0], p_out[1], cd_w_out[o])
        xs = _ffn([x], mod, g[2:3], ffn_w13, ffn_w2, l, 1, 2, [N_TOK] if l + 1 < DEPTH else [N_CTX, N_LAT])

    y_prompt = xs[0].reshape(BATCH, SEQ, D_MODEL)
    y_sample = xs[1].reshape(DEC_BATCH, DEC_SEQ, D_MODEL)
    return (y_prompt, y_sample, jnp.stack(k_list, axis=1), jnp.stack(v_list, axis=1),
            jnp.stack(hf_list, axis=1), jnp.stack(hb_list, axis=1))
```

```python
import functools
import math

import numpy as np
import jax
import jax.numpy as jnp
from jax import lax
from jax.experimental import pallas as pl
from jax.experimental.pallas import tpu as pltpu

F32 = jnp.float32
BF16 = jnp.bfloat16

D_MODEL = 2048
BATCH = 32
SEQ = 256
DEPTH = 2
DEC_BATCH = 2
DEC_SEQ = 1024
PAST_LEN = 512
GRID_W = 64
N_MOD = 9
NORM_EPS = 1e-6
D_FF = 5632
HEAD_DIM = 128
N_Q_HEADS = 8
N_KV_HEADS = 2
Q_PER_KV = N_Q_HEADS // N_KV_HEADS
ATTN_W = N_Q_HEADS * HEAD_DIM
KV_W = N_KV_HEADS * HEAD_DIM
QKV_W = ATTN_W + 2 * KV_W
ROPE_THETA = 10000.0
LRU_W = 1024
LRU_HEADS = 8
LRU_BLK = LRU_W // LRU_HEADS
LRU_CONV = 4
LRU_CONV_LEFT = 2
LRU_C = 8.0
HY_W = 1024
HY_ORDER = 2
HY_CONV = 3
HY_CONV_LEFT = 1
HY_EMB = 33
HY_EMB_PAD = 128
HY_BANDS = (HY_EMB - 1) // 2
HY_FH = 64
POOL_W = 1024
POOL_WINDOWS = (2, 4, 8, 16)
POOL_GW = POOL_W // len(POOL_WINDOWS)

N_CTX = BATCH * SEQ
N_LAT = DEC_BATCH * DEC_SEQ
N_TOK = N_CTX + N_LAT
N_COND = 1 + DEC_BATCH
COND_PAD = 8

VMEM_LIMIT_BYTES = 56 * 1024 * 1024

ROW_TILE = 512
FFN_ROW_TILE = 1024
FF_TILE = 512
FFN_NORM_ROWS = 32
MOD_TILE = 1024
Q_TILE = 256
ATTN_CTX_GROUP = 4
ATTN_CTX_HEADS_PER_CHAIN = 2
ATTN_LAT_HEADS_PER_CHAIN = 1
LRU_CTX_GROUP = 2
LRU_SCAN_ROWS = 8
HY_CTX_GROUP = 2
POOL_CTX_GROUP = 4
HY_CHAIN_W = 512
HY_LAT_BLOCK_W = 512


def _params(sem):
    return pltpu.CompilerParams(dimension_semantics=sem, vmem_limit_bytes=VMEM_LIMIT_BYTES)


def _cond_index(i, tile):
    return jnp.maximum((i * tile) // DEC_SEQ - (N_CTX // DEC_SEQ - 1), 0)


def _sigmoid(x):
    return 1.0 / (1.0 + jnp.exp(-x))


def _dot(a, b):
    return jnp.dot(a, b, preferred_element_type=F32)


def _dot_f32(a, b):
    return jnp.dot(a, b, preferred_element_type=F32, precision=lax.Precision.HIGHEST)


def _split_bf16(x):
    hi = x.astype(BF16)
    return hi, (x - hi.astype(F32)).astype(BF16)


def _dot3(a_hi, a_lo, b):
    b_hi, b_lo = _split_bf16(b)
    return _dot(a_hi, b_hi) + (_dot(a_hi, b_lo) + _dot(a_lo, b_hi))


def _modulated_norm(x, g, mod_ref, j):
    ms = jnp.mean(x * x, axis=-1, keepdims=True)
    y = x * lax.rsqrt(ms + NORM_EPS) * g
    return y * (1.0 + mod_ref[0, 3 * j + 1:3 * j + 2, :]) + mod_ref[0, 3 * j:3 * j + 1, :]


def _head_norm(x, g):
    ms = jnp.mean(x * x, axis=-1, keepdims=True)
    return x * lax.rsqrt(ms + NORM_EPS) * g


class _RowShifter:
    def __init__(self, period):
        self.period = period
        self._masks = {}

    def __call__(self, x, off):
        if off == 0:
            return x
        key = (x.shape, off)
        if key not in self._masks:
            t = lax.broadcasted_iota(jnp.int32, x.shape, 0) & (self.period - 1)
            self._masks[key] = t >= -off if off < 0 else t < self.period - off
        return jnp.where(self._masks[key], pltpu.roll(x, (-off) % x.shape[0], axis=0), 0.0)


def _dwconv(x, w, b, left, shift):
    acc = b + shift(x, -left) * w[0:1, :]
    for j in range(1, w.shape[0]):
        acc = acc + shift(x, j - left) * w[j:j + 1, :]
    return acc


def _mod_kernel(c_ref, w_ref, b_ref, o_ref):
    c = c_ref[...]
    s = (c * _sigmoid(c)).astype(BF16)
    o_ref[0] = _dot(s, w_ref[0].astype(BF16)) + b_ref[0]


def _modulation(cond, w_mod, b_mod):
    n = N_MOD * D_MODEL
    return pl.pallas_call(
        _mod_kernel,
        grid=(DEPTH, n // MOD_TILE),
        in_specs=[
            pl.BlockSpec((COND_PAD, D_MODEL), lambda l, j: (0, 0)),
            pl.BlockSpec((1, D_MODEL, MOD_TILE), lambda l, j: (l, 0, j)),
            pl.BlockSpec((1, 1, MOD_TILE), lambda l, j: (l, 0, j)),
        ],
        out_specs=pl.BlockSpec((1, COND_PAD, MOD_TILE), lambda l, j: (l, 0, j)),
        out_shape=jax.ShapeDtypeStruct((DEPTH, COND_PAD, n), F32),
        compiler_params=_params(("arbitrary", "arbitrary")),
        name="modulation",
    )(cond, w_mod, b_mod.reshape(DEPTH, 1, n))


def _ffn_kernel(*refs, j, n_in, n_out, in_split, out_split):
    x_hbm = refs[:n_in]
    mod_ref, g_ref, w1_ref, w3_ref, w2_ref = refs[n_in:n_in + 5]
    o_hbm = refs[n_in + 5:n_in + 5 + n_out]
    buf, h_ref, sem_in, sem_out = refs[n_in + 5 + n_out:]
    i, f = pl.program_id(0), pl.program_id(1)
    n, nf = pl.num_programs(0), pl.num_programs(1)
    slot = lax.rem(i, 2)
    acc = buf.at[slot]

    def tile_rows(t):
        return pl.ds(pl.multiple_of(t * FFN_ROW_TILE, FFN_ROW_TILE), FFN_ROW_TILE)

    def in_copy(src, t, s):
        return pltpu.make_async_copy(src.at[tile_rows(t), :], buf.at[s], sem_in.at[s])

    def out_copy(dst, t, s):
        return pltpu.make_async_copy(buf.at[s], dst.at[tile_rows(t), :], sem_out.at[s])

    def start_split(copy, refs_, split, t, s):
        if len(refs_) == 1:
            copy(refs_[0], t, s).start()
        else:
            @pl.when(t < split)
            def _():
                copy(refs_[0], t, s).start()

            @pl.when(t >= split)
            def _():
                copy(refs_[1], t - split, s).start()

    @pl.when(f == 0)
    def _():
        @pl.when(i == 0)
        def _():
            start_split(in_copy, x_hbm, in_split, i, slot)

        in_copy(x_hbm[0], 0, slot).wait()
        gain = g_ref[...] * (1.0 + mod_ref[0, 3 * j + 1:3 * j + 2, :])
        shift = mod_ref[0, 3 * j:3 * j + 1, :]

        def chunk(r, carry):
            rows = pl.ds(pl.multiple_of(r * FFN_NORM_ROWS, FFN_NORM_ROWS), FFN_NORM_ROWS)
            x = acc[rows, :]
            ms = jnp.mean(x * x, axis=-1, keepdims=True)
            h_ref[rows, :] = (x * lax.rsqrt(ms + NORM_EPS) * gain + shift).astype(BF16)
            return carry

        lax.fori_loop(0, FFN_ROW_TILE // FFN_NORM_ROWS, chunk, 0, unroll=4)

    @pl.when((f == 1) & (i + 1 < n))
    def _():
        @pl.when(i >= 1)
        def _():
            out_copy(o_hbm[0], 0, 1 - slot).wait()

        start_split(in_copy, x_hbm, in_split, i + 1, 1 - slot)

    h = h_ref[...]
    gt = _dot(h, w1_ref[...].astype(BF16))
    up = _dot(h, w3_ref[...].astype(BF16))
    a = (gt * _sigmoid(gt) * up).astype(BF16)
    half_gate = 0.5 * mod_ref[0, 3 * j + 2:3 * j + 3, :]
    acc[...] += half_gate * _dot(a, w2_ref[...].astype(BF16))

    @pl.when(f == nf - 1)
    def _():
        start_split(out_copy, o_hbm, out_split, i, slot)

        @pl.when(i == n - 1)
        def _():
            out_copy(o_hbm[0], 0, slot).wait()
            out_copy(o_hbm[0], 0, 1 - slot).wait()


def _ffn(xs, mod, g, w13, w2, l, sub, j, out_rows):
    nf = D_FF // FF_TILE
    n_tiles = N_TOK // FFN_ROW_TILE
    assert nf >= 2 and n_tiles >= 2 and sum(x.shape[0] for x in xs) == N_TOK and sum(out_rows) == N_TOK
    assert all(x.shape[0] % FFN_ROW_TILE == 0 for x in xs) and all(r % FFN_ROW_TILE == 0 for r in out_rows)
    any_spec = pl.BlockSpec(memory_space=pl.ANY)
    outs = pl.pallas_call(
        functools.partial(_ffn_kernel, j=j, n_in=len(xs), n_out=len(out_rows),
                          in_split=xs[0].shape[0] // FFN_ROW_TILE, out_split=out_rows[0] // FFN_ROW_TILE),
        grid=(n_tiles, nf),
        in_specs=[any_spec] * len(xs) + [
            pl.BlockSpec((1, N_MOD, D_MODEL), lambda i, f: (_cond_index(i, FFN_ROW_TILE), 0, 0)),
            pl.BlockSpec((1, D_MODEL), lambda i, f: (0, 0)),
            pl.BlockSpec((None, None, D_MODEL, FF_TILE), lambda i, f: (l, sub, 0, f)),
            pl.BlockSpec((None, None, D_MODEL, FF_TILE), lambda i, f: (l, sub, 0, nf + f)),
            pl.BlockSpec((None, None, FF_TILE, D_MODEL), lambda i, f: (l, sub, f, 0)),
        ],
        out_specs=[any_spec] * len(out_rows),
        out_shape=[jax.ShapeDtypeStruct((r, D_MODEL), F32) for r in out_rows],
        scratch_shapes=[
            pltpu.VMEM((2, FFN_ROW_TILE, D_MODEL), F32),
            pltpu.VMEM((FFN_ROW_TILE, D_MODEL), BF16),
            pltpu.SemaphoreType.DMA((2,)),
            pltpu.SemaphoreType.DMA((2,)),
        ],
        compiler_params=_params(("arbitrary", "arbitrary")),
        name="ffn",
    )(*xs, mod, g, w13, w13, w2)
    return outs


W_STAGE_ROWS = 256


def _stage_weight_bf16(w_hbm, w_scr, stage, sem):
    n_chunks = w_hbm.shape[0] // W_STAGE_ROWS

    def copy(c, s):
        return pltpu.make_async_copy(w_hbm.at[pl.ds(c * W_STAGE_ROWS, W_STAGE_ROWS), :], stage.at[s], sem.at[s])

    copy(0, 0).start()
    for c in range(n_chunks):
        s = c % 2
        if c + 1 < n_chunks:
            copy(c + 1, 1 - s).start()
        copy(c, s).wait()
        w_scr[c * W_STAGE_ROWS:(c + 1) * W_STAGE_ROWS, :] = stage[s].astype(BF16)


def _weight_scratch(k, n):
    return [pltpu.VMEM((k, n), BF16), pltpu.VMEM((2, W_STAGE_ROWS, n), F32), pltpu.SemaphoreType.DMA((2,))]


IN_CHUNK = 512


def _inproj_kernel(x_ref, mod_ref, g_ref, w_hbm, *rest, widths):
    o_refs, (w_ref, stage, sem) = rest[:len(widths)], rest[len(widths):]

    @pl.when(pl.program_id(0) == 0)
    def _():
        _stage_weight_bf16(w_hbm, w_ref, stage, sem)

    h = _modulated_norm(x_ref[...], g_ref[...], mod_ref, 1).astype(BF16)
    off = 0
    for o_ref, wd in zip(o_refs, widths):
        for c0 in range(0, wd, IN_CHUNK):
            o_ref[:, c0:c0 + IN_CHUNK] = _dot(h, w_ref[:, off + c0:off + c0 + IN_CHUNK])
        off += wd


def _inproj(x, mod, g, w, widths):
    n = sum(widths)
    return pl.pallas_call(
        functools.partial(_inproj_kernel, widths=widths),
        grid=(N_TOK // ROW_TILE,),
        in_specs=[
            pl.BlockSpec((ROW_TILE, D_MODEL), lambda i: (i, 0)),
            pl.BlockSpec((1, N_MOD, D_MODEL), lambda i: (_cond_index(i, ROW_TILE), 0, 0)),
            pl.BlockSpec((1, D_MODEL), lambda i: (0, 0)),
            pl.BlockSpec(memory_space=pl.ANY),
        ],
        out_specs=[pl.BlockSpec((ROW_TILE, wd), lambda i: (i, 0)) for wd in widths],
        out_shape=[jax.ShapeDtypeStruct((N_TOK, wd), F32) for wd in widths],
        scratch_shapes=_weight_scratch(D_MODEL, n),
        compiler_params=_params(("arbitrary",)),
        name="mixer_in_proj",
    )(x, mod, g, w)


def _outproj_kernel(x_ref, mod_ref, ac_ref, al_ref, bc_ref, bl_ref, w_hbm, o_ref, w_ref, stage, sem):
    @pl.when(pl.program_id(0) == 0)
    def _():
        _stage_weight_bf16(w_hbm, w_ref, stage, sem)

    wa = w_ref.shape[0] // 2
    is_ctx = pl.program_id(0) < N_CTX // ROW_TILE
    a = jnp.where(is_ctx, ac_ref[...], al_ref[...])
    b = jnp.where(is_ctx, bc_ref[...], bl_ref[...])
    y = _dot(a, w_ref[0:wa, :]) + _dot(b, w_ref[wa:, :])
    o_ref[...] = x_ref[...] + mod_ref[0, 5:6, :] * y


def _outproj(x, mod, a_ctx, a_lat, b_ctx, b_lat, w):
    wa = a_ctx.shape[1]
    nc = N_CTX // ROW_TILE
    ctx_spec = pl.BlockSpec((ROW_TILE, wa), lambda i: (jnp.minimum(i, nc - 1), 0))
    lat_spec = pl.BlockSpec((ROW_TILE, wa), lambda i: (jnp.maximum(i - nc, 0), 0))
    return pl.pallas_call(
        _outproj_kernel,
        grid=(N_TOK // ROW_TILE,),
        in_specs=[
            pl.BlockSpec((ROW_TILE, D_MODEL), lambda i: (i, 0)),
            pl.BlockSpec((1, N_MOD, D_MODEL), lambda i: (_cond_index(i, ROW_TILE), 0, 0)),
            ctx_spec, lat_spec, ctx_spec, lat_spec,
            pl.BlockSpec(memory_space=pl.ANY),
        ],
        out_specs=pl.BlockSpec((ROW_TILE, D_MODEL), lambda i: (i, 0)),
        out_shape=jax.ShapeDtypeStruct((N_TOK, D_MODEL), F32),
        scratch_shapes=_weight_scratch(2 * wa, D_MODEL),
        compiler_params=_params(("arbitrary",)),
        name="mixer_out_proj",
    )(x, mod, a_ctx, a_lat, b_ctx, b_lat, w)


def _rope(x, cos, sin):
    lane = lax.broadcasted_iota(jnp.int32, x.shape, 1)
    swapped = jnp.where((lane & 1) == 0, pltpu.roll(x, HEAD_DIM - 1, axis=1), pltpu.roll(x, 1, axis=1))
    return x * cos + swapped * sin


def _attend(q_heads, k, v_ones, o_ref, rows, col0, heads_per_chain):
    t = q_heads[0].shape[0]
    for i0 in range(0, len(q_heads), heads_per_chain):
        heads = q_heads[i0:i0 + heads_per_chain]
        q = jnp.concatenate(heads, axis=0)
        s = lax.dot_general(q, k, (((1,), (1,)), ((), ())), preferred_element_type=F32) * (HEAD_DIM ** -0.5)
        e = jnp.exp(s - jnp.max(s, axis=-1, keepdims=True)).astype(BF16)
        oe = _dot(e, v_ones)
        o = oe[:, 0:HEAD_DIM] / oe[:, HEAD_DIM:]
        for i in range(len(heads)):
            c0 = col0 + (i0 + i) * HEAD_DIM
            o_ref[rows, c0:c0 + HEAD_DIM] = o[i * t:(i + 1) * t].astype(o_ref.dtype)


def _attn_ctx_kernel(qkv_ref, qn_ref, kn_ref, o_ref, kc_ref, vc_ref):
    ones = jnp.ones((SEQ, HEAD_DIM), BF16)
    for s in range(ATTN_CTX_GROUP):
        rows = slice(s * SEQ, (s + 1) * SEQ)
        for kv in range(N_KV_HEADS):
            kcol = ATTN_W + kv * HEAD_DIM
            vcol = ATTN_W + KV_W + kv * HEAD_DIM
            k = _head_norm(qkv_ref[rows, kcol:kcol + HEAD_DIM], kn_ref[...])
            v = qkv_ref[rows, vcol:vcol + HEAD_DIM]
            kc_ref[rows, kv * HEAD_DIM:(kv + 1) * HEAD_DIM] = k
            vc_ref[rows, kv * HEAD_DIM:(kv + 1) * HEAD_DIM] = v
            qs = []
            for i in range(Q_PER_KV):
                qcol = (kv * Q_PER_KV + i) * HEAD_DIM
                qs.append(_head_norm(qkv_ref[rows, qcol:qcol + HEAD_DIM], qn_ref[...]).astype(BF16))
            _attend(qs, k.astype(BF16), jnp.concatenate([v.astype(BF16), ones], axis=1), o_ref, rows,
                    kv * Q_PER_KV * HEAD_DIM, ATTN_CTX_HEADS_PER_CHAIN)


def _attn_ctx(qkv, qn, kn):
    rows = ATTN_CTX_GROUP * SEQ
    return pl.pallas_call(
        _attn_ctx_kernel,
        grid=(BATCH // ATTN_CTX_GROUP,),
        in_specs=[
            pl.BlockSpec((rows, QKV_W), lambda b: (b, 0)),
            pl.BlockSpec((1, HEAD_DIM), lambda b: (0, 0)),
            pl.BlockSpec((1, HEAD_DIM), lambda b: (0, 0)),
        ],
        out_specs=[
            pl.BlockSpec((rows, ATTN_W), lambda b: (b, 0)),
            pl.BlockSpec((rows, KV_W), lambda b: (b, 0)),
            pl.BlockSpec((rows, KV_W), lambda b: (b, 0)),
        ],
        out_shape=[
            jax.ShapeDtypeStruct((N_CTX, ATTN_W), BF16),
            jax.ShapeDtypeStruct((N_CTX, KV_W), F32),
            jax.ShapeDtypeStruct((N_CTX, KV_W), F32),
        ],
        compiler_params=_params(("arbitrary",)),
        name="attention_context",
    )(qkv, qn, kn)


def _attn_lat_kernel(qkv_ref, ck_ref, cv_ref, qn_ref, kn_ref, cos_ref, sin_ref, o_ref, k_scr, v_scr):
    qi = pl.program_id(1)

    @pl.when(qi == 0)
    def _():
        k_scr[0:PAST_LEN, :] = ck_ref[0].astype(BF16)
        for kv in range(N_KV_HEADS):
            kcol = ATTN_W + kv * HEAD_DIM
            vcol = ATTN_W + KV_W + kv * HEAD_DIM
            k = _head_norm(qkv_ref[:, kcol:kcol + HEAD_DIM], kn_ref[...])
            k = _rope(k, cos_ref[...], sin_ref[...])
            k_scr[PAST_LEN:, kv * HEAD_DIM:(kv + 1) * HEAD_DIM] = k.astype(BF16)
            v0 = 2 * kv * HEAD_DIM
            v_scr[0:PAST_LEN, v0:v0 + HEAD_DIM] = cv_ref[0, :, kv * HEAD_DIM:(kv + 1) * HEAD_DIM].astype(BF16)
            v_scr[PAST_LEN:, v0:v0 + HEAD_DIM] = qkv_ref[:, vcol:vcol + HEAD_DIM].astype(BF16)
            v_scr[:, v0 + HEAD_DIM:v0 + 2 * HEAD_DIM] = jnp.ones((PAST_LEN + DEC_SEQ, HEAD_DIM), BF16)

    r0 = pl.multiple_of(qi * Q_TILE, Q_TILE)
    cos = cos_ref[pl.ds(r0, Q_TILE), :]
    sin = sin_ref[pl.ds(r0, Q_TILE), :]
    for kv in range(N_KV_HEADS):
        qs = []
        for i in range(Q_PER_KV):
            qcol = (kv * Q_PER_KV + i) * HEAD_DIM
            q = _head_norm(qkv_ref[pl.ds(r0, Q_TILE), qcol:qcol + HEAD_DIM], qn_ref[...])
            qs.append(_rope(q, cos, sin).astype(BF16))
        _attend(qs, k_scr[:, kv * HEAD_DIM:(kv + 1) * HEAD_DIM], v_scr[:, 2 * kv * HEAD_DIM:2 * (kv + 1) * HEAD_DIM],
                o_ref, slice(None), kv * Q_PER_KV * HEAD_DIM, ATTN_LAT_HEADS_PER_CHAIN)


def _attn_lat(qkv, cache_k, cache_v, qn, kn, cos, sin):
    row0 = N_CTX // DEC_SEQ
    qt = DEC_SEQ // Q_TILE
    return pl.pallas_call(
        _attn_lat_kernel,
        grid=(DEC_BATCH, qt),
        in_specs=[
            pl.BlockSpec((DEC_SEQ, QKV_W), lambda b, q: (row0 + b, 0)),
            pl.BlockSpec((1, PAST_LEN, KV_W), lambda b, q: (b, 0, 0)),
            pl.BlockSpec((1, PAST_LEN, KV_W), lambda b, q: (b, 0, 0)),
            pl.BlockSpec((1, HEAD_DIM), lambda b, q: (0, 0)),
            pl.BlockSpec((1, HEAD_DIM), lambda b, q: (0, 0)),
            pl.BlockSpec((DEC_SEQ, HEAD_DIM), lambda b, q: (0, 0)),
            pl.BlockSpec((DEC_SEQ, HEAD_DIM), lambda b, q: (0, 0)),
        ],
        out_specs=pl.BlockSpec((Q_TILE, ATTN_W), lambda b, q: (b * qt + q, 0)),
        out_shape=jax.ShapeDtypeStruct((N_LAT, ATTN_W), BF16),
        scratch_shapes=[pltpu.VMEM((PAST_LEN + DEC_SEQ, KV_W), BF16),
                        pltpu.VMEM((PAST_LEN + DEC_SEQ, 2 * KV_W), BF16)],
        compiler_params=_params(("arbitrary", "arbitrary")),
        name="attention_latent",
    )(qkv, cache_k, cache_v, qn, kn, cos, sin)


def _lru_kernel(l_ref, cw_ref, cb_ref, gw_ref, gb_ref, lam_ref, h0f_ref, h0b_ref, o_ref, hf_ref, hb_ref,
                af_scr, bf_scr, ab_scr, bb_scr, yf_scr, yb_scr, *, seq_len):
    group = l_ref.shape[0] // seq_len
    xc = _dwconv(l_ref[:, 0:LRU_W], cw_ref[...], cb_ref[...], LRU_CONV_LEFT, _RowShifter(seq_len))
    xcb = xc.astype(BF16)
    half_xc = 0.5 * xc
    half_rate = []
    for d in range(2):
        z = -lam_ref[d:d + 1, :]
        half_rate.append((0.5 * LRU_C) * (jnp.maximum(z, 0.0) + jnp.log1p(jnp.exp(-jnp.abs(z)))))
    half_gb = 0.5 * gb_ref[...]
    for hd in range(LRU_HEADS):
        cols = slice(hd * LRU_BLK, (hd + 1) * LRU_BLK)
        pre = _dot(xcb[:, cols], gw_ref[hd].astype(BF16))
        for d, (a_scr, b_scr) in enumerate(((af_scr, bf_scr), (ab_scr, bb_scr))):
            base = d * 2 * LRU_BLK
            tr = jnp.tanh(pre[:, base:base + LRU_BLK] + half_gb[2 * d:2 * d + 1, cols])
            ti = jnp.tanh(pre[:, base + LRU_BLK:base + 2 * LRU_BLK] + half_gb[2 * d + 1:2 * d + 2, cols])
            hr = half_rate[d][:, cols]
            neg_log_a = tr * hr + hr
            a = jnp.exp(-neg_log_a)
            a_scr[:, cols] = a
            var = jnp.tanh(neg_log_a) * (1.0 + a * a)
            b_scr[:, cols] = jnp.where(var > 0.0, var * lax.rsqrt(var), 0.0) * ((ti + 1.0) * half_xc[:, cols])

    n_tiles = seq_len // LRU_SCAN_ROWS

    def tile_step(k, carry):
        out = list(carry)
        base_f = pl.multiple_of(k * LRU_SCAN_ROWS, LRU_SCAN_ROWS)
        base_b = pl.multiple_of((n_tiles - 1 - k) * LRU_SCAN_ROWS, LRU_SCAN_ROWS)
        for g in range(group):
            rows_f = pl.ds(g * seq_len + base_f, LRU_SCAN_ROWS)
            rows_b = pl.ds(g * seq_len + base_b, LRU_SCAN_ROWS)
            af, bf, yf = af_scr.at[rows_f, :], bf_scr.at[rows_f, :], yf_scr.at[rows_f, :]
            ab, bb, yb = ab_scr.at[rows_b, :], bb_scr.at[rows_b, :], yb_scr.at[rows_b, :]
            hf, hb = out[2 * g], out[2 * g + 1]
            for r in range(LRU_SCAN_ROWS):
                rb = LRU_SCAN_ROWS - 1 - r
                hf = af[r:r + 1, :] * hf + bf[r:r + 1, :]
                hb = ab[rb:rb + 1, :] * hb + bb[rb:rb + 1, :]
                yf[r:r + 1, :] = hf
                yb[rb:rb + 1, :] = hb
            out[2 * g], out[2 * g + 1] = hf, hb
        return tuple(out)

    init = []
    for g in range(group):
        init += [h0f_ref[g], h0b_ref[g]]
    final = lax.fori_loop(0, n_tiles, tile_step, tuple(init))
    for g in range(group):
        hf_ref[g] = final[2 * g]
        hb_ref[g] = final[2 * g + 1]
    lg = l_ref[:, LRU_W:]
    gelu = 0.5 * lg * (1.0 + jnp.tanh(math.sqrt(2.0 / math.pi) * (lg + 0.044715 * (lg * lg * lg))))
    o_ref[...] = (gelu * (yf_scr[...] + yb_scr[...])).astype(o_ref.dtype)


def _lru(l_all, cw, cb, gw, gb, lam, h0f, h0b, seq_len, n_seq, row0, group):
    rows = seq_len * group
    blk0 = row0 // rows
    vec = lambda b: (0, 0)
    return pl.pallas_call(
        functools.partial(_lru_kernel, seq_len=seq_len),
        grid=(n_seq // group,),
        in_specs=[
            pl.BlockSpec((rows, 2 * LRU_W), lambda b: (blk0 + b, 0)),
            pl.BlockSpec((LRU_CONV, LRU_W), vec),
            pl.BlockSpec((1, LRU_W), vec),
            pl.BlockSpec((LRU_HEADS, LRU_BLK, 4 * LRU_BLK), lambda b: (0, 0, 0)),
            pl.BlockSpec((4, LRU_W), vec),
            pl.BlockSpec((2, LRU_W), vec),
            pl.BlockSpec((group, 1, LRU_W), lambda b: (b, 0, 0)),
            pl.BlockSpec((group, 1, LRU_W), lambda b: (b, 0, 0)),
        ],
        out_specs=[
            pl.BlockSpec((rows, LRU_W), lambda b: (b, 0)),
            pl.BlockSpec((group, 1, LRU_W), lambda b: (b, 0, 0)),
            pl.BlockSpec((group, 1, LRU_W), lambda b: (b, 0, 0)),
        ],
        out_shape=[
            jax.ShapeDtypeStruct((n_seq * seq_len, LRU_W), BF16),
            jax.ShapeDtypeStruct((n_seq, 1, LRU_W), F32),
            jax.ShapeDtypeStruct((n_seq, 1, LRU_W), F32),
        ],
        scratch_shapes=[pltpu.VMEM((rows, LRU_W), F32)] * 6,
        compiler_params=_params(("arbitrary",)),
        name="rg_lru",
    )(l_all, cw, cb, gw, gb, lam, h0f, h0b)


def _hy_filter_kernel(z_ref, t_ref, ckh_ref, ckl_ref, alt_ref, w1_ref, b1_ref, w2_ref, b2_ref, w3_ref, fr_ref,
                      dec_ref, skip_ref, kf_ref):
    z = jnp.sin(fr_ref[0:1, :] * (_dot_f32(z_ref[...], w1_ref[...]) + b1_ref[...]))
    z = jnp.sin(fr_ref[1:2, :] * (_dot_f32(z, w2_ref[...]) + b2_ref[...]))
    zh, zl = _split_bf16(z)
    filt = _dot3(zh, zl, w3_ref[...]) * jnp.exp(-t_ref[...] * jnp.abs(dec_ref[...]))
    filt = filt / jnp.sum(jnp.abs(filt), axis=0, keepdims=True)
    n = t_ref.shape[0]
    kf = _dot3(ckh_ref[...], ckl_ref[...], filt)
    kn = jnp.sum(alt_ref[...] * filt, axis=0, keepdims=True)
    for o in range(HY_ORDER):
        cols = slice(o * HY_W, (o + 1) * HY_W)
        skip = skip_ref[o:o + 1, :]
        kf_ref[o, 0:n, :] = kf[:, cols] + skip
        kf_ref[o, n:, :] = kf[:, cols] + skip
        kf_ref[o, n:n + 1, :] = kn[:, cols] + skip


def _hy_filter(consts, w1, b1, w2, b2, w3, freq, decay, skip, seq_len):
    ckh, ckl = _split_bf16(consts["ck"])
    return pl.pallas_call(
        _hy_filter_kernel,
        out_shape=jax.ShapeDtypeStruct((HY_ORDER, 2 * seq_len, HY_W), F32),
        compiler_params=pltpu.CompilerParams(vmem_limit_bytes=VMEM_LIMIT_BYTES),
        name="hyena_filter",
    )(consts["z"], consts["t"], ckh, ckl, consts["alt_w"], w1, b1, w2, b2, w3, freq, decay, skip)


def _longconv(u, fwd_ref, inv_ref, kf2):
    spec = _dot(fwd_ref[...], u.astype(BF16)) * kf2
    return _dot(inv_ref[...], spec.astype(BF16))


def _hyena_kernel(x1_ref, x2_ref, v_ref, cw_ref, cb_ref, kf_ref, fwd_ref, inv_ref, o_ref, *, seq_len, chain_w):
    shift = _RowShifter(seq_len)
    v = _dwconv(v_ref[...], cw_ref[2], cb_ref[2:3, :], HY_CONV_LEFT, shift)
    gates = [_dwconv(g_ref[...], cw_ref[o], cb_ref[o:o + 1, :], HY_CONV_LEFT, shift)
             for o, g_ref in enumerate((x1_ref, x2_ref))]
    for s in range(v.shape[0] // seq_len):
        rows = slice(s * seq_len, (s + 1) * seq_len)
        for c0 in range(0, v.shape[1], chain_w):
            cols = slice(c0, c0 + chain_w)
            z = v[rows, cols]
            for o in range(HY_ORDER):
                z = gates[o][rows, cols] * _longconv(z, fwd_ref, inv_ref, kf_ref[o, :, cols])
            o_ref[rows, cols] = z.astype(o_ref.dtype)


def _hyena(hy_all, cw, cb, kf, consts, seq_len, n_seq, row0, group, block_w, chain_w):
    rows = seq_len * group
    blk0 = row0 // rows
    nc = HY_W // block_w
    const = dict(pipeline_mode=pl.Buffered(1))
    in_specs = [pl.BlockSpec((rows, block_w), lambda b, c, g=g: (blk0 + b, g * nc + c)) for g in range(3)]
    in_specs += [
        pl.BlockSpec((HY_ORDER + 1, HY_CONV, block_w), lambda b, c: (0, 0, c)),
        pl.BlockSpec((HY_ORDER + 1, block_w), lambda b, c: (0, c)),
        pl.BlockSpec((HY_ORDER, 2 * seq_len, block_w), lambda b, c: (0, 0, c)),
        pl.BlockSpec((2 * seq_len, seq_len), lambda b, c: (0, 0), **const),
        pl.BlockSpec((seq_len, 2 * seq_len), lambda b, c: (0, 0), **const),
    ]
    return pl.pallas_call(
        functools.partial(_hyena_kernel, seq_len=seq_len, chain_w=chain_w),
        grid=(n_seq // group, nc),
        in_specs=in_specs,
        out_specs=pl.BlockSpec((rows, block_w), lambda b, c: (b, c)),
        out_shape=jax.ShapeDtypeStruct((n_seq * seq_len, HY_W), BF16),
        compiler_params=_params(("arbitrary", "arbitrary")),
        name="hyena",
    )(hy_all, hy_all, hy_all, cw.reshape(HY_CONV, HY_ORDER + 1, HY_W).transpose(1, 0, 2),
      cb.reshape(HY_ORDER + 1, HY_W), kf, consts["fwd"].astype(BF16), consts["inv"].astype(BF16))


def _pool_kernel(p_ref, w_ref, s_ref, o_ref, *, seq_len):
    shift = _RowShifter(seq_len)
    t = lax.broadcasted_iota(jnp.int32, (p_ref.shape[0], 1), 0) & (seq_len - 1)
    for gi, win in enumerate(POOL_WINDOWS):
        cols = slice(gi * POOL_GW, (gi + 1) * POOL_GW)
        x = p_ref[:, cols]
        half = win // 2
        back, fwd = x, x
        m = 1
        while m < half:
            back = back + shift(back, -m)
            fwd = fwd + shift(fwd, m)
            m *= 2
        s = shift(back, -1) + fwd
        cnt = (jnp.minimum(t + half, seq_len) - jnp.maximum(t - half, 0)).astype(F32)
        y = _dot((s / cnt - x).astype(BF16), w_ref[gi].astype(BF16))
        o_ref[:, cols] = (y * s_ref[:, cols]).astype(o_ref.dtype)


def _pool(p_all, w, scale, seq_len, n_seq, row0, group):
    rows = seq_len * group
    blk0 = row0 // rows
    return pl.pallas_call(
        functools.partial(_pool_kernel, seq_len=seq_len),
        grid=(n_seq // group,),
        in_specs=[
            pl.BlockSpec((rows, POOL_W), lambda b: (blk0 + b, 0)),
            pl.BlockSpec((len(POOL_WINDOWS), POOL_GW, POOL_GW), lambda b: (0, 0, 0)),
            pl.BlockSpec((1, POOL_W), lambda b: (0, 0)),
        ],
        out_specs=pl.BlockSpec((rows, POOL_W), lambda b: (b, 0)),
        out_shape=jax.ShapeDtypeStruct((n_seq * seq_len, POOL_W), BF16),
        compiler_params=_params(("arbitrary",)),
        name="multi_pool",
    )(p_all, w, scale)


def _rope_tables():
    rows = DEC_SEQ // GRID_W
    r = np.repeat(np.arange(rows), GRID_W).astype(np.float64)
    col = np.tile(np.arange(GRID_W), rows).astype(np.float64)
    half = HEAD_DIM // 2
    inv = ROPE_THETA ** (-np.arange(0, half, 2, dtype=np.float64) / half)
    ang = np.concatenate([r[:, None] * inv, col[:, None] * inv], axis=-1)
    cos = np.repeat(np.cos(ang), 2, axis=-1)
    sin = np.repeat(np.sin(ang), 2, axis=-1) * np.tile(np.array([-1.0, 1.0]), half)
    return jnp.asarray(cos, F32), jnp.asarray(sin, F32)


def _hyena_tables(n):
    idx = np.arange(n, dtype=np.float64)
    t = idx / max(n - 1, 1)
    bands = np.linspace(1e-4, HY_BANDS - 1, HY_BANDS)
    f = 2.0 * math.pi * idx[:, None] * bands[None, :] / n
    z = np.zeros((n, HY_EMB_PAD))
    z[:, 0] = t
    z[:, 1:1 + HY_BANDS] = np.cos(f)
    z[:, 1 + HY_BANDS:HY_EMB] = -np.sin(f)
    ang = math.pi * ((idx[:, None] * idx[None, :]) % (2 * n)) / n
    wgt_n = np.where(idx == 0, 1.0, 2.0)
    alt = np.where(idx % 2 == 0, 1.0, -1.0)
    fwd = np.concatenate([np.cos(ang), np.sin(ang)], axis=0)
    inv = np.concatenate([np.cos(ang) * wgt_n[None, :], np.sin(ang) * wgt_n[None, :]], axis=1) / (2 * n)
    fwd[n, :] = alt
    inv[:, n] = alt / (2 * n)
    return {
        "z": jnp.asarray(z, F32),
        "t": jnp.asarray(t[:, None], F32),
        "ck": jnp.asarray(np.cos(ang) * wgt_n[None, :], F32),
        "alt_w": jnp.asarray((alt * wgt_n)[:, None], F32),
        "fwd": jnp.asarray(fwd, F32),
        "inv": jnp.asarray(inv, F32),
    }


def kernel(x_prompt, x_sample, cache_k, cache_v, state_lru_fwd, state_lru_bwd, c, c_ctx, norm_g, w_mod, b_mod, ffn_w13, ffn_w2, ab_w_in, ab_q_norm, ab_k_norm, lru_conv_w, lru_conv_b, lru_gate_w, lru_gate_b, lru_lambda, ab_w_out, cd_w_in, hy_conv_w, hy_conv_b, hy_w1, hy_b1, hy_w2, hy_b2, hy_w3, hy_freq, hy_decay, hy_skip, pool_w, pool_scale, cd_w_out):
    xs = [x_prompt.reshape(N_CTX, D_MODEL), x_sample.reshape(N_LAT, D_MODEL)]
    cond = jnp.concatenate([c_ctx[None], c, jnp.zeros((COND_PAD - N_COND, D_MODEL), F32)], axis=0)
    mods = _modulation(cond, w_mod, b_mod)[:, :N_COND].reshape(DEPTH, N_COND, N_MOD, D_MODEL)
    rope_cos, rope_sin = _rope_tables()

    k_list, v_list, hf_list, hb_list = [], [], [], []
    for l in range(DEPTH):
        mod = mods[l]
        g = norm_g[l]
        (x,) = _ffn(xs, mod, g[0:1], ffn_w13, ffn_w2, l, 0, 0, [N_TOK])
        if l % 2 == 0:
            e = l // 2
            qkv, lxg = _inproj(x, mod, g[1:2], ab_w_in[e], (QKV_W, 2 * LRU_W))
            qn, kn = ab_q_norm[e][None], ab_k_norm[e][None]
            attn_c, kc, vc = _attn_ctx(qkv, qn, kn)
            attn_l = _attn_lat(qkv, cache_k[:, e].reshape(DEC_BATCH, PAST_LEN, KV_W),
                               cache_v[:, e].reshape(DEC_BATCH, PAST_LEN, KV_W), qn, kn, rope_cos, rope_sin)
            gw = 0.5 * jnp.transpose(lru_gate_w[e], (2, 3, 0, 1, 4)).reshape(LRU_HEADS, LRU_BLK, 4 * LRU_BLK)
            gb = lru_gate_b[e].reshape(4, LRU_W)
            lru_args = (lru_conv_w[e], lru_conv_b[e][None], gw, gb, lru_lambda[e])
            zeros = jnp.zeros((BATCH, 1, LRU_W), F32)
            rec_c, hf, hb = _lru(lxg, *lru_args, zeros, zeros, SEQ, BATCH, 0, LRU_CTX_GROUP)
            rec_l, _, _ = _lru(lxg, *lru_args, state_lru_fwd[:, e][:, None], state_lru_bwd[:, e][:, None],
                               DEC_SEQ, DEC_BATCH, N_CTX, 1)
            x = _outproj(x, mod, attn_c, attn_l, rec_c, rec_l, ab_w_out[e])
            k_list.append(kc.reshape(BATCH, SEQ, N_KV_HEADS, HEAD_DIM))
            v_list.append(vc.reshape(BATCH, SEQ, N_KV_HEADS, HEAD_DIM))
            hf_list.append(hf.reshape(BATCH, LRU_W))
            hb_list.append(hb.reshape(BATCH, LRU_W))
        else:
            o = l // 2
            hy, pw = _inproj(x, mod, g[1:2], cd_w_in[o], ((HY_ORDER + 1) * HY_W, POOL_W))
            w1 = jnp.zeros((HY_EMB_PAD, HY_FH), F32).at[:HY_EMB].set(hy_w1[o])
            z_out, p_out = [], []
            for seq_len, n_seq, row0, hy_cfg, pool_group in (
                    (SEQ, BATCH, 0, (HY_CTX_GROUP, HY_W, HY_W), POOL_CTX_GROUP),
                    (DEC_SEQ, DEC_BATCH, N_CTX, (1, HY_LAT_BLOCK_W, HY_CHAIN_W), 1)):
                consts = _hyena_tables(seq_len)
                kf = _hy_filter(consts, w1, hy_b1[o][None], hy_w2[o], hy_b2[o][None], hy_w3[o], hy_freq[o],
                                hy_decay[o][None], hy_skip[o], seq_len)
                z_out.append(_hyena(hy, hy_conv_w[o], hy_conv_b[o], kf, consts, seq_len, n_seq, row0, *hy_cfg))
                p_out.append(_pool(pw, pool_w[o], pool_scale[o][None], seq_len, n_seq, row0, pool_group))
            x = _outproj(x, mod, z_out[0], z_out[1], p_out[0], p_out[1], cd_w_out[o])
        xs = _ffn([x], mod, g[2:3], ffn_w13, ffn_w2, l, 1, 2, [N_TOK] if l + 1 < DEPTH else [N_CTX, N_LAT])

    y_prompt = xs[0].reshape(BATCH, SEQ, D_MODEL)
    y_sample = xs[1].reshape(DEC_BATCH, DEC_SEQ, D_MODEL)
    return (y_prompt, y_sample, jnp.stack(k_list, axis=1), jnp.stack(v_list, axis=1),
            jnp.stack(hf_list, axis=1), jnp.stack(hb_list, axis=1))
```

```python
import functools
import math

import numpy as np
import jax
import jax.numpy as jnp
from jax import lax
from jax.experimental import pallas as pl
from jax.experimental.pallas import tpu as pltpu

F32 = jnp.float32
BF16 = jnp.bfloat16

D_MODEL = 2048
BATCH = 32
SEQ = 256
DEPTH = 2
DEC_BATCH = 2
DEC_SEQ = 1024
PAST_LEN = 512
GRID_W = 64
N_MOD = 9
NORM_EPS = 1e-6
D_FF = 5632
HEAD_DIM = 128
N_Q_HEADS = 8
N_KV_HEADS = 2
Q_PER_KV = N_Q_HEADS // N_KV_HEADS
ATTN_W = N_Q_HEADS * HEAD_DIM
KV_W = N_KV_HEADS * HEAD_DIM
QKV_W = ATTN_W + 2 * KV_W
ROPE_THETA = 10000.0
LRU_W = 1024
LRU_HEADS = 8
LRU_BLK = LRU_W // LRU_HEADS
LRU_CONV = 4
LRU_CONV_LEFT = 2
LRU_C = 8.0
HY_W = 1024
HY_ORDER = 2
HY_CONV = 3
HY_CONV_LEFT = 1
HY_EMB = 33
HY_EMB_PAD = 128
HY_BANDS = (HY_EMB - 1) // 2
HY_FH = 64
POOL_W = 1024
POOL_WINDOWS = (2, 4, 8, 16)
POOL_GW = POOL_W // len(POOL_WINDOWS)

N_CTX = BATCH * SEQ
N_LAT = DEC_BATCH * DEC_SEQ
N_TOK = N_CTX + N_LAT
N_COND = 1 + DEC_BATCH
COND_PAD = 8

VMEM_LIMIT_BYTES = 56 * 1024 * 1024

ROW_TILE = 512
FFN_ROW_TILE = 1024
FF_TILE = 512
FFN_NORM_ROWS = 32
MOD_TILE = 1024
Q_TILE = 256
ATTN_CTX_GROUP = 4
ATTN_CTX_HEADS_PER_CHAIN = 2
ATTN_LAT_HEADS_PER_CHAIN = 1
LRU_CTX_GROUP = 2
LRU_SCAN_ROWS = 8
HY_CTX_GROUP = 2
POOL_CTX_GROUP = 4
HY_CHAIN_W = 512
HY_LAT_BLOCK_W = 512


def _params(sem):
    return pltpu.CompilerParams(dimension_semantics=sem, vmem_limit_bytes=VMEM_LIMIT_BYTES)


def _cond_index(i, tile):
    return jnp.maximum((i * tile) // DEC_SEQ - (N_CTX // DEC_SEQ - 1), 0)


def _sigmoid(x):
    return 1.0 / (1.0 + jnp.exp(-x))


def _dot(a, b):
    return jnp.dot(a, b, preferred_element_type=F32)


def _dot_f32(a, b):
    return jnp.dot(a, b, preferred_element_type=F32, precision=lax.Precision.HIGHEST)


def _split_bf16(x):
    hi = x.astype(BF16)
    return hi, (x - hi.astype(F32)).astype(BF16)


def _dot3(a_hi, a_lo, b):
    b_hi, b_lo = _split_bf16(b)
    return _dot(a_hi, b_hi) + (_dot(a_hi, b_lo) + _dot(a_lo, b_hi))


def _modulated_norm(x, g, mod_ref, j):
    ms = jnp.mean(x * x, axis=-1, keepdims=True)
    y = x * lax.rsqrt(ms + NORM_EPS) * g
    return y * (1.0 + mod_ref[0, 3 * j + 1:3 * j + 2, :]) + mod_ref[0, 3 * j:3 * j + 1, :]


def _head_norm(x, g):
    ms = jnp.mean(x * x, axis=-1, keepdims=True)
    return x * lax.rsqrt(ms + NORM_EPS) * g


class _RowShifter:
    def __init__(self, period):
        self.period = period
        self._masks = {}

    def __call__(self, x, off):
        if off == 0:
            return x
        key = (x.shape, off)
        if key not in self._masks:
            t = lax.broadcasted_iota(jnp.int32, x.shape, 0) & (self.period - 1)
            self._masks[key] = t >= -off if off < 0 else t < self.period - off
        return jnp.where(self._masks[key], pltpu.roll(x, (-off) % x.shape[0], axis=0), 0.0)


def _dwconv(x, w, b, left, shift):
    acc = b + shift(x, -left) * w[0:1, :]
    for j in range(1, w.shape[0]):
        acc = acc + shift(x, j - left) * w[j:j + 1, :]
    return acc


def _mod_kernel(c_ref, w_ref, b_ref, o_ref):
    c = c_ref[...]
    s = (c * _sigmoid(c)).astype(BF16)
    o_ref[0] = _dot(s, w_ref[0].astype(BF16)) + b_ref[0]


def _modulation(cond, w_mod, b_mod):
    n = N_MOD * D_MODEL
    return pl.pallas_call(
        _mod_kernel,
        grid=(DEPTH, n // MOD_TILE),
        in_specs=[
            pl.BlockSpec((COND_PAD, D_MODEL), lambda l, j: (0, 0)),
            pl.BlockSpec((1, D_MODEL, MOD_TILE), lambda l, j: (l, 0, j)),
            pl.BlockSpec((1, 1, MOD_TILE), lambda l, j: (l, 0, j)),
        ],
        out_specs=pl.BlockSpec((1, COND_PAD, MOD_TILE), lambda l, j: (l, 0, j)),
        out_shape=jax.ShapeDtypeStruct((DEPTH, COND_PAD, n), F32),
        compiler_params=_params(("arbitrary", "arbitrary")),
        name="modulation",
    )(cond, w_mod, b_mod.reshape(DEPTH, 1, n))


def _ffn_kernel(*refs, j, n_in, n_out, in_split, out_split):
    x_hbm = refs[:n_in]
    mod_ref, g_ref, w1_ref, w3_ref, w2_ref = refs[n_in:n_in + 5]
    o_hbm = refs[n_in + 5:n_in + 5 + n_out]
    buf, h_ref, sem_in, sem_out = refs[n_in + 5 + n_out:]
    i, f = pl.program_id(0), pl.program_id(1)
    n, nf = pl.num_programs(0), pl.num_programs(1)
    slot = lax.rem(i, 2)
    acc = buf.at[slot]

    def tile_rows(t):
        return pl.ds(pl.multiple_of(t * FFN_ROW_TILE, FFN_ROW_TILE), FFN_ROW_TILE)

    def in_copy(src, t, s):
        return pltpu.make_async_copy(src.at[tile_rows(t), :], buf.at[s], sem_in.at[s])

    def out_copy(dst, t, s):
        return pltpu.make_async_copy(buf.at[s], dst.at[tile_rows(t), :], sem_out.at[s])

    def start_split(copy, refs_, split, t, s):
        if len(refs_) == 1:
            copy(refs_[0], t, s).start()
        else:
            @pl.when(t < split)
            def _():
                copy(refs_[0], t, s).start()

            @pl.when(t >= split)
            def _():
                copy(refs_[1], t - split, s).start()

    @pl.when(f == 0)
    def _():
        @pl.when(i == 0)
        def _():
            start_split(in_copy, x_hbm, in_split, i, slot)

        in_copy(x_hbm[0], 0, slot).wait()
        gain = g_ref[...] * (1.0 + mod_ref[0, 3 * j + 1:3 * j + 2, :])
        shift = mod_ref[0, 3 * j:3 * j + 1, :]

        def chunk(r, carry):
            rows = pl.ds(pl.multiple_of(r * FFN_NORM_ROWS, FFN_NORM_ROWS), FFN_NORM_ROWS)
            x = acc[rows, :]
            ms = jnp.mean(x * x, axis=-1, keepdims=True)
            h_ref[rows, :] = (x * lax.rsqrt(ms + NORM_EPS) * gain + shift).astype(BF16)
            return carry

        lax.fori_loop(0, FFN_ROW_TILE // FFN_NORM_ROWS, chunk, 0, unroll=4)

    @pl.when((f == 1) & (i + 1 < n))
    def _():
        @pl.when(i >= 1)
        def _():
            out_copy(o_hbm[0], 0, 1 - slot).wait()

        start_split(in_copy, x_hbm, in_split, i + 1, 1 - slot)

    h = h_ref[...]
    gt = _dot(h, w1_ref[...].astype(BF16))
    up = _dot(h, w3_ref[...].astype(BF16))
    a = (gt * _sigmoid(gt) * up).astype(BF16)
    half_gate = 0.5 * mod_ref[0, 3 * j + 2:3 * j + 3, :]
    acc[...] += half_gate * _dot(a, w2_ref[...].astype(BF16))

    @pl.when(f == nf - 1)
    def _():
        start_split(out_copy, o_hbm, out_split, i, slot)

        @pl.when(i == n - 1)
        def _():
            out_copy(o_hbm[0], 0, slot).wait()
            out_copy(o_hbm[0], 0, 1 - slot).wait()


def _ffn(xs, mod, g, w13, w2, l, sub, j, out_rows):
    nf = D_FF // FF_TILE
    n_tiles = N_TOK // FFN_ROW_TILE
    assert nf >= 2 and n_tiles >= 2 and sum(x.shape[0] for x in xs) == N_TOK and sum(out_rows) == N_TOK
    assert all(x.shape[0] % FFN_ROW_TILE == 0 for x in xs) and all(r % FFN_ROW_TILE == 0 for r in out_rows)
    any_spec = pl.BlockSpec(memory_space=pl.ANY)
    outs = pl.pallas_call(
        functools.partial(_ffn_kernel, j=j, n_in=len(xs), n_out=len(out_rows),
                          in_split=xs[0].shape[0] // FFN_ROW_TILE, out_split=out_rows[0] // FFN_ROW_TILE),
        grid=(n_tiles, nf),
        in_specs=[any_spec] * len(xs) + [
            pl.BlockSpec((1, N_MOD, D_MODEL), lambda i, f: (_cond_index(i, FFN_ROW_TILE), 0, 0)),
            pl.BlockSpec((1, D_MODEL), lambda i, f: (0, 0)),
            pl.BlockSpec((None, None, D_MODEL, FF_TILE), lambda i, f: (l, sub, 0, f)),
            pl.BlockSpec((None, None, D_MODEL, FF_TILE), lambda i, f: (l, sub, 0, nf + f)),
            pl.BlockSpec((None, None, FF_TILE, D_MODEL), lambda i, f: (l, sub, f, 0)),
        ],
        out_specs=[any_spec] * len(out_rows),
        out_shape=[jax.ShapeDtypeStruct((r, D_MODEL), F32) for r in out_rows],
        scratch_shapes=[
            pltpu.VMEM((2, FFN_ROW_TILE, D_MODEL), F32),
            pltpu.VMEM((FFN_ROW_TILE, D_MODEL), BF16),
            pltpu.SemaphoreType.DMA((2,)),
            pltpu.SemaphoreType.DMA((2,)),
        ],
        compiler_params=_params(("arbitrary", "arbitrary")),
        name="ffn",
    )(*xs, mod, g, w13, w13, w2)
    return outs


W_CHUNK = 512


def _for_each_weight_chunk(w_hbm, w_scr, stage, sem, prepare, body):
    n_chunks = w_hbm.shape[1] // W_CHUNK
    chunks = [slice(c * W_CHUNK, (c + 1) * W_CHUNK) for c in range(n_chunks)]
    first = pl.program_id(0) == 0

    def copy(c, s):
        return pltpu.make_async_copy(w_hbm.at[:, pl.ds(c * W_CHUNK, W_CHUNK)], stage.at[s], sem.at[s])

    @pl.when(first)
    def _():
        copy(0, 0).start()
        lhs = prepare()
        for c, cols in enumerate(chunks):
            s = c % 2
            if c + 1 < n_chunks:
                copy(c + 1, 1 - s).start()
            copy(c, s).wait()
            w_scr[:, cols] = stage[s].astype(BF16)
            body(lhs, cols)

    @pl.when(jnp.logical_not(first))
    def _():
        lhs = prepare()
        for cols in chunks:
            body(lhs, cols)


def _weight_scratch(k, n):
    assert n % W_CHUNK == 0
    return [pltpu.VMEM((k, n), BF16), pltpu.VMEM((2, k, W_CHUNK), F32), pltpu.SemaphoreType.DMA((2,))]


def _inproj_kernel(x_ref, mod_ref, g_ref, w_hbm, *rest, widths):
    o_refs, (w_ref, stage, sem) = rest[:len(widths)], rest[len(widths):]
    starts = [sum(widths[:k]) for k in range(len(widths))]

    def normalise():
        return _modulated_norm(x_ref[...], g_ref[...], mod_ref, 1).astype(BF16)

    def project(h, cols):
        k = max(k for k, s0 in enumerate(starts) if s0 <= cols.start)
        o_refs[k][:, cols.start - starts[k]:cols.stop - starts[k]] = _dot(h, w_ref[:, cols])

    _for_each_weight_chunk(w_hbm, w_ref, stage, sem, normalise, project)


def _inproj(x, mod, g, w, widths):
    n = sum(widths)
    return pl.pallas_call(
        functools.partial(_inproj_kernel, widths=widths),
        grid=(N_TOK // ROW_TILE,),
        in_specs=[
            pl.BlockSpec((ROW_TILE, D_MODEL), lambda i: (i, 0)),
            pl.BlockSpec((1, N_MOD, D_MODEL), lambda i: (_cond_index(i, ROW_TILE), 0, 0)),
            pl.BlockSpec((1, D_MODEL), lambda i: (0, 0)),
            pl.BlockSpec(memory_space=pl.ANY),
        ],
        out_specs=[pl.BlockSpec((ROW_TILE, wd), lambda i: (i, 0)) for wd in widths],
        out_shape=[jax.ShapeDtypeStruct((N_TOK, wd), F32) for wd in widths],
        scratch_shapes=_weight_scratch(D_MODEL, n),
        compiler_params=_params(("arbitrary",)),
        name="mixer_in_proj",
    )(x, mod, g, w)


def _outproj_kernel(x_ref, mod_ref, ac_ref, al_ref, bc_ref, bl_ref, w_hbm, o_ref, w_ref, stage, sem):
    wa = w_ref.shape[0] // 2
    is_ctx = pl.program_id(0) < N_CTX // ROW_TILE

    def pick():
        return jnp.where(is_ctx, ac_ref[...], al_ref[...]), jnp.where(is_ctx, bc_ref[...], bl_ref[...])

    def project(ab, cols):
        y = _dot(ab[0], w_ref[0:wa, cols]) + _dot(ab[1], w_ref[wa:, cols])
        o_ref[:, cols] = x_ref[:, cols] + mod_ref[0, 5:6, cols] * y

    _for_each_weight_chunk(w_hbm, w_ref, stage, sem, pick, project)


def _outproj(x, mod, a_ctx, a_lat, b_ctx, b_lat, w):
    wa = a_ctx.shape[1]
    nc = N_CTX // ROW_TILE
    ctx_spec = pl.BlockSpec((ROW_TILE, wa), lambda i: (jnp.minimum(i, nc - 1), 0))
    lat_spec = pl.BlockSpec((ROW_TILE, wa), lambda i: (jnp.maximum(i - nc, 0), 0))
    return pl.pallas_call(
        _outproj_kernel,
        grid=(N_TOK // ROW_TILE,),
        in_specs=[
            pl.BlockSpec((ROW_TILE, D_MODEL), lambda i: (i, 0)),
            pl.BlockSpec((1, N_MOD, D_MODEL), lambda i: (_cond_index(i, ROW_TILE), 0, 0)),
            ctx_spec, lat_spec, ctx_spec, lat_spec,
            pl.BlockSpec(memory_space=pl.ANY),
        ],
        out_specs=pl.BlockSpec((ROW_TILE, D_MODEL), lambda i: (i, 0)),
        out_shape=jax.ShapeDtypeStruct((N_TOK, D_MODEL), F32),
        scratch_shapes=_weight_scratch(2 * wa, D_MODEL),
        compiler_params=_params(("arbitrary",)),
        name="mixer_out_proj",
    )(x, mod, a_ctx, a_lat, b_ctx, b_lat, w)


def _rope(x, cos, sin):
    lane = lax.broadcasted_iota(jnp.int32, x.shape, 1)
    swapped = jnp.where((lane & 1) == 0, pltpu.roll(x, HEAD_DIM - 1, axis=1), pltpu.roll(x, 1, axis=1))
    return x * cos + swapped * sin


def _attend(q_heads, k, v_ones, o_ref, rows, col0, heads_per_chain):
    t = q_heads[0].shape[0]
    for i0 in range(0, len(q_heads), heads_per_chain):
        heads = q_heads[i0:i0 + heads_per_chain]
        q = jnp.concatenate(heads, axis=0)
        s = lax.dot_general(q, k, (((1,), (1,)), ((), ())), preferred_element_type=F32) * (HEAD_DIM ** -0.5)
        e = jnp.exp(s - jnp.max(s, axis=-1, keepdims=True)).astype(BF16)
        oe = _dot(e, v_ones)
        o = oe[:, 0:HEAD_DIM] / oe[:, HEAD_DIM:]
        for i in range(len(heads)):
            c0 = col0 + (i0 + i) * HEAD_DIM
            o_ref[rows, c0:c0 + HEAD_DIM] = o[i * t:(i + 1) * t].astype(o_ref.dtype)


def _attn_ctx_kernel(qkv_ref, qn_ref, kn_ref, o_ref, kc_ref, vc_ref):
    ones = jnp.ones((SEQ, HEAD_DIM), BF16)
    for s in range(ATTN_CTX_GROUP):
        rows = slice(s * SEQ, (s + 1) * SEQ)
        for kv in range(N_KV_HEADS):
            kcol = ATTN_W + kv * HEAD_DIM
            vcol = ATTN_W + KV_W + kv * HEAD_DIM
            k = _head_norm(qkv_ref[rows, kcol:kcol + HEAD_DIM], kn_ref[...])
            v = qkv_ref[rows, vcol:vcol + HEAD_DIM]
            kc_ref[rows, kv * HEAD_DIM:(kv + 1) * HEAD_DIM] = k
            vc_ref[rows, kv * HEAD_DIM:(kv + 1) * HEAD_DIM] = v
            qs = []
            for i in range(Q_PER_KV):
                qcol = (kv * Q_PER_KV + i) * HEAD_DIM
                qs.append(_head_norm(qkv_ref[rows, qcol:qcol + HEAD_DIM], qn_ref[...]).astype(BF16))
            _attend(qs, k.astype(BF16), jnp.concatenate([v.astype(BF16), ones], axis=1), o_ref, rows,
                    kv * Q_PER_KV * HEAD_DIM, ATTN_CTX_HEADS_PER_CHAIN)


def _attn_ctx(qkv, qn, kn):
    rows = ATTN_CTX_GROUP * SEQ
    return pl.pallas_call(
        _attn_ctx_kernel,
        grid=(BATCH // ATTN_CTX_GROUP,),
        in_specs=[
            pl.BlockSpec((rows, QKV_W), lambda b: (b, 0)),
            pl.BlockSpec((1, HEAD_DIM), lambda b: (0, 0)),
            pl.BlockSpec((1, HEAD_DIM), lambda b: (0, 0)),
        ],
        out_specs=[
            pl.BlockSpec((rows, ATTN_W), lambda b: (b, 0)),
            pl.BlockSpec((rows, KV_W), lambda b: (b, 0)),
            pl.BlockSpec((rows, KV_W), lambda b: (b, 0)),
        ],
        out_shape=[
            jax.ShapeDtypeStruct((N_CTX, ATTN_W), BF16),
            jax.ShapeDtypeStruct((N_CTX, KV_W), F32),
            jax.ShapeDtypeStruct((N_CTX, KV_W), F32),
        ],
        compiler_params=_params(("arbitrary",)),
        name="attention_context",
    )(qkv, qn, kn)


def _attn_lat_kernel(qkv_ref, ck_ref, cv_ref, qn_ref, kn_ref, cos_ref, sin_ref, o_ref, k_scr, v_scr):
    qi = pl.program_id(1)

    @pl.when(qi == 0)
    def _():
        k_scr[0:PAST_LEN, :] = ck_ref[0].astype(BF16)
        for kv in range(N_KV_HEADS):
            kcol = ATTN_W + kv * HEAD_DIM
            vcol = ATTN_W + KV_W + kv * HEAD_DIM
            k = _head_norm(qkv_ref[:, kcol:kcol + HEAD_DIM], kn_ref[...])
            k = _rope(k, cos_ref[...], sin_ref[...])
            k_scr[PAST_LEN:, kv * HEAD_DIM:(kv + 1) * HEAD_DIM] = k.astype(BF16)
            v0 = 2 * kv * HEAD_DIM
            v_scr[0:PAST_LEN, v0:v0 + HEAD_DIM] = cv_ref[0, :, kv * HEAD_DIM:(kv + 1) * HEAD_DIM].astype(BF16)
            v_scr[PAST_LEN:, v0:v0 + HEAD_DIM] = qkv_ref[:, vcol:vcol + HEAD_DIM].astype(BF16)
            v_scr[:, v0 + HEAD_DIM:v0 + 2 * HEAD_DIM] = jnp.ones((PAST_LEN + DEC_SEQ, HEAD_DIM), BF16)

    r0 = pl.multiple_of(qi * Q_TILE, Q_TILE)
    cos = cos_ref[pl.ds(r0, Q_TILE), :]
    sin = sin_ref[pl.ds(r0, Q_TILE), :]
    for kv in range(N_KV_HEADS):
        qs = []
        for i in range(Q_PER_KV):
            qcol = (kv * Q_PER_KV + i) * HEAD_DIM
            q = _head_norm(qkv_ref[pl.ds(r0, Q_TILE), qcol:qcol + HEAD_DIM], qn_ref[...])
            qs.append(_rope(q, cos, sin).astype(BF16))
        _attend(qs, k_scr[:, kv * HEAD_DIM:(kv + 1) * HEAD_DIM], v_scr[:, 2 * kv * HEAD_DIM:2 * (kv + 1) * HEAD_DIM],
                o_ref, slice(None), kv * Q_PER_KV * HEAD_DIM, ATTN_LAT_HEADS_PER_CHAIN)


def _attn_lat(qkv, cache_k, cache_v, qn, kn, cos, sin):
    row0 = N_CTX // DEC_SEQ
    qt = DEC_SEQ // Q_TILE
    return pl.pallas_call(
        _attn_lat_kernel,
        grid=(DEC_BATCH, qt),
        in_specs=[
            pl.BlockSpec((DEC_SEQ, QKV_W), lambda b, q: (row0 + b, 0)),
            pl.BlockSpec((1, PAST_LEN, KV_W), lambda b, q: (b, 0, 0)),
            pl.BlockSpec((1, PAST_LEN, KV_W), lambda b, q: (b, 0, 0)),
            pl.BlockSpec((1, HEAD_DIM), lambda b, q: (0, 0)),
            pl.BlockSpec((1, HEAD_DIM), lambda b, q: (0, 0)),
            pl.BlockSpec((DEC_SEQ, HEAD_DIM), lambda b, q: (0, 0)),
            pl.BlockSpec((DEC_SEQ, HEAD_DIM), lambda b, q: (0, 0)),
        ],
        out_specs=pl.BlockSpec((Q_TILE, ATTN_W), lambda b, q: (b * qt + q, 0)),
        out_shape=jax.ShapeDtypeStruct((N_LAT, ATTN_W), BF16),
        scratch_shapes=[pltpu.VMEM((PAST_LEN + DEC_SEQ, KV_W), BF16),
                        pltpu.VMEM((PAST_LEN + DEC_SEQ, 2 * KV_W), BF16)],
        compiler_params=_params(("arbitrary", "arbitrary")),
        name="attention_latent",
    )(qkv, cache_k, cache_v, qn, kn, cos, sin)


def _lru_kernel(l_ref, cw_ref, cb_ref, gw_ref, gb_ref, lam_ref, h0f_ref, h0b_ref, o_ref, hf_ref, hb_ref,
                af_scr, bf_scr, ab_scr, bb_scr, yf_scr, yb_scr, *, seq_len):
    group = l_ref.shape[0] // seq_len
    xc = _dwconv(l_ref[:, 0:LRU_W], cw_ref[...], cb_ref[...], LRU_CONV_LEFT, _RowShifter(seq_len))
    xcb = xc.astype(BF16)
    half_xc = 0.5 * xc
    half_rate = []
    for d in range(2):
        z = -lam_ref[d:d + 1, :]
        half_rate.append((0.5 * LRU_C) * (jnp.maximum(z, 0.0) + jnp.log1p(jnp.exp(-jnp.abs(z)))))
    half_gb = 0.5 * gb_ref[...]
    for hd in range(LRU_HEADS):
        cols = slice(hd * LRU_BLK, (hd + 1) * LRU_BLK)
        pre = _dot(xcb[:, cols], gw_ref[hd].astype(BF16))
        for d, (a_scr, b_scr) in enumerate(((af_scr, bf_scr), (ab_scr, bb_scr))):
            base = d * 2 * LRU_BLK
            tr = jnp.tanh(pre[:, base:base + LRU_BLK] + half_gb[2 * d:2 * d + 1, cols])
            ti = jnp.tanh(pre[:, base + LRU_BLK:base + 2 * LRU_BLK] + half_gb[2 * d + 1:2 * d + 2, cols])
            hr = half_rate[d][:, cols]
            neg_log_a = tr * hr + hr
            a = jnp.exp(-neg_log_a)
            a_scr[:, cols] = a
            var = jnp.tanh(neg_log_a) * (1.0 + a * a)
            b_scr[:, cols] = jnp.where(var > 0.0, var * lax.rsqrt(var), 0.0) * ((ti + 1.0) * half_xc[:, cols])

    n_tiles = seq_len // LRU_SCAN_ROWS

    def tile_step(k, carry):
        out = list(carry)
        base_f = pl.multiple_of(k * LRU_SCAN_ROWS, LRU_SCAN_ROWS)
        base_b = pl.multiple_of((n_tiles - 1 - k) * LRU_SCAN_ROWS, LRU_SCAN_ROWS)
        for g in range(group):
            rows_f = pl.ds(g * seq_len + base_f, LRU_SCAN_ROWS)
            rows_b = pl.ds(g * seq_len + base_b, LRU_SCAN_ROWS)
            af, bf, yf = af_scr.at[rows_f, :], bf_scr.at[rows_f, :], yf_scr.at[rows_f, :]
            ab, bb, yb = ab_scr.at[rows_b, :], bb_scr.at[rows_b, :], yb_scr.at[rows_b, :]
            hf, hb = out[2 * g], out[2 * g + 1]
            for r in range(LRU_SCAN_ROWS):
                rb = LRU_SCAN_ROWS - 1 - r
                hf = af[r:r + 1, :] * hf + bf[r:r + 1, :]
                hb = ab[rb:rb + 1, :] * hb + bb[rb:rb + 1, :]
                yf[r:r + 1, :] = hf
                yb[rb:rb + 1, :] = hb
            out[2 * g], out[2 * g + 1] = hf, hb
        return tuple(out)

    init = []
    for g in range(group):
        init += [h0f_ref[g], h0b_ref[g]]
    final = lax.fori_loop(0, n_tiles, tile_step, tuple(init))
    for g in range(group):
        hf_ref[g] = final[2 * g]
        hb_ref[g] = final[2 * g + 1]
    lg = l_ref[:, LRU_W:]
    gelu = 0.5 * lg * (1.0 + jnp.tanh(math.sqrt(2.0 / math.pi) * (lg + 0.044715 * (lg * lg * lg))))
    o_ref[...] = (gelu * (yf_scr[...] + yb_scr[...])).astype(o_ref.dtype)


def _lru(l_all, cw, cb, gw, gb, lam, h0f, h0b, seq_len, n_seq, row0, group):
    rows = seq_len * group
    blk0 = row0 // rows
    vec = lambda b: (0, 0)
    return pl.pallas_call(
        functools.partial(_lru_kernel, seq_len=seq_len),
        grid=(n_seq // group,),
        in_specs=[
            pl.BlockSpec((rows, 2 * LRU_W), lambda b: (blk0 + b, 0)),
            pl.BlockSpec((LRU_CONV, LRU_W), vec),
            pl.BlockSpec((1, LRU_W), vec),
            pl.BlockSpec((LRU_HEADS, LRU_BLK, 4 * LRU_BLK), lambda b: (0, 0, 0)),
            pl.BlockSpec((4, LRU_W), vec),
            pl.BlockSpec((2, LRU_W), vec),
            pl.BlockSpec((group, 1, LRU_W), lambda b: (b, 0, 0)),
            pl.BlockSpec((group, 1, LRU_W), lambda b: (b, 0, 0)),
        ],
        out_specs=[
            pl.BlockSpec((rows, LRU_W), lambda b: (b, 0)),
            pl.BlockSpec((group, 1, LRU_W), lambda b: (b, 0, 0)),
            pl.BlockSpec((group, 1, LRU_W), lambda b: (b, 0, 0)),
        ],
        out_shape=[
            jax.ShapeDtypeStruct((n_seq * seq_len, LRU_W), BF16),
            jax.ShapeDtypeStruct((n_seq, 1, LRU_W), F32),
            jax.ShapeDtypeStruct((n_seq, 1, LRU_W), F32),
        ],
        scratch_shapes=[pltpu.VMEM((rows, LRU_W), F32)] * 6,
        compiler_params=_params(("arbitrary",)),
        name="rg_lru",
    )(l_all, cw, cb, gw, gb, lam, h0f, h0b)


def _hy_filter_kernel(z_ref, t_ref, ckh_ref, ckl_ref, alt_ref, w1_ref, b1_ref, w2_ref, b2_ref, w3_ref, fr_ref,
                      dec_ref, skip_ref, kf_ref):
    z = jnp.sin(fr_ref[0:1, :] * (_dot_f32(z_ref[...], w1_ref[...]) + b1_ref[...]))
    z = jnp.sin(fr_ref[1:2, :] * (_dot_f32(z, w2_ref[...]) + b2_ref[...]))
    zh, zl = _split_bf16(z)
    filt = _dot3(zh, zl, w3_ref[...]) * jnp.exp(-t_ref[...] * jnp.abs(dec_ref[...]))
    filt = filt / jnp.sum(jnp.abs(filt), axis=0, keepdims=True)
    n = t_ref.shape[0]
    kf = _dot3(ckh_ref[...], ckl_ref[...], filt)
    kn = jnp.sum(alt_ref[...] * filt, axis=0, keepdims=True)
    for o in range(HY_ORDER):
        cols = slice(o * HY_W, (o + 1) * HY_W)
        skip = skip_ref[o:o + 1, :]
        kf_ref[o, 0:n, :] = kf[:, cols] + skip
        kf_ref[o, n:, :] = kf[:, cols] + skip
        kf_ref[o, n:n + 1, :] = kn[:, cols] + skip


def _hy_filter(consts, w1, b1, w2, b2, w3, freq, decay, skip, seq_len):
    ckh, ckl = _split_bf16(consts["ck"])
    return pl.pallas_call(
        _hy_filter_kernel,
        out_shape=jax.ShapeDtypeStruct((HY_ORDER, 2 * seq_len, HY_W), F32),
        compiler_params=pltpu.CompilerParams(vmem_limit_bytes=VMEM_LIMIT_BYTES),
        name="hyena_filter",
    )(consts["z"], consts["t"], ckh, ckl, consts["alt_w"], w1, b1, w2, b2, w3, freq, decay, skip)


def _longconv(u, fwd_ref, inv_ref, kf2):
    spec = _dot(fwd_ref[...], u.astype(BF16)) * kf2
    return _dot(inv_ref[...], spec.astype(BF16))


def _hyena_kernel(x1_ref, x2_ref, v_ref, cw_ref, cb_ref, kf_ref, fwd_ref, inv_ref, o_ref, *, seq_len, chain_w):
    shift = _RowShifter(seq_len)
    v = _dwconv(v_ref[...], cw_ref[2], cb_ref[2:3, :], HY_CONV_LEFT, shift)
    gates = [_dwconv(g_ref[...], cw_ref[o], cb_ref[o:o + 1, :], HY_CONV_LEFT, shift)
             for o, g_ref in enumerate((x1_ref, x2_ref))]
    for s in range(v.shape[0] // seq_len):
        rows = slice(s * seq_len, (s + 1) * seq_len)
        for c0 in range(0, v.shape[1], chain_w):
            cols = slice(c0, c0 + chain_w)
            z = v[rows, cols]
            for o in range(HY_ORDER):
                z = gates[o][rows, cols] * _longconv(z, fwd_ref, inv_ref, kf_ref[o, :, cols])
            o_ref[rows, cols] = z.astype(o_ref.dtype)


def _hyena(hy_all, cw, cb, kf, consts, seq_len, n_seq, row0, group, block_w, chain_w):
    rows = seq_len * group
    blk0 = row0 // rows
    nc = HY_W // block_w
    const = dict(pipeline_mode=pl.Buffered(1))
    in_specs = [pl.BlockSpec((rows, block_w), lambda b, c, g=g: (blk0 + b, g * nc + c)) for g in range(3)]
    in_specs += [
        pl.BlockSpec((HY_ORDER + 1, HY_CONV, block_w), lambda b, c: (0, 0, c)),
        pl.BlockSpec((HY_ORDER + 1, block_w), lambda b, c: (0, c)),
        pl.BlockSpec((HY_ORDER, 2 * seq_len, block_w), lambda b, c: (0, 0, c)),
        pl.BlockSpec((2 * seq_len, seq_len), lambda b, c: (0, 0), **const),
        pl.BlockSpec((seq_len, 2 * seq_len), lambda b, c: (0, 0), **const),
    ]
    return pl.pallas_call(
        functools.partial(_hyena_kernel, seq_len=seq_len, chain_w=chain_w),
        grid=(n_seq // group, nc),
        in_specs=in_specs,
        out_specs=pl.BlockSpec((rows, block_w), lambda b, c: (b, c)),
        out_shape=jax.ShapeDtypeStruct((n_seq * seq_len, HY_W), BF16),
        compiler_params=_params(("arbitrary", "arbitrary")),
        name="hyena",
    )(hy_all, hy_all, hy_all, cw.reshape(HY_CONV, HY_ORDER + 1, HY_W).transpose(1, 0, 2),
      cb.reshape(HY_ORDER + 1, HY_W), kf, consts["fwd"].astype(BF16), consts["inv"].astype(BF16))


def _pool_kernel(p_ref, w_ref, s_ref, o_ref, *, seq_len):
    shift = _RowShifter(seq_len)
    t = lax.broadcasted_iota(jnp.int32, (p_ref.shape[0], 1), 0) & (seq_len - 1)
    for gi, win in enumerate(POOL_WINDOWS):
        cols = slice(gi * POOL_GW, (gi + 1) * POOL_GW)
        x = p_ref[:, cols]
        half = win // 2
        back, fwd = x, x
        m = 1
        while m < half:
            back = back + shift(back, -m)
            fwd = fwd + shift(fwd, m)
            m *= 2
        s = shift(back, -1) + fwd
        cnt = (jnp.minimum(t + half, seq_len) - jnp.maximum(t - half, 0)).astype(F32)
        y = _dot((s / cnt - x).astype(BF16), w_ref[gi].astype(BF16))
        o_ref[:, cols] = (y * s_ref[:, cols]).astype(o_ref.dtype)


def _pool(p_all, w, scale, seq_len, n_seq, row0, group):
    rows = seq_len * group
    blk0 = row0 // rows
    return pl.pallas_call(
        functools.partial(_pool_kernel, seq_len=seq_len),
        grid=(n_seq // group,),
        in_specs=[
            pl.BlockSpec((rows, POOL_W), lambda b: (blk0 + b, 0)),
            pl.BlockSpec((len(POOL_WINDOWS), POOL_GW, POOL_GW), lambda b: (0, 0, 0)),
            pl.BlockSpec((1, POOL_W), lambda b: (0, 0)),
        ],
        out_specs=pl.BlockSpec((rows, POOL_W), lambda b: (b, 0)),
        out_shape=jax.ShapeDtypeStruct((n_seq * seq_len, POOL_W), BF16),
        compiler_params=_params(("arbitrary",)),
        name="multi_pool",
    )(p_all, w, scale)


def _rope_tables():
    rows = DEC_SEQ // GRID_W
    r = np.repeat(np.arange(rows), GRID_W).astype(np.float64)
    col = np.tile(np.arange(GRID_W), rows).astype(np.float64)
    half = HEAD_DIM // 2
    inv = ROPE_THETA ** (-np.arange(0, half, 2, dtype=np.float64) / half)
    ang = np.concatenate([r[:, None] * inv, col[:, None] * inv], axis=-1)
    cos = np.repeat(np.cos(ang), 2, axis=-1)
    sin = np.repeat(np.sin(ang), 2, axis=-1) * np.tile(np.array([-1.0, 1.0]), half)
    return jnp.asarray(cos, F32), jnp.asarray(sin, F32)


def _hyena_tables(n):
    idx = np.arange(n, dtype=np.float64)
    t = idx / max(n - 1, 1)
    bands = np.linspace(1e-4, HY_BANDS - 1, HY_BANDS)
    f = 2.0 * math.pi * idx[:, None] * bands[None, :] / n
    z = np.zeros((n, HY_EMB_PAD))
    z[:, 0] = t
    z[:, 1:1 + HY_BANDS] = np.cos(f)
    z[:, 1 + HY_BANDS:HY_EMB] = -np.sin(f)
    ang = math.pi * ((idx[:, None] * idx[None, :]) % (2 * n)) / n
    wgt_n = np.where(idx == 0, 1.0, 2.0)
    alt = np.where(idx % 2 == 0, 1.0, -1.0)
    fwd = np.concatenate([np.cos(ang), np.sin(ang)], axis=0)
    inv = np.concatenate([np.cos(ang) * wgt_n[None, :], np.sin(ang) * wgt_n[None, :]], axis=1) / (2 * n)
    fwd[n, :] = alt
    inv[:, n] = alt / (2 * n)
    return {
        "z": jnp.asarray(z, F32),
        "t": jnp.asarray(t[:, None], F32),
        "ck": jnp.asarray(np.cos(ang) * wgt_n[None, :], F32),
        "alt_w": jnp.asarray((alt * wgt_n)[:, None], F32),
        "fwd": jnp.asarray(fwd, F32),
        "inv": jnp.asarray(inv, F32),
    }


def kernel(x_prompt, x_sample, cache_k, cache_v, state_lru_fwd, state_lru_bwd, c, c_ctx, norm_g, w_mod, b_mod, ffn_w13, ffn_w2, ab_w_in, ab_q_norm, ab_k_norm, lru_conv_w, lru_conv_b, lru_gate_w, lru_gate_b, lru_lambda, ab_w_out, cd_w_in, hy_conv_w, hy_conv_b, hy_w1, hy_b1, hy_w2, hy_b2, hy_w3, hy_freq, hy_decay, hy_skip, pool_w, pool_scale, cd_w_out):
    xs = [x_prompt.reshape(N_CTX, D_MODEL), x_sample.reshape(N_LAT, D_MODEL)]
    cond = jnp.concatenate([c_ctx[None], c, jnp.zeros((COND_PAD - N_COND, D_MODEL), F32)], axis=0)
    mods = _modulation(cond, w_mod, b_mod)[:, :N_COND].reshape(DEPTH, N_COND, N_MOD, D_MODEL)
    rope_cos, rope_sin = _rope_tables()

    k_list, v_list, hf_list, hb_list = [], [], [], []
    for l in range(DEPTH):
        mod = mods[l]
        g = norm_g[l]
        (x,) = _ffn(xs, mod, g[0:1], ffn_w13, ffn_w2, l, 0, 0, [N_TOK])
        if l % 2 == 0:
            e = l // 2
            qkv, lxg = _inproj(x, mod, g[1:2], ab_w_in[e], (QKV_W, 2 * LRU_W))
            qn, kn = ab_q_norm[e][None], ab_k_norm[e][None]
            attn_c, kc, vc = _attn_ctx(qkv, qn, kn)
            attn_l = _attn_lat(qkv, cache_k[:, e].reshape(DEC_BATCH, PAST_LEN, KV_W),
                               cache_v[:, e].reshape(DEC_BATCH, PAST_LEN, KV_W), qn, kn, rope_cos, rope_sin)
            gw = 0.5 * jnp.transpose(lru_gate_w[e], (2, 3, 0, 1, 4)).reshape(LRU_HEADS, LRU_BLK, 4 * LRU_BLK)
            gb = lru_gate_b[e].reshape(4, LRU_W)
            lru_args = (lru_conv_w[e], lru_conv_b[e][None], gw, gb, lru_lambda[e])
            zeros = jnp.zeros((BATCH, 1, LRU_W), F32)
            rec_c, hf, hb = _lru(lxg, *lru_args, zeros, zeros, SEQ, BATCH, 0, LRU_CTX_GROUP)
            rec_l, _, _ = _lru(lxg, *lru_args, state_lru_fwd[:, e][:, None], state_lru_bwd[:, e][:, None],
                               DEC_SEQ, DEC_BATCH, N_CTX, 1)
            x = _outproj(x, mod, attn_c, attn_l, rec_c, rec_l, ab_w_out[e])
            k_list.append(kc.reshape(BATCH, SEQ, N_KV_HEADS, HEAD_DIM))
            v_list.append(vc.reshape(BATCH, SEQ, N_KV_HEADS, HEAD_DIM))
            hf_list.append(hf.reshape(BATCH, LRU_W))
            hb_list.append(hb.reshape(BATCH, LRU_W))
        else:
            o = l // 2
            hy, pw = _inproj(x, mod, g[1:2], cd_w_in[o], ((HY_ORDER + 1) * HY_W, POOL_W))
            w1 = jnp.zeros((HY_EMB_PAD, HY_FH), F32).at[:HY_EMB].set(hy_w1[o])
            z_out, p_out = [], []
            for seq_len, n_seq, row0, hy_cfg, pool_group in (
                    (SEQ, BATCH, 0, (HY_CTX_GROUP, HY_W, HY_W), POOL_CTX_GROUP),
                    (DEC_SEQ, DEC_BATCH, N_CTX, (1, HY_LAT_BLOCK_W, HY_CHAIN_W), 1)):
                consts = _hyena_tables(seq_len)
                kf = _hy_filter(consts, w1, hy_b1[o][None], hy_w2[o], hy_b2[o][None], hy_w3[o], hy_freq[o],
                                hy_decay[o][None], hy_skip[o], seq_len)
                z_out.append(_hyena(hy, hy_conv_w[o], hy_conv_b[o], kf, consts, seq_len, n_seq, row0, *hy_cfg))
                p_out.append(_pool(pw, pool_w[o], pool_scale[o][None], seq_len, n_seq, row0, pool_group))
            x = _outproj(x, mod, z_out[0], z_out[1], p_out[0], p_out[1], cd_w_out[o])
        xs = _ffn([x], mod, g[2:3], ffn_w13, ffn_w2, l, 1, 2, [N_TOK] if l + 1 < DEPTH else [N_CTX, N_LAT])

    y_prompt = xs[0].reshape(BATCH, SEQ, D_MODEL)
    y_sample = xs[1].reshape(DEC_BATCH, DEC_SEQ, D_MODEL)
    return (y_prompt, y_sample, jnp.stack(k_list, axis=1), jnp.stack(v_list, axis=1),
            jnp.stack(hf_list, axis=1), jnp.stack(hb_list, axis=1))
```

```python
import functools
import math

import numpy as np
import jax
import jax.numpy as jnp
from jax import lax
from jax.experimental import pallas as pl
from jax.experimental.pallas import tpu as pltpu

F32 = jnp.float32
BF16 = jnp.bfloat16

D_MODEL = 2048
BATCH = 32
SEQ = 256
DEPTH = 2
DEC_BATCH = 2
DEC_SEQ = 1024
PAST_LEN = 512
GRID_W = 64
N_MOD = 9
NORM_EPS = 1e-6
D_FF = 5632
HEAD_DIM = 128
N_Q_HEADS = 8
N_KV_HEADS = 2
Q_PER_KV = N_Q_HEADS // N_KV_HEADS
ATTN_W = N_Q_HEADS * HEAD_DIM
KV_W = N_KV_HEADS * HEAD_DIM
QKV_W = ATTN_W + 2 * KV_W
ROPE_THETA = 10000.0
LRU_W = 1024
LRU_HEADS = 8
LRU_BLK = LRU_W // LRU_HEADS
LRU_CONV = 4
LRU_CONV_LEFT = 2
LRU_C = 8.0
HY_W = 1024
HY_ORDER = 2
HY_CONV = 3
HY_CONV_LEFT = 1
HY_EMB = 33
HY_EMB_PAD = 128
HY_BANDS = (HY_EMB - 1) // 2
HY_FH = 64
POOL_W = 1024
POOL_WINDOWS = (2, 4, 8, 16)
POOL_GW = POOL_W // len(POOL_WINDOWS)

N_CTX = BATCH * SEQ
N_LAT = DEC_BATCH * DEC_SEQ
N_TOK = N_CTX + N_LAT
N_COND = 1 + DEC_BATCH
COND_PAD = 8

VMEM_LIMIT_BYTES = 56 * 1024 * 1024

ROW_TILE = 512
FFN_ROW_TILE = 1024
FF_TILE = 512
FFN_NORM_ROWS = 32
MOD_TILE = 1024
Q_TILE = 256
ATTN_CTX_GROUP = 4
ATTN_CTX_HEADS_PER_CHAIN = 2
ATTN_LAT_HEADS_PER_CHAIN = 1
LRU_CTX_GROUP = 2
LRU_SCAN_ROWS = 8
HY_CTX_GROUP = 2
POOL_CTX_GROUP = 4
HY_CHAIN_W = 512
HY_LAT_BLOCK_W = 512


def _params(sem):
    return pltpu.CompilerParams(dimension_semantics=sem, vmem_limit_bytes=VMEM_LIMIT_BYTES)


def _cond_index(i, tile):
    return jnp.maximum((i * tile) // DEC_SEQ - (N_CTX // DEC_SEQ - 1), 0)


def _sigmoid(x):
    return 1.0 / (1.0 + jnp.exp(-x))


def _dot(a, b):
    return jnp.dot(a, b, preferred_element_type=F32)


def _dot_f32(a, b):
    return jnp.dot(a, b, preferred_element_type=F32, precision=lax.Precision.HIGHEST)


def _split_bf16(x):
    hi = x.astype(BF16)
    return hi, (x - hi.astype(F32)).astype(BF16)


def _dot3(a_hi, a_lo, b):
    b_hi, b_lo = _split_bf16(b)
    return _dot(a_hi, b_hi) + (_dot(a_hi, b_lo) + _dot(a_lo, b_hi))


def _modulated_norm(x, g, mod_ref, j):
    ms = jnp.mean(x * x, axis=-1, keepdims=True)
    y = x * lax.rsqrt(ms + NORM_EPS) * g
    return y * (1.0 + mod_ref[0, 3 * j + 1:3 * j + 2, :]) + mod_ref[0, 3 * j:3 * j + 1, :]


def _head_norm(x, g):
    ms = jnp.mean(x * x, axis=-1, keepdims=True)
    return x * lax.rsqrt(ms + NORM_EPS) * g


class _RowShifter:
    def __init__(self, period):
        self.period = period
        self._masks = {}

    def __call__(self, x, off):
        if off == 0:
            return x
        key = (x.shape, off)
        if key not in self._masks:
            t = lax.broadcasted_iota(jnp.int32, x.shape, 0) & (self.period - 1)
            self._masks[key] = t >= -off if off < 0 else t < self.period - off
        return jnp.where(self._masks[key], pltpu.roll(x, (-off) % x.shape[0], axis=0), 0.0)


def _dwconv(x, w, b, left, shift):
    acc = b + shift(x, -left) * w[0:1, :]
    for j in range(1, w.shape[0]):
        acc = acc + shift(x, j - left) * w[j:j + 1, :]
    return acc


def _mod_kernel(c_ref, w_ref, b_ref, o_ref):
    c = c_ref[...]
    s = (c * _sigmoid(c)).astype(BF16)
    o_ref[0] = _dot(s, w_ref[0].astype(BF16)) + b_ref[0]


def _modulation(cond, w_mod, b_mod):
    n = N_MOD * D_MODEL
    return pl.pallas_call(
        _mod_kernel,
        grid=(DEPTH, n // MOD_TILE),
        in_specs=[
            pl.BlockSpec((COND_PAD, D_MODEL), lambda l, j: (0, 0)),
            pl.BlockSpec((1, D_MODEL, MOD_TILE), lambda l, j: (l, 0, j)),
            pl.BlockSpec((1, 1, MOD_TILE), lambda l, j: (l, 0, j)),
        ],
        out_specs=pl.BlockSpec((1, COND_PAD, MOD_TILE), lambda l, j: (l, 0, j)),
        out_shape=jax.ShapeDtypeStruct((DEPTH, COND_PAD, n), F32),
        compiler_params=_params(("arbitrary", "arbitrary")),
        name="modulation",
    )(cond, w_mod, b_mod.reshape(DEPTH, 1, n))


def _ffn_kernel(*refs, j, n_in, n_out, in_split, out_split):
    x_hbm = refs[:n_in]
    mod_ref, g_ref, w1_ref, w3_ref, w2_ref = refs[n_in:n_in + 5]
    o_hbm = refs[n_in + 5:n_in + 5 + n_out]
    buf, h_ref, sem_in, sem_out = refs[n_in + 5 + n_out:]
    i, f = pl.program_id(0), pl.program_id(1)
    n, nf = pl.num_programs(0), pl.num_programs(1)
    slot = lax.rem(i, 2)
    acc = buf.at[slot]

    def tile_rows(t):
        return pl.ds(pl.multiple_of(t * FFN_ROW_TILE, FFN_ROW_TILE), FFN_ROW_TILE)

    def in_copy(src, t, s):
        return pltpu.make_async_copy(src.at[tile_rows(t), :], buf.at[s], sem_in.at[s])

    def out_copy(dst, t, s):
        return pltpu.make_async_copy(buf.at[s], dst.at[tile_rows(t), :], sem_out.at[s])

    def start_split(copy, refs_, split, t, s):
        if len(refs_) == 1:
            copy(refs_[0], t, s).start()
        else:
            @pl.when(t < split)
            def _():
                copy(refs_[0], t, s).start()

            @pl.when(t >= split)
            def _():
                copy(refs_[1], t - split, s).start()

    @pl.when(f == 0)
    def _():
        @pl.when(i == 0)
        def _():
            start_split(in_copy, x_hbm, in_split, i, slot)

        in_copy(x_hbm[0], 0, slot).wait()
        gain = g_ref[...] * (1.0 + mod_ref[0, 3 * j + 1:3 * j + 2, :])
        shift = mod_ref[0, 3 * j:3 * j + 1, :]

        def chunk(r, carry):
            rows = pl.ds(pl.multiple_of(r * FFN_NORM_ROWS, FFN_NORM_ROWS), FFN_NORM_ROWS)
            x = acc[rows, :]
            ms = jnp.mean(x * x, axis=-1, keepdims=True)
            h_ref[rows, :] = (x * lax.rsqrt(ms + NORM_EPS) * gain + shift).astype(BF16)
            return carry

        lax.fori_loop(0, FFN_ROW_TILE // FFN_NORM_ROWS, chunk, 0, unroll=4)

    @pl.when((f == 1) & (i + 1 < n))
    def _():
        @pl.when(i >= 1)
        def _():
            out_copy(o_hbm[0], 0, 1 - slot).wait()

        start_split(in_copy, x_hbm, in_split, i + 1, 1 - slot)

    h = h_ref[...]
    gt = _dot(h, w1_ref[...].astype(BF16))
    up = _dot(h, w3_ref[...].astype(BF16))
    a = (gt * _sigmoid(gt) * up).astype(BF16)
    half_gate = 0.5 * mod_ref[0, 3 * j + 2:3 * j + 3, :]
    acc[...] += half_gate * _dot(a, w2_ref[...].astype(BF16))

    @pl.when(f == nf - 1)
    def _():
        start_split(out_copy, o_hbm, out_split, i, slot)

        @pl.when(i == n - 1)
        def _():
            out_copy(o_hbm[0], 0, slot).wait()
            out_copy(o_hbm[0], 0, 1 - slot).wait()


def _ffn(xs, mod, g, w13, w2, l, sub, j, out_rows):
    nf = D_FF // FF_TILE
    n_tiles = N_TOK // FFN_ROW_TILE
    assert nf >= 2 and n_tiles >= 2 and sum(x.shape[0] for x in xs) == N_TOK and sum(out_rows) == N_TOK
    assert all(x.shape[0] % FFN_ROW_TILE == 0 for x in xs) and all(r % FFN_ROW_TILE == 0 for r in out_rows)
    any_spec = pl.BlockSpec(memory_space=pl.ANY)
    outs = pl.pallas_call(
        functools.partial(_ffn_kernel, j=j, n_in=len(xs), n_out=len(out_rows),
                          in_split=xs[0].shape[0] // FFN_ROW_TILE, out_split=out_rows[0] // FFN_ROW_TILE),
        grid=(n_tiles, nf),
        in_specs=[any_spec] * len(xs) + [
            pl.BlockSpec((1, N_MOD, D_MODEL), lambda i, f: (_cond_index(i, FFN_ROW_TILE), 0, 0)),
            pl.BlockSpec((1, D_MODEL), lambda i, f: (0, 0)),
            pl.BlockSpec((None, None, D_MODEL, FF_TILE), lambda i, f: (l, sub, 0, f)),
            pl.BlockSpec((None, None, D_MODEL, FF_TILE), lambda i, f: (l, sub, 0, nf + f)),
            pl.BlockSpec((None, None, FF_TILE, D_MODEL), lambda i, f: (l, sub, f, 0)),
        ],
        out_specs=[any_spec] * len(out_rows),
        out_shape=[jax.ShapeDtypeStruct((r, D_MODEL), F32) for r in out_rows],
        scratch_shapes=[
            pltpu.VMEM((2, FFN_ROW_TILE, D_MODEL), F32),
            pltpu.VMEM((FFN_ROW_TILE, D_MODEL), BF16),
            pltpu.SemaphoreType.DMA((2,)),
            pltpu.SemaphoreType.DMA((2,)),
        ],
        compiler_params=_params(("arbitrary", "arbitrary")),
        name="ffn",
    )(*xs, mod, g, w13, w13, w2)
    return outs


W_CHUNK = 512


def _for_each_weight_chunk(w_hbm, w_scr, stage, sem, prepare, body):
    n_chunks = w_hbm.shape[1] // W_CHUNK
    chunks = [slice(c * W_CHUNK, (c + 1) * W_CHUNK) for c in range(n_chunks)]
    first = pl.program_id(0) == 0

    def copy(c, s):
        return pltpu.make_async_copy(w_hbm.at[:, pl.ds(c * W_CHUNK, W_CHUNK)], stage.at[s], sem.at[s])

    @pl.when(first)
    def _():
        copy(0, 0).start()
        lhs = prepare()
        for c, cols in enumerate(chunks):
            s = c % 2
            if c + 1 < n_chunks:
                copy(c + 1, 1 - s).start()
            copy(c, s).wait()
            w_scr[:, cols] = stage[s].astype(BF16)
            body(lhs, cols)

    @pl.when(jnp.logical_not(first))
    def _():
        lhs = prepare()
        for cols in chunks:
            body(lhs, cols)


def _weight_scratch(k, n):
    assert n % W_CHUNK == 0
    return [pltpu.VMEM((k, n), BF16), pltpu.VMEM((2, k, W_CHUNK), F32), pltpu.SemaphoreType.DMA((2,))]


def _inproj_kernel(x_ref, mod_ref, g_ref, w_hbm, *rest, widths):
    o_refs, (w_ref, stage, sem) = rest[:len(widths)], rest[len(widths):]
    starts = [sum(widths[:k]) for k in range(len(widths))]

    def normalise():
        return _modulated_norm(x_ref[...], g_ref[...], mod_ref, 1).astype(BF16)

    def project(h, cols):
        k = max(k for k, s0 in enumerate(starts) if s0 <= cols.start)
        o_refs[k][:, cols.start - starts[k]:cols.stop - starts[k]] = _dot(h, w_ref[:, cols])

    _for_each_weight_chunk(w_hbm, w_ref, stage, sem, normalise, project)


def _inproj(x, mod, g, w, widths):
    n = sum(widths)
    return pl.pallas_call(
        functools.partial(_inproj_kernel, widths=widths),
        grid=(N_TOK // ROW_TILE,),
        in_specs=[
            pl.BlockSpec((ROW_TILE, D_MODEL), lambda i: (i, 0)),
            pl.BlockSpec((1, N_MOD, D_MODEL), lambda i: (_cond_index(i, ROW_TILE), 0, 0)),
            pl.BlockSpec((1, D_MODEL), lambda i: (0, 0)),
            pl.BlockSpec(memory_space=pl.ANY),
        ],
        out_specs=[pl.BlockSpec((ROW_TILE, wd), lambda i: (i, 0)) for wd in widths],
        out_shape=[jax.ShapeDtypeStruct((N_TOK, wd), F32) for wd in widths],
        scratch_shapes=_weight_scratch(D_MODEL, n),
        compiler_params=_params(("arbitrary",)),
        name="mixer_in_proj",
    )(x, mod, g, w)


def _outproj_kernel(x_ref, mod_ref, ac_ref, al_ref, bc_ref, bl_ref, w_hbm, o_ref, w_ref, stage, sem):
    wa = w_ref.shape[0] // 2
    is_ctx = pl.program_id(0) < N_CTX // ROW_TILE

    def pick():
        return jnp.where(is_ctx, ac_ref[...], al_ref[...]), jnp.where(is_ctx, bc_ref[...], bl_ref[...])

    def project(ab, cols):
        y = _dot(ab[0], w_ref[0:wa, cols]) + _dot(ab[1], w_ref[wa:, cols])
        o_ref[:, cols] = x_ref[:, cols] + mod_ref[0, 5:6, cols] * y

    _for_each_weight_chunk(w_hbm, w_ref, stage, sem, pick, project)


def _outproj(x, mod, a_ctx, a_lat, b_ctx, b_lat, w):
    wa = a_ctx.shape[1]
    nc = N_CTX // ROW_TILE
    ctx_spec = pl.BlockSpec((ROW_TILE, wa), lambda i: (jnp.minimum(i, nc - 1), 0))
    lat_spec = pl.BlockSpec((ROW_TILE, wa), lambda i: (jnp.maximum(i - nc, 0), 0))
    return pl.pallas_call(
        _outproj_kernel,
        grid=(N_TOK // ROW_TILE,),
        in_specs=[
            pl.BlockSpec((ROW_TILE, D_MODEL), lambda i: (i, 0)),
            pl.BlockSpec((1, N_MOD, D_MODEL), lambda i: (_cond_index(i, ROW_TILE), 0, 0)),
            ctx_spec, lat_spec, ctx_spec, lat_spec,
            pl.BlockSpec(memory_space=pl.ANY),
        ],
        out_specs=pl.BlockSpec((ROW_TILE, D_MODEL), lambda i: (i, 0)),
        out_shape=jax.ShapeDtypeStruct((N_TOK, D_MODEL), F32),
        scratch_shapes=_weight_scratch(2 * wa, D_MODEL),
        compiler_params=_params(("arbitrary",)),
        name="mixer_out_proj",
    )(x, mod, a_ctx, a_lat, b_ctx, b_lat, w)


def _rope(x, cos, sin):
    lane = lax.broadcasted_iota(jnp.int32, x.shape, 1)
    swapped = jnp.where((lane & 1) == 0, pltpu.roll(x, HEAD_DIM - 1, axis=1), pltpu.roll(x, 1, axis=1))
    return x * cos + swapped * sin


def _attend(q_heads, k, v_ones, o_ref, rows, col0, heads_per_chain):
    t = q_heads[0].shape[0]
    for i0 in range(0, len(q_heads), heads_per_chain):
        heads = q_heads[i0:i0 + heads_per_chain]
        q = jnp.concatenate(heads, axis=0)
        s = lax.dot_general(q, k, (((1,), (1,)), ((), ())), preferred_element_type=F32) * (HEAD_DIM ** -0.5)
        e = jnp.exp(s - jnp.max(s, axis=-1, keepdims=True)).astype(BF16)
        oe = _dot(e, v_ones)
        o = oe[:, 0:HEAD_DIM] / oe[:, HEAD_DIM:]
        for i in range(len(heads)):
            c0 = col0 + (i0 + i) * HEAD_DIM
            o_ref[rows, c0:c0 + HEAD_DIM] = o[i * t:(i + 1) * t].astype(o_ref.dtype)


def _attn_ctx_kernel(qkv_ref, qn_ref, kn_ref, o_ref, kc_ref, vc_ref):
    ones = jnp.ones((SEQ, HEAD_DIM), BF16)
    for s in range(ATTN_CTX_GROUP):
        rows = slice(s * SEQ, (s + 1) * SEQ)
        for kv in range(N_KV_HEADS):
            kcol = ATTN_W + kv * HEAD_DIM
            vcol = ATTN_W + KV_W + kv * HEAD_DIM
            k = _head_norm(qkv_ref[rows, kcol:kcol + HEAD_DIM], kn_ref[...])
            v = qkv_ref[rows, vcol:vcol + HEAD_DIM]
            cache_rows = pl.ds(s * SEQ * N_KV_HEADS + kv, SEQ, stride=N_KV_HEADS)
            kc_ref[cache_rows, :] = k
            vc_ref[cache_rows, :] = v
            qs = []
            for i in range(Q_PER_KV):
                qcol = (kv * Q_PER_KV + i) * HEAD_DIM
                qs.append(_head_norm(qkv_ref[rows, qcol:qcol + HEAD_DIM], qn_ref[...]).astype(BF16))
            _attend(qs, k.astype(BF16), jnp.concatenate([v.astype(BF16), ones], axis=1), o_ref, rows,
                    kv * Q_PER_KV * HEAD_DIM, ATTN_CTX_HEADS_PER_CHAIN)


def _attn_ctx(qkv, qn, kn):
    rows = ATTN_CTX_GROUP * SEQ
    return pl.pallas_call(
        _attn_ctx_kernel,
        grid=(BATCH // ATTN_CTX_GROUP,),
        in_specs=[
            pl.BlockSpec((rows, QKV_W), lambda b: (b, 0)),
            pl.BlockSpec((1, HEAD_DIM), lambda b: (0, 0)),
            pl.BlockSpec((1, HEAD_DIM), lambda b: (0, 0)),
        ],
        out_specs=[
            pl.BlockSpec((rows, ATTN_W), lambda b: (b, 0)),
            pl.BlockSpec((rows * N_KV_HEADS, HEAD_DIM), lambda b: (b, 0)),
            pl.BlockSpec((rows * N_KV_HEADS, HEAD_DIM), lambda b: (b, 0)),
        ],
        out_shape=[
            jax.ShapeDtypeStruct((N_CTX, ATTN_W), BF16),
            jax.ShapeDtypeStruct((N_CTX * N_KV_HEADS, HEAD_DIM), F32),
            jax.ShapeDtypeStruct((N_CTX * N_KV_HEADS, HEAD_DIM), F32),
        ],
        compiler_params=_params(("arbitrary",)),
        name="attention_context",
    )(qkv, qn, kn)


def _attn_lat_kernel(qkv_ref, ck_ref, cv_ref, qn_ref, kn_ref, cos_ref, sin_ref, o_ref, k_scr, v_scr):
    qi = pl.program_id(1)

    @pl.when(qi == 0)
    def _():
        k_scr[0:PAST_LEN, :] = ck_ref[0].astype(BF16)
        for kv in range(N_KV_HEADS):
            kcol = ATTN_W + kv * HEAD_DIM
            vcol = ATTN_W + KV_W + kv * HEAD_DIM
            k = _head_norm(qkv_ref[:, kcol:kcol + HEAD_DIM], kn_ref[...])
            k = _rope(k, cos_ref[...], sin_ref[...])
            k_scr[PAST_LEN:, kv * HEAD_DIM:(kv + 1) * HEAD_DIM] = k.astype(BF16)
            v0 = 2 * kv * HEAD_DIM
            v_scr[0:PAST_LEN, v0:v0 + HEAD_DIM] = cv_ref[0, :, kv * HEAD_DIM:(kv + 1) * HEAD_DIM].astype(BF16)
            v_scr[PAST_LEN:, v0:v0 + HEAD_DIM] = qkv_ref[:, vcol:vcol + HEAD_DIM].astype(BF16)
            v_scr[:, v0 + HEAD_DIM:v0 + 2 * HEAD_DIM] = jnp.ones((PAST_LEN + DEC_SEQ, HEAD_DIM), BF16)

    r0 = pl.multiple_of(qi * Q_TILE, Q_TILE)
    cos = cos_ref[pl.ds(r0, Q_TILE), :]
    sin = sin_ref[pl.ds(r0, Q_TILE), :]
    for kv in range(N_KV_HEADS):
        qs = []
        for i in range(Q_PER_KV):
            qcol = (kv * Q_PER_KV + i) * HEAD_DIM
            q = _head_norm(qkv_ref[pl.ds(r0, Q_TILE), qcol:qcol + HEAD_DIM], qn_ref[...])
            qs.append(_rope(q, cos, sin).astype(BF16))
        _attend(qs, k_scr[:, kv * HEAD_DIM:(kv + 1) * HEAD_DIM], v_scr[:, 2 * kv * HEAD_DIM:2 * (kv + 1) * HEAD_DIM],
                o_ref, slice(None), kv * Q_PER_KV * HEAD_DIM, ATTN_LAT_HEADS_PER_CHAIN)


def _attn_lat(qkv, cache_k, cache_v, qn, kn, cos, sin):
    row0 = N_CTX // DEC_SEQ
    qt = DEC_SEQ // Q_TILE
    return pl.pallas_call(
        _attn_lat_kernel,
        grid=(DEC_BATCH, qt),
        in_specs=[
            pl.BlockSpec((DEC_SEQ, QKV_W), lambda b, q: (row0 + b, 0)),
            pl.BlockSpec((1, PAST_LEN, KV_W), lambda b, q: (b, 0, 0)),
            pl.BlockSpec((1, PAST_LEN, KV_W), lambda b, q: (b, 0, 0)),
            pl.BlockSpec((1, HEAD_DIM), lambda b, q: (0, 0)),
            pl.BlockSpec((1, HEAD_DIM), lambda b, q: (0, 0)),
            pl.BlockSpec((DEC_SEQ, HEAD_DIM), lambda b, q: (0, 0)),
            pl.BlockSpec((DEC_SEQ, HEAD_DIM), lambda b, q: (0, 0)),
        ],
        out_specs=pl.BlockSpec((Q_TILE, ATTN_W), lambda b, q: (b * qt + q, 0)),
        out_shape=jax.ShapeDtypeStruct((N_LAT, ATTN_W), BF16),
        scratch_shapes=[pltpu.VMEM((PAST_LEN + DEC_SEQ, KV_W), BF16),
                        pltpu.VMEM((PAST_LEN + DEC_SEQ, 2 * KV_W), BF16)],
        compiler_params=_params(("arbitrary", "arbitrary")),
        name="attention_latent",
    )(qkv, cache_k, cache_v, qn, kn, cos, sin)


def _lru_kernel(l_ref, cw_ref, cb_ref, gw_ref, gb_ref, lam_ref, h0f_ref, h0b_ref, o_ref, hf_ref, hb_ref,
                af_scr, bf_scr, ab_scr, bb_scr, yf_scr, yb_scr, *, seq_len):
    group = l_ref.shape[0] // seq_len
    xc = _dwconv(l_ref[:, 0:LRU_W], cw_ref[...], cb_ref[...], LRU_CONV_LEFT, _RowShifter(seq_len))
    xcb = xc.astype(BF16)
    half_xc = 0.5 * xc
    half_rate = []
    for d in range(2):
        z = -lam_ref[d:d + 1, :]
        half_rate.append((0.5 * LRU_C) * (jnp.maximum(z, 0.0) + jnp.log1p(jnp.exp(-jnp.abs(z)))))
    half_gb = 0.5 * gb_ref[...]
    for hd in range(LRU_HEADS):
        cols = slice(hd * LRU_BLK, (hd + 1) * LRU_BLK)
        pre = _dot(xcb[:, cols], gw_ref[hd].astype(BF16))
        for d, (a_scr, b_scr) in enumerate(((af_scr, bf_scr), (ab_scr, bb_scr))):
            base = d * 2 * LRU_BLK
            tr = jnp.tanh(pre[:, base:base + LRU_BLK] + half_gb[2 * d:2 * d + 1, cols])
            ti = jnp.tanh(pre[:, base + LRU_BLK:base + 2 * LRU_BLK] + half_gb[2 * d + 1:2 * d + 2, cols])
            hr = half_rate[d][:, cols]
            neg_log_a = tr * hr + hr
            a = jnp.exp(-neg_log_a)
            a_scr[:, cols] = a
            var = jnp.tanh(neg_log_a) * (1.0 + a * a)
            b_scr[:, cols] = jnp.where(var > 0.0, var * lax.rsqrt(var), 0.0) * ((ti + 1.0) * half_xc[:, cols])

    n_tiles = seq_len // LRU_SCAN_ROWS

    def tile_step(k, carry):
        out = list(carry)
        base_f = pl.multiple_of(k * LRU_SCAN_ROWS, LRU_SCAN_ROWS)
        base_b = pl.multiple_of((n_tiles - 1 - k) * LRU_SCAN_ROWS, LRU_SCAN_ROWS)
        for g in range(group):
            rows_f = pl.ds(g * seq_len + base_f, LRU_SCAN_ROWS)
            rows_b = pl.ds(g * seq_len + base_b, LRU_SCAN_ROWS)
            af, bf, yf = af_scr.at[rows_f, :], bf_scr.at[rows_f, :], yf_scr.at[rows_f, :]
            ab, bb, yb = ab_scr.at[rows_b, :], bb_scr.at[rows_b, :], yb_scr.at[rows_b, :]
            hf, hb = out[2 * g], out[2 * g + 1]
            for r in range(LRU_SCAN_ROWS):
                rb = LRU_SCAN_ROWS - 1 - r
                hf = af[r:r + 1, :] * hf + bf[r:r + 1, :]
                hb = ab[rb:rb + 1, :] * hb + bb[rb:rb + 1, :]
                yf[r:r + 1, :] = hf
                yb[rb:rb + 1, :] = hb
            out[2 * g], out[2 * g + 1] = hf, hb
        return tuple(out)

    init = []
    for g in range(group):
        init += [h0f_ref[g], h0b_ref[g]]
    final = lax.fori_loop(0, n_tiles, tile_step, tuple(init))
    for g in range(group):
        hf_ref[g] = final[2 * g]
        hb_ref[g] = final[2 * g + 1]
    lg = l_ref[:, LRU_W:]
    gelu = 0.5 * lg * (1.0 + jnp.tanh(math.sqrt(2.0 / math.pi) * (lg + 0.044715 * (lg * lg * lg))))
    o_ref[...] = (gelu * (yf_scr[...] + yb_scr[...])).astype(o_ref.dtype)


def _lru(l_all, cw, cb, gw, gb, lam, h0f, h0b, seq_len, n_seq, row0, group):
    rows = seq_len * group
    blk0 = row0 // rows
    vec = lambda b: (0, 0)
    return pl.pallas_call(
        functools.partial(_lru_kernel, seq_len=seq_len),
        grid=(n_seq // group,),
        in_specs=[
            pl.BlockSpec((rows, 2 * LRU_W), lambda b: (blk0 + b, 0)),
            pl.BlockSpec((LRU_CONV, LRU_W), vec),
            pl.BlockSpec((1, LRU_W), vec),
            pl.BlockSpec((LRU_HEADS, LRU_BLK, 4 * LRU_BLK), lambda b: (0, 0, 0)),
            pl.BlockSpec((4, LRU_W), vec),
            pl.BlockSpec((2, LRU_W), vec),
            pl.BlockSpec((group, 1, LRU_W), lambda b: (b, 0, 0)),
            pl.BlockSpec((group, 1, LRU_W), lambda b: (b, 0, 0)),
        ],
        out_specs=[
            pl.BlockSpec((rows, LRU_W), lambda b: (b, 0)),
            pl.BlockSpec((group, 1, LRU_W), lambda b: (b, 0, 0)),
            pl.BlockSpec((group, 1, LRU_W), lambda b: (b, 0, 0)),
        ],
        out_shape=[
            jax.ShapeDtypeStruct((n_seq * seq_len, LRU_W), BF16),
            jax.ShapeDtypeStruct((n_seq, 1, LRU_W), F32),
            jax.ShapeDtypeStruct((n_seq, 1, LRU_W), F32),
        ],
        scratch_shapes=[pltpu.VMEM((rows, LRU_W), F32)] * 6,
        compiler_params=_params(("arbitrary",)),
        name="rg_lru",
    )(l_all, cw, cb, gw, gb, lam, h0f, h0b)


def _hy_filter_kernel(z_ref, t_ref, ckh_ref, ckl_ref, alt_ref, w1_ref, b1_ref, w2_ref, b2_ref, w3_ref, fr_ref,
                      dec_ref, skip_ref, kf_ref):
    z = jnp.sin(fr_ref[0:1, :] * (_dot_f32(z_ref[...], w1_ref[...]) + b1_ref[...]))
    z = jnp.sin(fr_ref[1:2, :] * (_dot_f32(z, w2_ref[...]) + b2_ref[...]))
    zh, zl = _split_bf16(z)
    filt = _dot3(zh, zl, w3_ref[...]) * jnp.exp(-t_ref[...] * jnp.abs(dec_ref[...]))
    filt = filt / jnp.sum(jnp.abs(filt), axis=0, keepdims=True)
    n = t_ref.shape[0]
    kf = _dot3(ckh_ref[...], ckl_ref[...], filt)
    kn = jnp.sum(alt_ref[...] * filt, axis=0, keepdims=True)
    for o in range(HY_ORDER):
        cols = slice(o * HY_W, (o + 1) * HY_W)
        skip = skip_ref[o:o + 1, :]
        kf_ref[o, 0:n, :] = kf[:, cols] + skip
        kf_ref[o, n:, :] = kf[:, cols] + skip
        kf_ref[o, n:n + 1, :] = kn[:, cols] + skip


def _hy_filter(consts, w1, b1, w2, b2, w3, freq, decay, skip, seq_len):
    ckh, ckl = _split_bf16(consts["ck"])
    return pl.pallas_call(
        _hy_filter_kernel,
        out_shape=jax.ShapeDtypeStruct((HY_ORDER, 2 * seq_len, HY_W), F32),
        compiler_params=pltpu.CompilerParams(vmem_limit_bytes=VMEM_LIMIT_BYTES),
        name="hyena_filter",
    )(consts["z"], consts["t"], ckh, ckl, consts["alt_w"], w1, b1, w2, b2, w3, freq, decay, skip)


def _longconv(u, fwd_ref, inv_ref, kf2):
    spec = _dot(fwd_ref[...], u.astype(BF16)) * kf2
    return _dot(inv_ref[...], spec.astype(BF16))


def _hyena_kernel(x1_ref, x2_ref, v_ref, cw_ref, cb_ref, kf_ref, fwd_ref, inv_ref, o_ref, *, seq_len, chain_w):
    shift = _RowShifter(seq_len)
    v = _dwconv(v_ref[...], cw_ref[2], cb_ref[2:3, :], HY_CONV_LEFT, shift)
    gates = [_dwconv(g_ref[...], cw_ref[o], cb_ref[o:o + 1, :], HY_CONV_LEFT, shift)
             for o, g_ref in enumerate((x1_ref, x2_ref))]
    for s in range(v.shape[0] // seq_len):
        rows = slice(s * seq_len, (s + 1) * seq_len)
        for c0 in range(0, v.shape[1], chain_w):
            cols = slice(c0, c0 + chain_w)
            z = v[rows, cols]
            for o in range(HY_ORDER):
                z = gates[o][rows, cols] * _longconv(z, fwd_ref, inv_ref, kf_ref[o, :, cols])
            o_ref[rows, cols] = z.astype(o_ref.dtype)


def _hyena(hy_all, cw, cb, kf, consts, seq_len, n_seq, row0, group, block_w, chain_w):
    rows = seq_len * group
    blk0 = row0 // rows
    nc = HY_W // block_w
    const = dict(pipeline_mode=pl.Buffered(1))
    in_specs = [pl.BlockSpec((rows, block_w), lambda b, c, g=g: (blk0 + b, g * nc + c)) for g in range(3)]
    in_specs += [
        pl.BlockSpec((HY_ORDER + 1, HY_CONV, block_w), lambda b, c: (0, 0, c)),
        pl.BlockSpec((HY_ORDER + 1, block_w), lambda b, c: (0, c)),
        pl.BlockSpec((HY_ORDER, 2 * seq_len, block_w), lambda b, c: (0, 0, c)),
        pl.BlockSpec((2 * seq_len, seq_len), lambda b, c: (0, 0), **const),
        pl.BlockSpec((seq_len, 2 * seq_len), lambda b, c: (0, 0), **const),
    ]
    return pl.pallas_call(
        functools.partial(_hyena_kernel, seq_len=seq_len, chain_w=chain_w),
        grid=(n_seq // group, nc),
        in_specs=in_specs,
        out_specs=pl.BlockSpec((rows, block_w), lambda b, c: (b, c)),
        out_shape=jax.ShapeDtypeStruct((n_seq * seq_len, HY_W), BF16),
        compiler_params=_params(("arbitrary", "arbitrary")),
        name="hyena",
    )(hy_all, hy_all, hy_all, cw.reshape(HY_CONV, HY_ORDER + 1, HY_W).transpose(1, 0, 2),
      cb.reshape(HY_ORDER + 1, HY_W), kf, consts["fwd"].astype(BF16), consts["inv"].astype(BF16))


def _pool_kernel(p_ref, w_ref, s_ref, o_ref, *, seq_len):
    shift = _RowShifter(seq_len)
    t = lax.broadcasted_iota(jnp.int32, (p_ref.shape[0], 1), 0) & (seq_len - 1)
    for gi, win in enumerate(POOL_WINDOWS):
        cols = slice(gi * POOL_GW, (gi + 1) * POOL_GW)
        x = p_ref[:, cols]
        half = win // 2
        back, fwd = x, x
        m = 1
        while m < half:
            back = back + shift(back, -m)
            fwd = fwd + shift(fwd, m)
            m *= 2
        s = shift(back, -1) + fwd
        cnt = (jnp.minimum(t + half, seq_len) - jnp.maximum(t - half, 0)).astype(F32)
        y = _dot((s / cnt - x).astype(BF16), w_ref[gi].astype(BF16))
        o_ref[:, cols] = (y * s_ref[:, cols]).astype(o_ref.dtype)


def _pool(p_all, w, scale, seq_len, n_seq, row0, group):
    rows = seq_len * group
    blk0 = row0 // rows
    return pl.pallas_call(
        functools.partial(_pool_kernel, seq_len=seq_len),
        grid=(n_seq // group,),
        in_specs=[
            pl.BlockSpec((rows, POOL_W), lambda b: (blk0 + b, 0)),
            pl.BlockSpec((len(POOL_WINDOWS), POOL_GW, POOL_GW), lambda b: (0, 0, 0)),
            pl.BlockSpec((1, POOL_W), lambda b: (0, 0)),
        ],
        out_specs=pl.BlockSpec((rows, POOL_W), lambda b: (b, 0)),
        out_shape=jax.ShapeDtypeStruct((n_seq * seq_len, POOL_W), BF16),
        compiler_params=_params(("arbitrary",)),
        name="multi_pool",
    )(p_all, w, scale)


def _rope_tables():
    rows = DEC_SEQ // GRID_W
    r = np.repeat(np.arange(rows), GRID_W).astype(np.float64)
    col = np.tile(np.arange(GRID_W), rows).astype(np.float64)
    half = HEAD_DIM // 2
    inv = ROPE_THETA ** (-np.arange(0, half, 2, dtype=np.float64) / half)
    ang = np.concatenate([r[:, None] * inv, col[:, None] * inv], axis=-1)
    cos = np.repeat(np.cos(ang), 2, axis=-1)
    sin = np.repeat(np.sin(ang), 2, axis=-1) * np.tile(np.array([-1.0, 1.0]), half)
    return jnp.asarray(cos, F32), jnp.asarray(sin, F32)


def _hyena_tables(n):
    idx = np.arange(n, dtype=np.float64)
    t = idx / max(n - 1, 1)
    bands = np.linspace(1e-4, HY_BANDS - 1, HY_BANDS)
    f = 2.0 * math.pi * idx[:, None] * bands[None, :] / n
    z = np.zeros((n, HY_EMB_PAD))
    z[:, 0] = t
    z[:, 1:1 + HY_BANDS] = np.cos(f)
    z[:, 1 + HY_BANDS:HY_EMB] = -np.sin(f)
    ang = math.pi * ((idx[:, None] * idx[None, :]) % (2 * n)) / n
    wgt_n = np.where(idx == 0, 1.0, 2.0)
    alt = np.where(idx % 2 == 0, 1.0, -1.0)
    fwd = np.concatenate([np.cos(ang), np.sin(ang)], axis=0)
    inv = np.concatenate([np.cos(ang) * wgt_n[None, :], np.sin(ang) * wgt_n[None, :]], axis=1) / (2 * n)
    fwd[n, :] = alt
    inv[:, n] = alt / (2 * n)
    return {
        "z": jnp.asarray(z, F32),
        "t": jnp.asarray(t[:, None], F32),
        "ck": jnp.asarray(np.cos(ang) * wgt_n[None, :], F32),
        "alt_w": jnp.asarray((alt * wgt_n)[:, None], F32),
        "fwd": jnp.asarray(fwd, F32),
        "inv": jnp.asarray(inv, F32),
    }


def kernel(x_prompt, x_sample, cache_k, cache_v, state_lru_fwd, state_lru_bwd, c, c_ctx, norm_g, w_mod, b_mod, ffn_w13, ffn_w2, ab_w_in, ab_q_norm, ab_k_norm, lru_conv_w, lru_conv_b, lru_gate_w, lru_gate_b, lru_lambda, ab_w_out, cd_w_in, hy_conv_w, hy_conv_b, hy_w1, hy_b1, hy_w2, hy_b2, hy_w3, hy_freq, hy_decay, hy_skip, pool_w, pool_scale, cd_w_out):
    xs = [x_prompt.reshape(N_CTX, D_MODEL), x_sample.reshape(N_LAT, D_MODEL)]
    cond = jnp.concatenate([c_ctx[None], c, jnp.zeros((COND_PAD - N_COND, D_MODEL), F32)], axis=0)
    mods = _modulation(cond, w_mod, b_mod)[:, :N_COND].reshape(DEPTH, N_COND, N_MOD, D_MODEL)
    rope_cos, rope_sin = _rope_tables()

    k_list, v_list, hf_list, hb_list = [], [], [], []
    for l in range(DEPTH):
        mod = mods[l]
        g = norm_g[l]
        (x,) = _ffn(xs, mod, g[0:1], ffn_w13, ffn_w2, l, 0, 0, [N_TOK])
        if l % 2 == 0:
            e = l // 2
            qkv, lxg = _inproj(x, mod, g[1:2], ab_w_in[e], (QKV_W, 2 * LRU_W))
            qn, kn = ab_q_norm[e][None], ab_k_norm[e][None]
            attn_c, kc, vc = _attn_ctx(qkv, qn, kn)
            attn_l = _attn_lat(qkv, cache_k[:, e].reshape(DEC_BATCH, PAST_LEN, KV_W),
                               cache_v[:, e].reshape(DEC_BATCH, PAST_LEN, KV_W), qn, kn, rope_cos, rope_sin)
            gw = 0.5 * jnp.transpose(lru_gate_w[e], (2, 3, 0, 1, 4)).reshape(LRU_HEADS, LRU_BLK, 4 * LRU_BLK)
            gb = lru_gate_b[e].reshape(4, LRU_W)
            lru_args = (lru_conv_w[e], lru_conv_b[e][None], gw, gb, lru_lambda[e])
            zeros = jnp.zeros((BATCH, 1, LRU_W), F32)
            rec_c, hf, hb = _lru(lxg, *lru_args, zeros, zeros, SEQ, BATCH, 0, LRU_CTX_GROUP)
            rec_l, _, _ = _lru(lxg, *lru_args, state_lru_fwd[:, e][:, None], state_lru_bwd[:, e][:, None],
                               DEC_SEQ, DEC_BATCH, N_CTX, 1)
            x = _outproj(x, mod, attn_c, attn_l, rec_c, rec_l, ab_w_out[e])
            k_list.append(kc.reshape(BATCH, SEQ, N_KV_HEADS, HEAD_DIM))
            v_list.append(vc.reshape(BATCH, SEQ, N_KV_HEADS, HEAD_DIM))
            hf_list.append(hf.reshape(BATCH, LRU_W))
            hb_list.append(hb.reshape(BATCH, LRU_W))
        else:
            o = l // 2
            hy, pw = _inproj(x, mod, g[1:2], cd_w_in[o], ((HY_ORDER + 1) * HY_W, POOL_W))
            w1 = jnp.zeros((HY_EMB_PAD, HY_FH), F32).at[:HY_EMB].set(hy_w1[o])
            z_out, p_out = [], []
            for seq_len, n_seq, row0, hy_cfg, pool_group in (
                    (SEQ, BATCH, 0, (HY_CTX_GROUP, HY_W, HY_W), POOL_CTX_GROUP),
                    (DEC_SEQ, DEC_BATCH, N_CTX, (1, HY_LAT_BLOCK_W, HY_CHAIN_W), 1)):
                consts = _hyena_tables(seq_len)
                kf = _hy_filter(consts, w1, hy_b1[o][None], hy_w2[o], hy_b2[o][None], hy_w3[o], hy_freq[o],
                                hy_decay[o][None], hy_skip[o], seq_len)
                z_out.append(_hyena(hy, hy_conv_w[o], hy_conv_b[o], kf, consts, seq_len, n_seq, row0, *hy_cfg))
                p_out.append(_pool(pw, pool_w[o], pool_scale[o][None], seq_len, n_seq, row0, pool_group))
            x = _outproj(x, mod, z_out[0], z_out[1], p_out[0], p_out[1], cd_w_out[o])
        xs = _ffn([x], mod, g[2:3], ffn_w13, ffn_w2, l, 1, 2, [N_TOK] if l + 1 < DEPTH else [N_CTX, N_LAT])

    y_prompt = xs[0].reshape(BATCH, SEQ, D_MODEL)
    y_sample = xs[1].reshape(DEC_BATCH, DEC_SEQ, D_MODEL)
    return (y_prompt, y_sample, jnp.stack(k_list, axis=1), jnp.stack(v_list, axis=1),
            jnp.stack(hf_list, axis=1), jnp.stack(hb_list, axis=1))
```

```python
import functools
import math

import numpy as np
import jax
import jax.numpy as jnp
from jax import lax
from jax.experimental import pallas as pl
from jax.experimental.pallas import tpu as pltpu

F32 = jnp.float32
BF16 = jnp.bfloat16

D_MODEL = 2048
BATCH = 32
SEQ = 256
DEPTH = 2
DEC_BATCH = 2
DEC_SEQ = 1024
PAST_LEN = 512
GRID_W = 64
N_MOD = 9
NORM_EPS = 1e-6
D_FF = 5632
HEAD_DIM = 128
N_Q_HEADS = 8
N_KV_HEADS = 2
Q_PER_KV = N_Q_HEADS // N_KV_HEADS
ATTN_W = N_Q_HEADS * HEAD_DIM
KV_W = N_KV_HEADS * HEAD_DIM
QKV_W = ATTN_W + 2 * KV_W
ROPE_THETA = 10000.0
LRU_W = 1024
LRU_HEADS = 8
LRU_BLK = LRU_W // LRU_HEADS
LRU_CONV = 4
LRU_CONV_LEFT = 2
LRU_C = 8.0
HY_W = 1024
HY_ORDER = 2
HY_CONV = 3
HY_CONV_LEFT = 1
HY_EMB = 33
HY_EMB_PAD = 128
HY_BANDS = (HY_EMB - 1) // 2
HY_FH = 64
POOL_W = 1024
POOL_WINDOWS = (2, 4, 8, 16)
POOL_GW = POOL_W // len(POOL_WINDOWS)

N_CTX = BATCH * SEQ
N_LAT = DEC_BATCH * DEC_SEQ
N_TOK = N_CTX + N_LAT
N_COND = 1 + DEC_BATCH
COND_PAD = 8

VMEM_LIMIT_BYTES = 56 * 1024 * 1024

ROW_TILE = 512
FFN_ROW_TILE = 1024
FF_TILE = 512
FFN_NORM_ROWS = 32
MOD_TILE = 1024
Q_TILE = 256
ATTN_CTX_GROUP = 8
ATTN_CTX_HEADS_PER_CHAIN = 2
ATTN_LAT_HEADS_PER_CHAIN = 1
LRU_CTX_GROUP = 2
LRU_SCAN_ROWS = 8
HY_CTX_GROUP = 4
POOL_CTX_GROUP = 4
HY_CHAIN_W = 512
HY_LAT_BLOCK_W = 512


def _params(sem):
    return pltpu.CompilerParams(dimension_semantics=sem, vmem_limit_bytes=VMEM_LIMIT_BYTES)


def _cond_index(i, tile):
    return jnp.maximum((i * tile) // DEC_SEQ - (N_CTX // DEC_SEQ - 1), 0)


def _sigmoid(x):
    return 1.0 / (1.0 + jnp.exp(-x))


def _dot(a, b):
    return jnp.dot(a, b, preferred_element_type=F32)


def _dot_f32(a, b):
    return jnp.dot(a, b, preferred_element_type=F32, precision=lax.Precision.HIGHEST)


def _split_bf16(x):
    hi = x.astype(BF16)
    return hi, (x - hi.astype(F32)).astype(BF16)


def _dot3(a_hi, a_lo, b):
    b_hi, b_lo = _split_bf16(b)
    return _dot(a_hi, b_hi) + (_dot(a_hi, b_lo) + _dot(a_lo, b_hi))


def _modulated_norm(x, g, mod_ref, j):
    ms = jnp.mean(x * x, axis=-1, keepdims=True)
    y = x * lax.rsqrt(ms + NORM_EPS) * g
    return y * (1.0 + mod_ref[0, 3 * j + 1:3 * j + 2, :]) + mod_ref[0, 3 * j:3 * j + 1, :]


def _head_norm(x, g):
    ms = jnp.mean(x * x, axis=-1, keepdims=True)
    return x * lax.rsqrt(ms + NORM_EPS) * g


class _RowShifter:
    def __init__(self, period):
        self.period = period
        self._masks = {}

    def __call__(self, x, off):
        if off == 0:
            return x
        key = (x.shape, off)
        if key not in self._masks:
            t = lax.broadcasted_iota(jnp.int32, x.shape, 0) & (self.period - 1)
            self._masks[key] = t >= -off if off < 0 else t < self.period - off
        return jnp.where(self._masks[key], pltpu.roll(x, (-off) % x.shape[0], axis=0), 0.0)


def _dwconv(x, w, b, left, shift):
    acc = b + shift(x, -left) * w[0:1, :]
    for j in range(1, w.shape[0]):
        acc = acc + shift(x, j - left) * w[j:j + 1, :]
    return acc


def _mod_kernel(c_ref, w_ref, b_ref, o_ref):
    c = c_ref[...]
    s = (c * _sigmoid(c)).astype(BF16)
    o_ref[0] = _dot(s, w_ref[0].astype(BF16)) + b_ref[0]


def _modulation(cond, w_mod, b_mod):
    n = N_MOD * D_MODEL
    return pl.pallas_call(
        _mod_kernel,
        grid=(DEPTH, n // MOD_TILE),
        in_specs=[
            pl.BlockSpec((COND_PAD, D_MODEL), lambda l, j: (0, 0)),
            pl.BlockSpec((1, D_MODEL, MOD_TILE), lambda l, j: (l, 0, j)),
            pl.BlockSpec((1, 1, MOD_TILE), lambda l, j: (l, 0, j)),
        ],
        out_specs=pl.BlockSpec((1, COND_PAD, MOD_TILE), lambda l, j: (l, 0, j)),
        out_shape=jax.ShapeDtypeStruct((DEPTH, COND_PAD, n), F32),
        compiler_params=_params(("arbitrary", "arbitrary")),
        name="modulation",
    )(cond, w_mod, b_mod.reshape(DEPTH, 1, n))


def _ffn_kernel(*refs, j, n_in, n_out, in_split, out_split):
    x_hbm = refs[:n_in]
    mod_ref, g_ref, w1_ref, w3_ref, w2_ref = refs[n_in:n_in + 5]
    o_hbm = refs[n_in + 5:n_in + 5 + n_out]
    buf, h_ref, sem_in, sem_out = refs[n_in + 5 + n_out:]
    i, f = pl.program_id(0), pl.program_id(1)
    n, nf = pl.num_programs(0), pl.num_programs(1)
    slot = lax.rem(i, 2)
    acc = buf.at[slot]

    def tile_rows(t):
        return pl.ds(pl.multiple_of(t * FFN_ROW_TILE, FFN_ROW_TILE), FFN_ROW_TILE)

    def in_copy(src, t, s):
        return pltpu.make_async_copy(src.at[tile_rows(t), :], buf.at[s], sem_in.at[s])

    def out_copy(dst, t, s):
        return pltpu.make_async_copy(buf.at[s], dst.at[tile_rows(t), :], sem_out.at[s])

    def start_split(copy, refs_, split, t, s):
        if len(refs_) == 1:
            copy(refs_[0], t, s).start()
        else:
            @pl.when(t < split)
            def _():
                copy(refs_[0], t, s).start()

            @pl.when(t >= split)
            def _():
                copy(refs_[1], t - split, s).start()

    @pl.when(f == 0)
    def _():
        @pl.when(i == 0)
        def _():
            start_split(in_copy, x_hbm, in_split, i, slot)

        in_copy(x_hbm[0], 0, slot).wait()
        gain = g_ref[...] * (1.0 + mod_ref[0, 3 * j + 1:3 * j + 2, :])
        shift = mod_ref[0, 3 * j:3 * j + 1, :]

        def chunk(r, carry):
            rows = pl.ds(pl.multiple_of(r * FFN_NORM_ROWS, FFN_NORM_ROWS), FFN_NORM_ROWS)
            x = acc[rows, :]
            ms = jnp.mean(x * x, axis=-1, keepdims=True)
            h_ref[rows, :] = (x * lax.rsqrt(ms + NORM_EPS) * gain + shift).astype(BF16)
            return carry

        lax.fori_loop(0, FFN_ROW_TILE // FFN_NORM_ROWS, chunk, 0, unroll=8)

    @pl.when((f == 1) & (i + 1 < n))
    def _():
        @pl.when(i >= 1)
        def _():
            out_copy(o_hbm[0], 0, 1 - slot).wait()

        start_split(in_copy, x_hbm, in_split, i + 1, 1 - slot)

    h = h_ref[...]
    gt = _dot(h, w1_ref[...].astype(BF16))
    up = _dot(h, w3_ref[...].astype(BF16))
    a = (gt * _sigmoid(gt) * up).astype(BF16)
    half_gate = 0.5 * mod_ref[0, 3 * j + 2:3 * j + 3, :]
    acc[...] += half_gate * _dot(a, w2_ref[...].astype(BF16))

    @pl.when(f == nf - 1)
    def _():
        start_split(out_copy, o_hbm, out_split, i, slot)

        @pl.when(i == n - 1)
        def _():
            out_copy(o_hbm[0], 0, slot).wait()
            out_copy(o_hbm[0], 0, 1 - slot).wait()


def _ffn(xs, mod, g, w13, w2, l, sub, j, out_rows):
    nf = D_FF // FF_TILE
    n_tiles = N_TOK // FFN_ROW_TILE
    assert nf >= 2 and n_tiles >= 2 and sum(x.shape[0] for x in xs) == N_TOK and sum(out_rows) == N_TOK
    assert all(x.shape[0] % FFN_ROW_TILE == 0 for x in xs) and all(r % FFN_ROW_TILE == 0 for r in out_rows)
    any_spec = pl.BlockSpec(memory_space=pl.ANY)
    outs = pl.pallas_call(
        functools.partial(_ffn_kernel, j=j, n_in=len(xs), n_out=len(out_rows),
                          in_split=xs[0].shape[0] // FFN_ROW_TILE, out_split=out_rows[0] // FFN_ROW_TILE),
        grid=(n_tiles, nf),
        in_specs=[any_spec] * len(xs) + [
            pl.BlockSpec((1, N_MOD, D_MODEL), lambda i, f: (_cond_index(i, FFN_ROW_TILE), 0, 0)),
            pl.BlockSpec((1, D_MODEL), lambda i, f: (0, 0)),
            pl.BlockSpec((None, None, D_MODEL, FF_TILE), lambda i, f: (l, sub, 0, f)),
            pl.BlockSpec((None, None, D_MODEL, FF_TILE), lambda i, f: (l, sub, 0, nf + f)),
            pl.BlockSpec((None, None, FF_TILE, D_MODEL), lambda i, f: (l, sub, f, 0)),
        ],
        out_specs=[any_spec] * len(out_rows),
        out_shape=[jax.ShapeDtypeStruct((r, D_MODEL), F32) for r in out_rows],
        scratch_shapes=[
            pltpu.VMEM((2, FFN_ROW_TILE, D_MODEL), F32),
            pltpu.VMEM((FFN_ROW_TILE, D_MODEL), BF16),
            pltpu.SemaphoreType.DMA((2,)),
            pltpu.SemaphoreType.DMA((2,)),
        ],
        compiler_params=_params(("arbitrary", "arbitrary")),
        name="ffn",
    )(*xs, mod, g, w13, w13, w2)
    return outs


W_CHUNK = 512


def _for_each_weight_chunk(w_hbm, w_scr, stage, sem, prepare, body):
    n_chunks = w_hbm.shape[1] // W_CHUNK
    chunks = [slice(c * W_CHUNK, (c + 1) * W_CHUNK) for c in range(n_chunks)]
    first = pl.program_id(0) == 0

    def copy(c, s):
        return pltpu.make_async_copy(w_hbm.at[:, pl.ds(c * W_CHUNK, W_CHUNK)], stage.at[s], sem.at[s])

    @pl.when(first)
    def _():
        copy(0, 0).start()
        lhs = prepare()
        for c, cols in enumerate(chunks):
            s = c % 2
            if c + 1 < n_chunks:
                copy(c + 1, 1 - s).start()
            copy(c, s).wait()
            w_scr[:, cols] = stage[s].astype(BF16)
            body(lhs, cols)

    @pl.when(jnp.logical_not(first))
    def _():
        lhs = prepare()
        for cols in chunks:
            body(lhs, cols)


def _weight_scratch(k, n):
    assert n % W_CHUNK == 0
    return [pltpu.VMEM((k, n), BF16), pltpu.VMEM((2, k, W_CHUNK), F32), pltpu.SemaphoreType.DMA((2,))]


def _inproj_kernel(x_ref, mod_ref, g_ref, w_hbm, *rest, widths):
    o_refs, (w_ref, stage, sem) = rest[:len(widths)], rest[len(widths):]
    starts = [sum(widths[:k]) for k in range(len(widths))]

    def normalise():
        return _modulated_norm(x_ref[...], g_ref[...], mod_ref, 1).astype(BF16)

    def project(h, cols):
        k = max(k for k, s0 in enumerate(starts) if s0 <= cols.start)
        o_refs[k][:, cols.start - starts[k]:cols.stop - starts[k]] = _dot(h, w_ref[:, cols])

    _for_each_weight_chunk(w_hbm, w_ref, stage, sem, normalise, project)


def _inproj(x, mod, g, w, widths):
    n = sum(widths)
    return pl.pallas_call(
        functools.partial(_inproj_kernel, widths=widths),
        grid=(N_TOK // ROW_TILE,),
        in_specs=[
            pl.BlockSpec((ROW_TILE, D_MODEL), lambda i: (i, 0)),
            pl.BlockSpec((1, N_MOD, D_MODEL), lambda i: (_cond_index(i, ROW_TILE), 0, 0)),
            pl.BlockSpec((1, D_MODEL), lambda i: (0, 0)),
            pl.BlockSpec(memory_space=pl.ANY),
        ],
        out_specs=[pl.BlockSpec((ROW_TILE, wd), lambda i: (i, 0)) for wd in widths],
        out_shape=[jax.ShapeDtypeStruct((N_TOK, wd), F32) for wd in widths],
        scratch_shapes=_weight_scratch(D_MODEL, n),
        compiler_params=_params(("arbitrary",)),
        name="mixer_in_proj",
    )(x, mod, g, w)


def _outproj_kernel(x_ref, mod_ref, ac_ref, al_ref, bc_ref, bl_ref, w_hbm, o_ref, w_ref, stage, sem):
    wa = w_ref.shape[0] // 2
    is_ctx = pl.program_id(0) < N_CTX // ROW_TILE

    def pick():
        return jnp.where(is_ctx, ac_ref[...], al_ref[...]), jnp.where(is_ctx, bc_ref[...], bl_ref[...])

    def project(ab, cols):
        y = _dot(ab[0], w_ref[0:wa, cols]) + _dot(ab[1], w_ref[wa:, cols])
        o_ref[:, cols] = x_ref[:, cols] + mod_ref[0, 5:6, cols] * y

    _for_each_weight_chunk(w_hbm, w_ref, stage, sem, pick, project)


def _outproj(x, mod, a_ctx, a_lat, b_ctx, b_lat, w):
    wa = a_ctx.shape[1]
    nc = N_CTX // ROW_TILE
    ctx_spec = pl.BlockSpec((ROW_TILE, wa), lambda i: (jnp.minimum(i, nc - 1), 0))
    lat_spec = pl.BlockSpec((ROW_TILE, wa), lambda i: (jnp.maximum(i - nc, 0), 0))
    return pl.pallas_call(
        _outproj_kernel,
        grid=(N_TOK // ROW_TILE,),
        in_specs=[
            pl.BlockSpec((ROW_TILE, D_MODEL), lambda i: (i, 0)),
            pl.BlockSpec((1, N_MOD, D_MODEL), lambda i: (_cond_index(i, ROW_TILE), 0, 0)),
            ctx_spec, lat_spec, ctx_spec, lat_spec,
            pl.BlockSpec(memory_space=pl.ANY),
        ],
        out_specs=pl.BlockSpec((ROW_TILE, D_MODEL), lambda i: (i, 0)),
        out_shape=jax.ShapeDtypeStruct((N_TOK, D_MODEL), F32),
        scratch_shapes=_weight_scratch(2 * wa, D_MODEL),
        compiler_params=_params(("arbitrary",)),
        name="mixer_out_proj",
    )(x, mod, a_ctx, a_lat, b_ctx, b_lat, w)


def _rope(x, cos, sin):
    lane = lax.broadcasted_iota(jnp.int32, x.shape, 1)
    swapped = jnp.where((lane & 1) == 0, pltpu.roll(x, HEAD_DIM - 1, axis=1), pltpu.roll(x, 1, axis=1))
    return x * cos + swapped * sin


def _attend(q_heads, k, v_ones, o_ref, rows, col0, heads_per_chain):
    t = q_heads[0].shape[0]
    for i0 in range(0, len(q_heads), heads_per_chain):
        heads = q_heads[i0:i0 + heads_per_chain]
        q = jnp.concatenate(heads, axis=0)
        s = lax.dot_general(q, k, (((1,), (1,)), ((), ())), preferred_element_type=F32) * (HEAD_DIM ** -0.5)
        e = jnp.exp(s - jnp.max(s, axis=-1, keepdims=True)).astype(BF16)
        oe = _dot(e, v_ones)
        o = oe[:, 0:HEAD_DIM] / oe[:, HEAD_DIM:]
        for i in range(len(heads)):
            c0 = col0 + (i0 + i) * HEAD_DIM
            o_ref[rows, c0:c0 + HEAD_DIM] = o[i * t:(i + 1) * t].astype(o_ref.dtype)


def _attn_ctx_kernel(qkv_ref, qn_ref, kn_ref, o_ref, kc_ref, vc_ref):
    ones = jnp.ones((SEQ, HEAD_DIM), BF16)
    for s in range(ATTN_CTX_GROUP):
        rows = slice(s * SEQ, (s + 1) * SEQ)
        for kv in range(N_KV_HEADS):
            kcol = ATTN_W + kv * HEAD_DIM
            vcol = ATTN_W + KV_W + kv * HEAD_DIM
            k = _head_norm(qkv_ref[rows, kcol:kcol + HEAD_DIM], kn_ref[...])
            v = qkv_ref[rows, vcol:vcol + HEAD_DIM]
            cache_rows = pl.ds(s * SEQ * N_KV_HEADS + kv, SEQ, stride=N_KV_HEADS)
            kc_ref[cache_rows, :] = k
            vc_ref[cache_rows, :] = v
            qs = []
            for i in range(Q_PER_KV):
                qcol = (kv * Q_PER_KV + i) * HEAD_DIM
                qs.append(_head_norm(qkv_ref[rows, qcol:qcol + HEAD_DIM], qn_ref[...]).astype(BF16))
            _attend(qs, k.astype(BF16), jnp.concatenate([v.astype(BF16), ones], axis=1), o_ref, rows,
                    kv * Q_PER_KV * HEAD_DIM, ATTN_CTX_HEADS_PER_CHAIN)


def _attn_ctx(qkv, qn, kn):
    rows = ATTN_CTX_GROUP * SEQ
    return pl.pallas_call(
        _attn_ctx_kernel,
        grid=(BATCH // ATTN_CTX_GROUP,),
        in_specs=[
            pl.BlockSpec((rows, QKV_W), lambda b: (b, 0)),
            pl.BlockSpec((1, HEAD_DIM), lambda b: (0, 0)),
            pl.BlockSpec((1, HEAD_DIM), lambda b: (0, 0)),
        ],
        out_specs=[
            pl.BlockSpec((rows, ATTN_W), lambda b: (b, 0)),
            pl.BlockSpec((rows * N_KV_HEADS, HEAD_DIM), lambda b: (b, 0)),
            pl.BlockSpec((rows * N_KV_HEADS, HEAD_DIM), lambda b: (b, 0)),
        ],
        out_shape=[
            jax.ShapeDtypeStruct((N_CTX, ATTN_W), BF16),
            jax.ShapeDtypeStruct((N_CTX * N_KV_HEADS, HEAD_DIM), F32),
            jax.ShapeDtypeStruct((N_CTX * N_KV_HEADS, HEAD_DIM), F32),
        ],
        compiler_params=_params(("arbitrary",)),
        name="attention_context",
    )(qkv, qn, kn)


def _attn_lat_kernel(qkv_ref, ck_ref, cv_ref, qn_ref, kn_ref, cos_ref, sin_ref, o_ref, k_scr, v_scr):
    qi = pl.program_id(1)

    @pl.when(qi == 0)
    def _():
        k_scr[0:PAST_LEN, :] = ck_ref[0].astype(BF16)
        for kv in range(N_KV_HEADS):
            kcol = ATTN_W + kv * HEAD_DIM
            vcol = ATTN_W + KV_W + kv * HEAD_DIM
            k = _head_norm(qkv_ref[:, kcol:kcol + HEAD_DIM], kn_ref[...])
            k = _rope(k, cos_ref[...], sin_ref[...])
            k_scr[PAST_LEN:, kv * HEAD_DIM:(kv + 1) * HEAD_DIM] = k.astype(BF16)
            v0 = 2 * kv * HEAD_DIM
            v_scr[0:PAST_LEN, v0:v0 + HEAD_DIM] = cv_ref[0, :, kv * HEAD_DIM:(kv + 1) * HEAD_DIM].astype(BF16)
            v_scr[PAST_LEN:, v0:v0 + HEAD_DIM] = qkv_ref[:, vcol:vcol + HEAD_DIM].astype(BF16)
            v_scr[:, v0 + HEAD_DIM:v0 + 2 * HEAD_DIM] = jnp.ones((PAST_LEN + DEC_SEQ, HEAD_DIM), BF16)

    r0 = pl.multiple_of(qi * Q_TILE, Q_TILE)
    cos = cos_ref[pl.ds(r0, Q_TILE), :]
    sin = sin_ref[pl.ds(r0, Q_TILE), :]
    for kv in range(N_KV_HEADS):
        qs = []
        for i in range(Q_PER_KV):
            qcol = (kv * Q_PER_KV + i) * HEAD_DIM
            q = _head_norm(qkv_ref[pl.ds(r0, Q_TILE), qcol:qcol + HEAD_DIM], qn_ref[...])
            qs.append(_rope(q, cos, sin).astype(BF16))
        _attend(qs, k_scr[:, kv * HEAD_DIM:(kv + 1) * HEAD_DIM], v_scr[:, 2 * kv * HEAD_DIM:2 * (kv + 1) * HEAD_DIM],
                o_ref, slice(None), kv * Q_PER_KV * HEAD_DIM, ATTN_LAT_HEADS_PER_CHAIN)


def _attn_lat(qkv, cache_k, cache_v, qn, kn, cos, sin):
    row0 = N_CTX // DEC_SEQ
    qt = DEC_SEQ // Q_TILE
    return pl.pallas_call(
        _attn_lat_kernel,
        grid=(DEC_BATCH, qt),
        in_specs=[
            pl.BlockSpec((DEC_SEQ, QKV_W), lambda b, q: (row0 + b, 0)),
            pl.BlockSpec((1, PAST_LEN, KV_W), lambda b, q: (b, 0, 0)),
            pl.BlockSpec((1, PAST_LEN, KV_W), lambda b, q: (b, 0, 0)),
            pl.BlockSpec((1, HEAD_DIM), lambda b, q: (0, 0)),
            pl.BlockSpec((1, HEAD_DIM), lambda b, q: (0, 0)),
            pl.BlockSpec((DEC_SEQ, HEAD_DIM), lambda b, q: (0, 0)),
            pl.BlockSpec((DEC_SEQ, HEAD_DIM), lambda b, q: (0, 0)),
        ],
        out_specs=pl.BlockSpec((Q_TILE, ATTN_W), lambda b, q: (b * qt + q, 0)),
        out_shape=jax.ShapeDtypeStruct((N_LAT, ATTN_W), BF16),
        scratch_shapes=[pltpu.VMEM((PAST_LEN + DEC_SEQ, KV_W), BF16),
                        pltpu.VMEM((PAST_LEN + DEC_SEQ, 2 * KV_W), BF16)],
        compiler_params=_params(("arbitrary", "arbitrary")),
        name="attention_latent",
    )(qkv, cache_k, cache_v, qn, kn, cos, sin)


def _lru_kernel(l_ref, cw_ref, cb_ref, gw_ref, gb_ref, lam_ref, h0f_ref, h0b_ref, o_ref, hf_ref, hb_ref,
                af_scr, bf_scr, ab_scr, bb_scr, yf_scr, yb_scr, *, seq_len):
    group = l_ref.shape[0] // seq_len
    xc = _dwconv(l_ref[:, 0:LRU_W], cw_ref[...], cb_ref[...], LRU_CONV_LEFT, _RowShifter(seq_len))
    xcb = xc.astype(BF16)
    half_xc = 0.5 * xc
    half_rate = []
    for d in range(2):
        z = -lam_ref[d:d + 1, :]
        half_rate.append((0.5 * LRU_C) * (jnp.maximum(z, 0.0) + jnp.log1p(jnp.exp(-jnp.abs(z)))))
    half_gb = 0.5 * gb_ref[...]
    for hd in range(LRU_HEADS):
        cols = slice(hd * LRU_BLK, (hd + 1) * LRU_BLK)
        pre = _dot(xcb[:, cols], gw_ref[hd].astype(BF16))
        for d, (a_scr, b_scr) in enumerate(((af_scr, bf_scr), (ab_scr, bb_scr))):
            base = d * 2 * LRU_BLK
            tr = jnp.tanh(pre[:, base:base + LRU_BLK] + half_gb[2 * d:2 * d + 1, cols])
            ti = jnp.tanh(pre[:, base + LRU_BLK:base + 2 * LRU_BLK] + half_gb[2 * d + 1:2 * d + 2, cols])
            hr = half_rate[d][:, cols]
            neg_log_a = tr * hr + hr
            a = jnp.exp(-neg_log_a)
            a_scr[:, cols] = a
            var = jnp.tanh(neg_log_a) * (1.0 + a * a)
            b_scr[:, cols] = jnp.where(var > 0.0, var * lax.rsqrt(var), 0.0) * ((ti + 1.0) * half_xc[:, cols])

    n_tiles = seq_len // LRU_SCAN_ROWS

    def tile_step(k, carry):
        out = list(carry)
        base_f = pl.multiple_of(k * LRU_SCAN_ROWS, LRU_SCAN_ROWS)
        base_b = pl.multiple_of((n_tiles - 1 - k) * LRU_SCAN_ROWS, LRU_SCAN_ROWS)
        for g in range(group):
            rows_f = pl.ds(g * seq_len + base_f, LRU_SCAN_ROWS)
            rows_b = pl.ds(g * seq_len + base_b, LRU_SCAN_ROWS)
            af, bf, yf = af_scr.at[rows_f, :], bf_scr.at[rows_f, :], yf_scr.at[rows_f, :]
            ab, bb, yb = ab_scr.at[rows_b, :], bb_scr.at[rows_b, :], yb_scr.at[rows_b, :]
            hf, hb = out[2 * g], out[2 * g + 1]
            for r in range(LRU_SCAN_ROWS):
                rb = LRU_SCAN_ROWS - 1 - r
                hf = af[r:r + 1, :] * hf + bf[r:r + 1, :]
                hb = ab[rb:rb + 1, :] * hb + bb[rb:rb + 1, :]
                yf[r:r + 1, :] = hf
                yb[rb:rb + 1, :] = hb
            out[2 * g], out[2 * g + 1] = hf, hb
        return tuple(out)

    init = []
    for g in range(group):
        init += [h0f_ref[g], h0b_ref[g]]
    final = lax.fori_loop(0, n_tiles, tile_step, tuple(init))
    for g in range(group):
        hf_ref[g] = final[2 * g]
        hb_ref[g] = final[2 * g + 1]
    lg = l_ref[:, LRU_W:]
    gelu = 0.5 * lg * (1.0 + jnp.tanh(math.sqrt(2.0 / math.pi) * (lg + 0.044715 * (lg * lg * lg))))
    o_ref[...] = (gelu * (yf_scr[...] + yb_scr[...])).astype(o_ref.dtype)


def _lru(l_all, cw, cb, gw, gb, lam, h0f, h0b, seq_len, n_seq, row0, group):
    rows = seq_len * group
    blk0 = row0 // rows
    vec = lambda b: (0, 0)
    return pl.pallas_call(
        functools.partial(_lru_kernel, seq_len=seq_len),
        grid=(n_seq // group,),
        in_specs=[
            pl.BlockSpec((rows, 2 * LRU_W), lambda b: (blk0 + b, 0)),
            pl.BlockSpec((LRU_CONV, LRU_W), vec),
            pl.BlockSpec((1, LRU_W), vec),
            pl.BlockSpec((LRU_HEADS, LRU_BLK, 4 * LRU_BLK), lambda b: (0, 0, 0)),
            pl.BlockSpec((4, LRU_W), vec),
            pl.BlockSpec((2, LRU_W), vec),
            pl.BlockSpec((group, 1, LRU_W), lambda b: (b, 0, 0)),
            pl.BlockSpec((group, 1, LRU_W), lambda b: (b, 0, 0)),
        ],
        out_specs=[
            pl.BlockSpec((rows, LRU_W), lambda b: (b, 0)),
            pl.BlockSpec((group, 1, LRU_W), lambda b: (b, 0, 0)),
            pl.BlockSpec((group, 1, LRU_W), lambda b: (b, 0, 0)),
        ],
        out_shape=[
            jax.ShapeDtypeStruct((n_seq * seq_len, LRU_W), BF16),
            jax.ShapeDtypeStruct((n_seq, 1, LRU_W), F32),
            jax.ShapeDtypeStruct((n_seq, 1, LRU_W), F32),
        ],
        scratch_shapes=[pltpu.VMEM((rows, LRU_W), F32)] * 6,
        compiler_params=_params(("arbitrary",)),
        name="rg_lru",
    )(l_all, cw, cb, gw, gb, lam, h0f, h0b)


def _hy_filter_kernel(z_ref, t_ref, ckh_ref, ckl_ref, alt_ref, w1_ref, b1_ref, w2_ref, b2_ref, w3_ref, fr_ref,
                      dec_ref, skip_ref, kf_ref):
    z = jnp.sin(fr_ref[0:1, :] * (_dot_f32(z_ref[...], w1_ref[...]) + b1_ref[...]))
    z = jnp.sin(fr_ref[1:2, :] * (_dot_f32(z, w2_ref[...]) + b2_ref[...]))
    zh, zl = _split_bf16(z)
    filt = _dot3(zh, zl, w3_ref[...]) * jnp.exp(-t_ref[...] * jnp.abs(dec_ref[...]))
    filt = filt / jnp.sum(jnp.abs(filt), axis=0, keepdims=True)
    n = t_ref.shape[0]
    kf = _dot3(ckh_ref[...], ckl_ref[...], filt)
    kn = jnp.sum(alt_ref[...] * filt, axis=0, keepdims=True)
    for o in range(HY_ORDER):
        cols = slice(o * HY_W, (o + 1) * HY_W)
        skip = skip_ref[o:o + 1, :]
        kf_ref[o, 0:n, :] = kf[:, cols] + skip
        kf_ref[o, n:, :] = kf[:, cols] + skip
        kf_ref[o, n:n + 1, :] = kn[:, cols] + skip


def _hy_filter(consts, w1, b1, w2, b2, w3, freq, decay, skip, seq_len):
    ckh, ckl = _split_bf16(consts["ck"])
    return pl.pallas_call(
        _hy_filter_kernel,
        out_shape=jax.ShapeDtypeStruct((HY_ORDER, 2 * seq_len, HY_W), F32),
        compiler_params=pltpu.CompilerParams(vmem_limit_bytes=VMEM_LIMIT_BYTES),
        name="hyena_filter",
    )(consts["z"], consts["t"], ckh, ckl, consts["alt_w"], w1, b1, w2, b2, w3, freq, decay, skip)


def _longconv(u, fwd_ref, inv_ref, kf2):
    spec = _dot(fwd_ref[...], u.astype(BF16)) * kf2
    return _dot(inv_ref[...], spec.astype(BF16))


def _hyena_kernel(x1_ref, x2_ref, v_ref, cw_ref, cb_ref, kf_ref, fwd_ref, inv_ref, o_ref, *, seq_len, chain_w):
    shift = _RowShifter(seq_len)
    v = _dwconv(v_ref[...], cw_ref[2], cb_ref[2:3, :], HY_CONV_LEFT, shift)
    gates = [_dwconv(g_ref[...], cw_ref[o], cb_ref[o:o + 1, :], HY_CONV_LEFT, shift)
             for o, g_ref in enumerate((x1_ref, x2_ref))]
    for s in range(v.shape[0] // seq_len):
        rows = slice(s * seq_len, (s + 1) * seq_len)
        for c0 in range(0, v.shape[1], chain_w):
            cols = slice(c0, c0 + chain_w)
            z = v[rows, cols]
            for o in range(HY_ORDER):
                z = gates[o][rows, cols] * _longconv(z, fwd_ref, inv_ref, kf_ref[o, :, cols])
            o_ref[rows, cols] = z.astype(o_ref.dtype)


def _hyena(hy_all, cw, cb, kf, consts, seq_len, n_seq, row0, group, block_w, chain_w):
    rows = seq_len * group
    blk0 = row0 // rows
    nc = HY_W // block_w
    const = dict(pipeline_mode=pl.Buffered(1))
    in_specs = [pl.BlockSpec((rows, block_w), lambda b, c, g=g: (blk0 + b, g * nc + c)) for g in range(3)]
    in_specs += [
        pl.BlockSpec((HY_ORDER + 1, HY_CONV, block_w), lambda b, c: (0, 0, c)),
        pl.BlockSpec((HY_ORDER + 1, block_w), lambda b, c: (0, c)),
        pl.BlockSpec((HY_ORDER, 2 * seq_len, block_w), lambda b, c: (0, 0, c)),
        pl.BlockSpec((2 * seq_len, seq_len), lambda b, c: (0, 0), **const),
        pl.BlockSpec((seq_len, 2 * seq_len), lambda b, c: (0, 0), **const),
    ]
    return pl.pallas_call(
        functools.partial(_hyena_kernel, seq_len=seq_len, chain_w=chain_w),
        grid=(n_seq // group, nc),
        in_specs=in_specs,
        out_specs=pl.BlockSpec((rows, block_w), lambda b, c: (b, c)),
        out_shape=jax.ShapeDtypeStruct((n_seq * seq_len, HY_W), BF16),
        compiler_params=_params(("arbitrary", "arbitrary")),
        name="hyena",
    )(hy_all, hy_all, hy_all, cw.reshape(HY_CONV, HY_ORDER + 1, HY_W).transpose(1, 0, 2),
      cb.reshape(HY_ORDER + 1, HY_W), kf, consts["fwd"].astype(BF16), consts["inv"].astype(BF16))


def _pool_kernel(p_ref, w_ref, s_ref, o_ref, *, seq_len):
    shift = _RowShifter(seq_len)
    t = lax.broadcasted_iota(jnp.int32, (p_ref.shape[0], 1), 0) & (seq_len - 1)
    for gi, win in enumerate(POOL_WINDOWS):
        cols = slice(gi * POOL_GW, (gi + 1) * POOL_GW)
        x = p_ref[:, cols]
        half = win // 2
        back, fwd = x, x
        m = 1
        while m < half:
            back = back + shift(back, -m)
            fwd = fwd + shift(fwd, m)
            m *= 2
        s = shift(back, -1) + fwd
        cnt = (jnp.minimum(t + half, seq_len) - jnp.maximum(t - half, 0)).astype(F32)
        y = _dot((s / cnt - x).astype(BF16), w_ref[gi].astype(BF16))
        o_ref[:, cols] = (y * s_ref[:, cols]).astype(o_ref.dtype)


def _pool(p_all, w, scale, seq_len, n_seq, row0, group):
    rows = seq_len * group
    blk0 = row0 // rows
    return pl.pallas_call(
        functools.partial(_pool_kernel, seq_len=seq_len),
        grid=(n_seq // group,),
        in_specs=[
            pl.BlockSpec((rows, POOL_W), lambda b: (blk0 + b, 0)),
            pl.BlockSpec((len(POOL_WINDOWS), POOL_GW, POOL_GW), lambda b: (0, 0, 0)),
            pl.BlockSpec((1, POOL_W), lambda b: (0, 0)),
        ],
        out_specs=pl.BlockSpec((rows, POOL_W), lambda b: (b, 0)),
        out_shape=jax.ShapeDtypeStruct((n_seq * seq_len, POOL_W), BF16),
        compiler_params=_params(("arbitrary",)),
        name="multi_pool",
    )(p_all, w, scale)


def _rope_tables():
    rows = DEC_SEQ // GRID_W
    r = np.repeat(np.arange(rows), GRID_W).astype(np.float64)
    col = np.tile(np.arange(GRID_W), rows).astype(np.float64)
    half = HEAD_DIM // 2
    inv = ROPE_THETA ** (-np.arange(0, half, 2, dtype=np.float64) / half)
    ang = np.concatenate([r[:, None] * inv, col[:, None] * inv], axis=-1)
    cos = np.repeat(np.cos(ang), 2, axis=-1)
    sin = np.repeat(np.sin(ang), 2, axis=-1) * np.tile(np.array([-1.0, 1.0]), half)
    return jnp.asarray(cos, F32), jnp.asarray(sin, F32)


def _hyena_tables(n):
    idx = np.arange(n, dtype=np.float64)
    t = idx / max(n - 1, 1)
    bands = np.linspace(1e-4, HY_BANDS - 1, HY_BANDS)
    f = 2.0 * math.pi * idx[:, None] * bands[None, :] / n
    z = np.zeros((n, HY_EMB_PAD))
    z[:, 0] = t
    z[:, 1:1 + HY_BANDS] = np.cos(f)
    z[:, 1 + HY_BANDS:HY_EMB] = -np.sin(f)
    ang = math.pi * ((idx[:, None] * idx[None, :]) % (2 * n)) / n
    wgt_n = np.where(idx == 0, 1.0, 2.0)
    alt = np.where(idx % 2 == 0, 1.0, -1.0)
    fwd = np.concatenate([np.cos(ang), np.sin(ang)], axis=0)
    inv = np.concatenate([np.cos(ang) * wgt_n[None, :], np.sin(ang) * wgt_n[None, :]], axis=1) / (2 * n)
    fwd[n, :] = alt
    inv[:, n] = alt / (2 * n)
    return {
        "z": jnp.asarray(z, F32),
        "t": jnp.asarray(t[:, None], F32),
        "ck": jnp.asarray(np.cos(ang) * wgt_n[None, :], F32),
        "alt_w": jnp.asarray((alt * wgt_n)[:, None], F32),
        "fwd": jnp.asarray(fwd, F32),
        "inv": jnp.asarray(inv, F32),
    }


def kernel(x_prompt, x_sample, cache_k, cache_v, state_lru_fwd, state_lru_bwd, c, c_ctx, norm_g, w_mod, b_mod, ffn_w13, ffn_w2, ab_w_in, ab_q_norm, ab_k_norm, lru_conv_w, lru_conv_b, lru_gate_w, lru_gate_b, lru_lambda, ab_w_out, cd_w_in, hy_conv_w, hy_conv_b, hy_w1, hy_b1, hy_w2, hy_b2, hy_w3, hy_freq, hy_decay, hy_skip, pool_w, pool_scale, cd_w_out):
    xs = [x_prompt.reshape(N_CTX, D_MODEL), x_sample.reshape(N_LAT, D_MODEL)]
    cond = jnp.concatenate([c_ctx[None], c, jnp.zeros((COND_PAD - N_COND, D_MODEL), F32)], axis=0)
    mods = _modulation(cond, w_mod, b_mod)[:, :N_COND].reshape(DEPTH, N_COND, N_MOD, D_MODEL)
    rope_cos, rope_sin = _rope_tables()

    k_list, v_list, hf_list, hb_list = [], [], [], []
    for l in range(DEPTH):
        mod = mods[l]
        g = norm_g[l]
        (x,) = _ffn(xs, mod, g[0:1], ffn_w13, ffn_w2, l, 0, 0, [N_TOK])
        if l % 2 == 0:
            e = l // 2
            qkv, lxg = _inproj(x, mod, g[1:2], ab_w_in[e], (QKV_W, 2 * LRU_W))
            qn, kn = ab_q_norm[e][None], ab_k_norm[e][None]
            attn_c, kc, vc = _attn_ctx(qkv, qn, kn)
            attn_l = _attn_lat(qkv, cache_k[:, e].reshape(DEC_BATCH, PAST_LEN, KV_W),
                               cache_v[:, e].reshape(DEC_BATCH, PAST_LEN, KV_W), qn, kn, rope_cos, rope_sin)
            gw = 0.5 * jnp.transpose(lru_gate_w[e], (2, 3, 0, 1, 4)).reshape(LRU_HEADS, LRU_BLK, 4 * LRU_BLK)
            gb = lru_gate_b[e].reshape(4, LRU_W)
            lru_args = (lru_conv_w[e], lru_conv_b[e][None], gw, gb, lru_lambda[e])
            zeros = jnp.zeros((BATCH, 1, LRU_W), F32)
            rec_c, hf, hb = _lru(lxg, *lru_args, zeros, zeros, SEQ, BATCH, 0, LRU_CTX_GROUP)
            rec_l, _, _ = _lru(lxg, *lru_args, state_lru_fwd[:, e][:, None], state_lru_bwd[:, e][:, None],
                               DEC_SEQ, DEC_BATCH, N_CTX, 1)
            x = _outproj(x, mod, attn_c, attn_l, rec_c, rec_l, ab_w_out[e])
            k_list.append(kc.reshape(BATCH, SEQ, N_KV_HEADS, HEAD_DIM))
            v_list.append(vc.reshape(BATCH, SEQ, N_KV_HEADS, HEAD_DIM))
            hf_list.append(hf.reshape(BATCH, LRU_W))
            hb_list.append(hb.reshape(BATCH, LRU_W))
        else:
            o = l // 2
            hy, pw = _inproj(x, mod, g[1:2], cd_w_in[o], ((HY_ORDER + 1) * HY_W, POOL_W))
            w1 = jnp.zeros((HY_EMB_PAD, HY_FH), F32).at[:HY_EMB].set(hy_w1[o])
            z_out, p_out = [], []
            for seq_len, n_seq, row0, hy_cfg, pool_group in (
                    (SEQ, BATCH, 0, (HY_CTX_GROUP, HY_W, HY_W), POOL_CTX_GROUP),
                    (DEC_SEQ, DEC_BATCH, N_CTX, (1, HY_LAT_BLOCK_W, HY_CHAIN_W), 1)):
                consts = _hyena_tables(seq_len)
                kf = _hy_filter(consts, w1, hy_b1[o][None], hy_w2[o], hy_b2[o][None], hy_w3[o], hy_freq[o],
                                hy_decay[o][None], hy_skip[o], seq_len)
                z_out.append(_hyena(hy, hy_conv_w[o], hy_conv_b[o], kf, consts, seq_len, n_seq, row0, *hy_cfg))
                p_out.append(_pool(pw, pool_w[o], pool_scale[o][None], seq_len, n_seq, row0, pool_group))
            x = _outproj(x, mod, z_out[0], z_out[1], p_out[0], p_out[1], cd_w_out[o])
        xs = _ffn([x], mod, g[2:3], ffn_w13, ffn_w2, l, 1, 2, [N_TOK] if l + 1 < DEPTH else [N_CTX, N_LAT])

    y_prompt = xs[0].reshape(BATCH, SEQ, D_MODEL)
    y_sample = xs[1].reshape(DEC_BATCH, DEC_SEQ, D_MODEL)
    return (y_prompt, y_sample, jnp.stack(k_list, axis=1), jnp.stack(v_list, axis=1),
            jnp.stack(hf_list, axis=1), jnp.stack(hb_list, axis=1))
```

```python
import functools
import math

import numpy as np
import jax
import jax.numpy as jnp
from jax import lax
from jax.experimental import pallas as pl
from jax.experimental.pallas import tpu as pltpu

F32 = jnp.float32
BF16 = jnp.bfloat16

D_MODEL = 2048
BATCH = 32
SEQ = 256
DEPTH = 2
DEC_BATCH = 2
DEC_SEQ = 1024
PAST_LEN = 512
GRID_W = 64
N_MOD = 9
NORM_EPS = 1e-6
D_FF = 5632
HEAD_DIM = 128
N_Q_HEADS = 8
N_KV_HEADS = 2
Q_PER_KV = N_Q_HEADS // N_KV_HEADS
ATTN_W = N_Q_HEADS * HEAD_DIM
KV_W = N_KV_HEADS * HEAD_DIM
QKV_W = ATTN_W + 2 * KV_W
ROPE_THETA = 10000.0
LRU_W = 1024
LRU_HEADS = 8
LRU_BLK = LRU_W // LRU_HEADS
LRU_CONV = 4
LRU_CONV_LEFT = 2
LRU_C = 8.0
HY_W = 1024
HY_ORDER = 2
HY_CONV = 3
HY_CONV_LEFT = 1
HY_EMB = 33
HY_EMB_PAD = 128
HY_BANDS = (HY_EMB - 1) // 2
HY_FH = 64
POOL_W = 1024
POOL_WINDOWS = (2, 4, 8, 16)
POOL_GW = POOL_W // len(POOL_WINDOWS)

N_CTX = BATCH * SEQ
N_LAT = DEC_BATCH * DEC_SEQ
N_TOK = N_CTX + N_LAT
N_COND = 1 + DEC_BATCH
COND_PAD = 8

VMEM_LIMIT_BYTES = 56 * 1024 * 1024

ROW_TILE = 512
FFN_ROW_TILE = 1024
FF_TILE = 512
FFN_NORM_ROWS = 32
MOD_TILE = 1024
Q_TILE = 256
ATTN_CTX_GROUP = 4
ATTN_CTX_HEADS_PER_CHAIN = 2
ATTN_LAT_HEADS_PER_CHAIN = 1
LRU_CTX_GROUP = 2
LRU_SCAN_ROWS = 8
HY_CTX_GROUP = 2
POOL_CTX_GROUP = 4
HY_CHAIN_W = 512
HY_LAT_BLOCK_W = 512


def _params(sem):
    return pltpu.CompilerParams(dimension_semantics=sem, vmem_limit_bytes=VMEM_LIMIT_BYTES)


def _cond_index(i, tile):
    return jnp.maximum((i * tile) // DEC_SEQ - (N_CTX // DEC_SEQ - 1), 0)


def _sigmoid(x):
    return 1.0 / (1.0 + jnp.exp(-x))


def _dot(a, b):
    return jnp.dot(a, b, preferred_element_type=F32)


def _dot_f32(a, b):
    return jnp.dot(a, b, preferred_element_type=F32, precision=lax.Precision.HIGHEST)


def _split_bf16(x):
    hi = x.astype(BF16)
    return hi, (x - hi.astype(F32)).astype(BF16)


def _dot3(a_hi, a_lo, b):
    b_hi, b_lo = _split_bf16(b)
    return _dot(a_hi, b_hi) + (_dot(a_hi, b_lo) + _dot(a_lo, b_hi))


def _modulated_norm(x, g, mod_ref, j):
    ms = jnp.mean(x * x, axis=-1, keepdims=True)
    y = x * lax.rsqrt(ms + NORM_EPS) * g
    return y * (1.0 + mod_ref[0, 3 * j + 1:3 * j + 2, :]) + mod_ref[0, 3 * j:3 * j + 1, :]


def _head_norm(x, g):
    ms = jnp.mean(x * x, axis=-1, keepdims=True)
    return x * lax.rsqrt(ms + NORM_EPS) * g


class _RowShifter:
    def __init__(self, period):
        self.period = period
        self._masks = {}

    def __call__(self, x, off):
        if off == 0:
            return x
        key = (x.shape, off)
        if key not in self._masks:
            t = lax.broadcasted_iota(jnp.int32, x.shape, 0) & (self.period - 1)
            self._masks[key] = t >= -off if off < 0 else t < self.period - off
        return jnp.where(self._masks[key], pltpu.roll(x, (-off) % x.shape[0], axis=0), 0.0)


def _dwconv(x, w, b, left, shift):
    acc = b + shift(x, -left) * w[0:1, :]
    for j in range(1, w.shape[0]):
        acc = acc + shift(x, j - left) * w[j:j + 1, :]
    return acc


def _mod_kernel(c_ref, w_ref, b_ref, o_ref):
    c = c_ref[...]
    s = (c * _sigmoid(c)).astype(BF16)
    o_ref[0] = _dot(s, w_ref[0].astype(BF16)) + b_ref[0]


def _modulation(cond, w_mod, b_mod):
    n = N_MOD * D_MODEL
    return pl.pallas_call(
        _mod_kernel,
        grid=(DEPTH, n // MOD_TILE),
        in_specs=[
            pl.BlockSpec((COND_PAD, D_MODEL), lambda l, j: (0, 0)),
            pl.BlockSpec((1, D_MODEL, MOD_TILE), lambda l, j: (l, 0, j)),
            pl.BlockSpec((1, 1, MOD_TILE), lambda l, j: (l, 0, j)),
        ],
        out_specs=pl.BlockSpec((1, COND_PAD, MOD_TILE), lambda l, j: (l, 0, j)),
        out_shape=jax.ShapeDtypeStruct((DEPTH, COND_PAD, n), F32),
        compiler_params=_params(("arbitrary", "arbitrary")),
        name="modulation",
    )(cond, w_mod, b_mod.reshape(DEPTH, 1, n))


def _ffn_kernel(*refs, j, n_in, n_out, in_split, out_split):
    x_hbm = refs[:n_in]
    mod_ref, g_ref, w1_ref, w3_ref, w2_ref = refs[n_in:n_in + 5]
    o_hbm = refs[n_in + 5:n_in + 5 + n_out]
    buf, h_ref, sem_in, sem_out = refs[n_in + 5 + n_out:]
    i, f = pl.program_id(0), pl.program_id(1)
    n, nf = pl.num_programs(0), pl.num_programs(1)
    slot = lax.rem(i, 2)
    acc = buf.at[slot]

    def tile_rows(t):
        return pl.ds(pl.multiple_of(t * FFN_ROW_TILE, FFN_ROW_TILE), FFN_ROW_TILE)

    def in_copy(src, t, s):
        return pltpu.make_async_copy(src.at[tile_rows(t), :], buf.at[s], sem_in.at[s])

    def out_copy(dst, t, s):
        return pltpu.make_async_copy(buf.at[s], dst.at[tile_rows(t), :], sem_out.at[s])

    def start_split(copy, refs_, split, t, s):
        if len(refs_) == 1:
            copy(refs_[0], t, s).start()
        else:
            @pl.when(t < split)
            def _():
                copy(refs_[0], t, s).start()

            @pl.when(t >= split)
            def _():
                copy(refs_[1], t - split, s).start()

    @pl.when(f == 0)
    def _():
        @pl.when(i == 0)
        def _():
            start_split(in_copy, x_hbm, in_split, i, slot)

        in_copy(x_hbm[0], 0, slot).wait()
        gain = g_ref[...] * (1.0 + mod_ref[0, 3 * j + 1:3 * j + 2, :])
        shift = mod_ref[0, 3 * j:3 * j + 1, :]

        def chunk(r, carry):
            rows = pl.ds(pl.multiple_of(r * FFN_NORM_ROWS, FFN_NORM_ROWS), FFN_NORM_ROWS)
            x = acc[rows, :]
            ms = jnp.mean(x * x, axis=-1, keepdims=True)
            h_ref[rows, :] = (x * lax.rsqrt(ms + NORM_EPS) * gain + shift).astype(BF16)
            return carry

        lax.fori_loop(0, FFN_ROW_TILE // FFN_NORM_ROWS, chunk, 0, unroll=4)

    @pl.when((f == 1) & (i + 1 < n))
    def _():
        @pl.when(i >= 1)
        def _():
            out_copy(o_hbm[0], 0, 1 - slot).wait()

        start_split(in_copy, x_hbm, in_split, i + 1, 1 - slot)

    h = h_ref[...]
    gt = _dot(h, w1_ref[...].astype(BF16))
    up = _dot(h, w3_ref[...].astype(BF16))
    a = (gt * _sigmoid(gt) * up).astype(BF16)
    half_gate = 0.5 * mod_ref[0, 3 * j + 2:3 * j + 3, :]
    acc[...] += half_gate * _dot(a, w2_ref[...].astype(BF16))

    @pl.when(f == nf - 1)
    def _():
        start_split(out_copy, o_hbm, out_split, i, slot)

        @pl.when(i == n - 1)
        def _():
            out_copy(o_hbm[0], 0, slot).wait()
            out_copy(o_hbm[0], 0, 1 - slot).wait()


def _ffn(xs, mod, g, w13, w2, l, sub, j, out_rows):
    nf = D_FF // FF_TILE
    n_tiles = N_TOK // FFN_ROW_TILE
    assert nf >= 2 and n_tiles >= 2 and sum(x.shape[0] for x in xs) == N_TOK and sum(out_rows) == N_TOK
    assert all(x.shape[0] % FFN_ROW_TILE == 0 for x in xs) and all(r % FFN_ROW_TILE == 0 for r in out_rows)
    any_spec = pl.BlockSpec(memory_space=pl.ANY)
    outs = pl.pallas_call(
        functools.partial(_ffn_kernel, j=j, n_in=len(xs), n_out=len(out_rows),
                          in_split=xs[0].shape[0] // FFN_ROW_TILE, out_split=out_rows[0] // FFN_ROW_TILE),
        grid=(n_tiles, nf),
        in_specs=[any_spec] * len(xs) + [
            pl.BlockSpec((1, N_MOD, D_MODEL), lambda i, f: (_cond_index(i, FFN_ROW_TILE), 0, 0)),
            pl.BlockSpec((1, D_MODEL), lambda i, f: (0, 0)),
            pl.BlockSpec((None, None, D_MODEL, FF_TILE), lambda i, f: (l, sub, 0, f)),
            pl.BlockSpec((None, None, D_MODEL, FF_TILE), lambda i, f: (l, sub, 0, nf + f)),
            pl.BlockSpec((None, None, FF_TILE, D_MODEL), lambda i, f: (l, sub, f, 0)),
        ],
        out_specs=[any_spec] * len(out_rows),
        out_shape=[jax.ShapeDtypeStruct((r, D_MODEL), F32) for r in out_rows],
        scratch_shapes=[
            pltpu.VMEM((2, FFN_ROW_TILE, D_MODEL), F32),
            pltpu.VMEM((FFN_ROW_TILE, D_MODEL), BF16),
            pltpu.SemaphoreType.DMA((2,)),
            pltpu.SemaphoreType.DMA((2,)),
        ],
        compiler_params=_params(("arbitrary", "arbitrary")),
        name="ffn",
    )(*xs, mod, g, w13, w13, w2)
    return outs


W_CHUNK = 512


def _for_each_weight_chunk(w_hbm, w_scr, stage, sem, prepare, body):
    n_chunks = w_hbm.shape[1] // W_CHUNK
    chunks = [slice(c * W_CHUNK, (c + 1) * W_CHUNK) for c in range(n_chunks)]
    first = pl.program_id(0) == 0

    def copy(c, s):
        return pltpu.make_async_copy(w_hbm.at[:, pl.ds(c * W_CHUNK, W_CHUNK)], stage.at[s], sem.at[s])

    @pl.when(first)
    def _():
        copy(0, 0).start()
        lhs = prepare()
        for c, cols in enumerate(chunks):
            s = c % 2
            if c + 1 < n_chunks:
                copy(c + 1, 1 - s).start()
            copy(c, s).wait()
            w_scr[:, cols] = stage[s].astype(BF16)
            body(lhs, cols)

    @pl.when(jnp.logical_not(first))
    def _():
        lhs = prepare()
        for cols in chunks:
            body(lhs, cols)


def _weight_scratch(k, n):
    assert n % W_CHUNK == 0
    return [pltpu.VMEM((k, n), BF16), pltpu.VMEM((2, k, W_CHUNK), F32), pltpu.SemaphoreType.DMA((2,))]


def _inproj_kernel(x_ref, mod_ref, g_ref, w_hbm, *rest, widths):
    o_refs, (w_ref, stage, sem) = rest[:len(widths)], rest[len(widths):]
    starts = [sum(widths[:k]) for k in range(len(widths))]

    def normalise():
        return _modulated_norm(x_ref[...], g_ref[...], mod_ref, 1).astype(BF16)

    def project(h, cols):
        k = max(k for k, s0 in enumerate(starts) if s0 <= cols.start)
        o_refs[k][:, cols.start - starts[k]:cols.stop - starts[k]] = _dot(h, w_ref[:, cols])

    _for_each_weight_chunk(w_hbm, w_ref, stage, sem, normalise, project)


def _inproj(x, mod, g, w, widths):
    n = sum(widths)
    return pl.pallas_call(
        functools.partial(_inproj_kernel, widths=widths),
        grid=(N_TOK // ROW_TILE,),
        in_specs=[
            pl.BlockSpec((ROW_TILE, D_MODEL), lambda i: (i, 0)),
            pl.BlockSpec((1, N_MOD, D_MODEL), lambda i: (_cond_index(i, ROW_TILE), 0, 0)),
            pl.BlockSpec((1, D_MODEL), lambda i: (0, 0)),
            pl.BlockSpec(memory_space=pl.ANY),
        ],
        out_specs=[pl.BlockSpec((ROW_TILE, wd), lambda i: (i, 0)) for wd in widths],
        out_shape=[jax.ShapeDtypeStruct((N_TOK, wd), F32) for wd in widths],
        scratch_shapes=_weight_scratch(D_MODEL, n),
        compiler_params=_params(("arbitrary",)),
        name="mixer_in_proj",
    )(x, mod, g, w)


def _outproj_kernel(x_ref, mod_ref, ac_ref, al_ref, bc_ref, bl_ref, w_hbm, o_ref, w_ref, stage, sem):
    wa = w_ref.shape[0] // 2
    is_ctx = pl.program_id(0) < N_CTX // ROW_TILE

    def pick():
        return jnp.where(is_ctx, ac_ref[...], al_ref[...]), jnp.where(is_ctx, bc_ref[...], bl_ref[...])

    def project(ab, cols):
        y = _dot(ab[0], w_ref[0:wa, cols]) + _dot(ab[1], w_ref[wa:, cols])
        o_ref[:, cols] = x_ref[:, cols] + mod_ref[0, 5:6, cols] * y

    _for_each_weight_chunk(w_hbm, w_ref, stage, sem, pick, project)


def _outproj(x, mod, a_ctx, a_lat, b_ctx, b_lat, w):
    wa = a_ctx.shape[1]
    nc = N_CTX // ROW_TILE
    ctx_spec = pl.BlockSpec((ROW_TILE, wa), lambda i: (jnp.minimum(i, nc - 1), 0))
    lat_spec = pl.BlockSpec((ROW_TILE, wa), lambda i: (jnp.maximum(i - nc, 0), 0))
    return pl.pallas_call(
        _outproj_kernel,
        grid=(N_TOK // ROW_TILE,),
        in_specs=[
            pl.BlockSpec((ROW_TILE, D_MODEL), lambda i: (i, 0)),
            pl.BlockSpec((1, N_MOD, D_MODEL), lambda i: (_cond_index(i, ROW_TILE), 0, 0)),
            ctx_spec, lat_spec, ctx_spec, lat_spec,
            pl.BlockSpec(memory_space=pl.ANY),
        ],
        out_specs=pl.BlockSpec((ROW_TILE, D_MODEL), lambda i: (i, 0)),
        out_shape=jax.ShapeDtypeStruct((N_TOK, D_MODEL), F32),
        scratch_shapes=_weight_scratch(2 * wa, D_MODEL),
        compiler_params=_params(("arbitrary",)),
        name="mixer_out_proj",
    )(x, mod, a_ctx, a_lat, b_ctx, b_lat, w)


def _rope(x, cos, sin):
    lane = lax.broadcasted_iota(jnp.int32, x.shape, 1)
    swapped = jnp.where((lane & 1) == 0, pltpu.roll(x, HEAD_DIM - 1, axis=1), pltpu.roll(x, 1, axis=1))
    return x * cos + swapped * sin


def _attend(q_heads, k, v_ones, o_ref, rows, col0, heads_per_chain):
    t = q_heads[0].shape[0]
    for i0 in range(0, len(q_heads), heads_per_chain):
        heads = q_heads[i0:i0 + heads_per_chain]
        q = jnp.concatenate(heads, axis=0)
        s = lax.dot_general(q, k, (((1,), (1,)), ((), ())), preferred_element_type=F32) * (HEAD_DIM ** -0.5)
        e = jnp.exp(s - jnp.max(s, axis=-1, keepdims=True)).astype(BF16)
        oe = _dot(e, v_ones)
        o = oe[:, 0:HEAD_DIM] / oe[:, HEAD_DIM:]
        for i in range(len(heads)):
            c0 = col0 + (i0 + i) * HEAD_DIM
            o_ref[rows, c0:c0 + HEAD_DIM] = o[i * t:(i + 1) * t].astype(o_ref.dtype)


def _attn_ctx_kernel(qkv_ref, qn_ref, kn_ref, o_ref, kc_ref, vc_ref):
    ones = jnp.ones((SEQ, HEAD_DIM), BF16)
    for s in range(ATTN_CTX_GROUP):
        rows = slice(s * SEQ, (s + 1) * SEQ)
        for kv in range(N_KV_HEADS):
            kcol = ATTN_W + kv * HEAD_DIM
            vcol = ATTN_W + KV_W + kv * HEAD_DIM
            k = _head_norm(qkv_ref[rows, kcol:kcol + HEAD_DIM], kn_ref[...])
            v = qkv_ref[rows, vcol:vcol + HEAD_DIM]
            cache_rows = pl.ds(s * SEQ * N_KV_HEADS + kv, SEQ, stride=N_KV_HEADS)
            kc_ref[cache_rows, :] = k
            vc_ref[cache_rows, :] = v
            qs = []
            for i in range(Q_PER_KV):
                qcol = (kv * Q_PER_KV + i) * HEAD_DIM
                qs.append(_head_norm(qkv_ref[rows, qcol:qcol + HEAD_DIM], qn_ref[...]).astype(BF16))
            _attend(qs, k.astype(BF16), jnp.concatenate([v.astype(BF16), ones], axis=1), o_ref, rows,
                    kv * Q_PER_KV * HEAD_DIM, ATTN_CTX_HEADS_PER_CHAIN)


def _attn_ctx(qkv, qn, kn):
    rows = ATTN_CTX_GROUP * SEQ
    return pl.pallas_call(
        _attn_ctx_kernel,
        grid=(BATCH // ATTN_CTX_GROUP,),
        in_specs=[
            pl.BlockSpec((rows, QKV_W), lambda b: (b, 0)),
            pl.BlockSpec((1, HEAD_DIM), lambda b: (0, 0)),
            pl.BlockSpec((1, HEAD_DIM), lambda b: (0, 0)),
        ],
        out_specs=[
            pl.BlockSpec((rows, ATTN_W), lambda b: (b, 0)),
            pl.BlockSpec((rows * N_KV_HEADS, HEAD_DIM), lambda b: (b, 0)),
            pl.BlockSpec((rows * N_KV_HEADS, HEAD_DIM), lambda b: (b, 0)),
        ],
        out_shape=[
            jax.ShapeDtypeStruct((N_CTX, ATTN_W), BF16),
            jax.ShapeDtypeStruct((N_CTX * N_KV_HEADS, HEAD_DIM), F32),
            jax.ShapeDtypeStruct((N_CTX * N_KV_HEADS, HEAD_DIM), F32),
        ],
        compiler_params=_params(("arbitrary",)),
        name="attention_context",
    )(qkv, qn, kn)


def _attn_lat_kernel(qkv_ref, ck_ref, cv_ref, qn_ref, kn_ref, cos_ref, sin_ref, o_ref, k_scr, v_scr):
    qi = pl.program_id(1)

    @pl.when(qi == 0)
    def _():
        k_scr[0:PAST_LEN, :] = ck_ref[0].astype(BF16)
        for kv in range(N_KV_HEADS):
            kcol = ATTN_W + kv * HEAD_DIM
            vcol = ATTN_W + KV_W + kv * HEAD_DIM
            k = _head_norm(qkv_ref[:, kcol:kcol + HEAD_DIM], kn_ref[...])
            k = _rope(k, cos_ref[...], sin_ref[...])
            k_scr[PAST_LEN:, kv * HEAD_DIM:(kv + 1) * HEAD_DIM] = k.astype(BF16)
            v0 = 2 * kv * HEAD_DIM
            v_scr[0:PAST_LEN, v0:v0 + HEAD_DIM] = cv_ref[0, :, kv * HEAD_DIM:(kv + 1) * HEAD_DIM].astype(BF16)
            v_scr[PAST_LEN:, v0:v0 + HEAD_DIM] = qkv_ref[:, vcol:vcol + HEAD_DIM].astype(BF16)
            v_scr[:, v0 + HEAD_DIM:v0 + 2 * HEAD_DIM] = jnp.ones((PAST_LEN + DEC_SEQ, HEAD_DIM), BF16)

    r0 = pl.multiple_of(qi * Q_TILE, Q_TILE)
    cos = cos_ref[pl.ds(r0, Q_TILE), :]
    sin = sin_ref[pl.ds(r0, Q_TILE), :]
    for kv in range(N_KV_HEADS):
        qs = []
        for i in range(Q_PER_KV):
            qcol = (kv * Q_PER_KV + i) * HEAD_DIM
            q = _head_norm(qkv_ref[pl.ds(r0, Q_TILE), qcol:qcol + HEAD_DIM], qn_ref[...])
            qs.append(_rope(q, cos, sin).astype(BF16))
        _attend(qs, k_scr[:, kv * HEAD_DIM:(kv + 1) * HEAD_DIM], v_scr[:, 2 * kv * HEAD_DIM:2 * (kv + 1) * HEAD_DIM],
                o_ref, slice(None), kv * Q_PER_KV * HEAD_DIM, ATTN_LAT_HEADS_PER_CHAIN)


def _attn_lat(qkv, cache_k, cache_v, qn, kn, cos, sin):
    row0 = N_CTX // DEC_SEQ
    qt = DEC_SEQ // Q_TILE
    return pl.pallas_call(
        _attn_lat_kernel,
        grid=(DEC_BATCH, qt),
        in_specs=[
            pl.BlockSpec((DEC_SEQ, QKV_W), lambda b, q: (row0 + b, 0)),
            pl.BlockSpec((1, PAST_LEN, KV_W), lambda b, q: (b, 0, 0)),
            pl.BlockSpec((1, PAST_LEN, KV_W), lambda b, q: (b, 0, 0)),
            pl.BlockSpec((1, HEAD_DIM), lambda b, q: (0, 0)),
            pl.BlockSpec((1, HEAD_DIM), lambda b, q: (0, 0)),
            pl.BlockSpec((DEC_SEQ, HEAD_DIM), lambda b, q: (0, 0)),
            pl.BlockSpec((DEC_SEQ, HEAD_DIM), lambda b, q: (0, 0)),
        ],
        out_specs=pl.BlockSpec((Q_TILE, ATTN_W), lambda b, q: (b * qt + q, 0)),
        out_shape=jax.ShapeDtypeStruct((N_LAT, ATTN_W), BF16),
        scratch_shapes=[pltpu.VMEM((PAST_LEN + DEC_SEQ, KV_W), BF16),
                        pltpu.VMEM((PAST_LEN + DEC_SEQ, 2 * KV_W), BF16)],
        compiler_params=_params(("arbitrary", "arbitrary")),
        name="attention_latent",
    )(qkv, cache_k, cache_v, qn, kn, cos, sin)


def _lru_kernel(l_ref, cw_ref, cb_ref, gw_ref, gb_ref, lam_ref, h0f_ref, h0b_ref, o_ref, hf_ref, hb_ref,
                af_scr, bf_scr, ab_scr, bb_scr, yf_scr, yb_scr, *, seq_len):
    group = l_ref.shape[0] // seq_len
    xc = _dwconv(l_ref[:, 0:LRU_W], cw_ref[...], cb_ref[...], LRU_CONV_LEFT, _RowShifter(seq_len))
    xcb = xc.astype(BF16)
    half_xc = 0.5 * xc
    half_rate = []
    for d in range(2):
        z = -lam_ref[d:d + 1, :]
        half_rate.append((0.5 * LRU_C) * (jnp.maximum(z, 0.0) + jnp.log1p(jnp.exp(-jnp.abs(z)))))
    half_gb = 0.5 * gb_ref[...]
    for hd in range(LRU_HEADS):
        cols = slice(hd * LRU_BLK, (hd + 1) * LRU_BLK)
        pre = _dot(xcb[:, cols], gw_ref[hd].astype(BF16))
        for d, (a_scr, b_scr) in enumerate(((af_scr, bf_scr), (ab_scr, bb_scr))):
            base = d * 2 * LRU_BLK
            tr = jnp.tanh(pre[:, base:base + LRU_BLK] + half_gb[2 * d:2 * d + 1, cols])
            ti = jnp.tanh(pre[:, base + LRU_BLK:base + 2 * LRU_BLK] + half_gb[2 * d + 1:2 * d + 2, cols])
            hr = half_rate[d][:, cols]
            neg_log_a = tr * hr + hr
            a = jnp.exp(-neg_log_a)
            a_scr[:, cols] = a
            var = jnp.tanh(neg_log_a) * (1.0 + a * a)
            b_scr[:, cols] = jnp.where(var > 0.0, var * lax.rsqrt(var), 0.0) * ((ti + 1.0) * half_xc[:, cols])

    n_tiles = seq_len // LRU_SCAN_ROWS

    def tile_step(k, carry):
        out = list(carry)
        base_f = pl.multiple_of(k * LRU_SCAN_ROWS, LRU_SCAN_ROWS)
        base_b = pl.multiple_of((n_tiles - 1 - k) * LRU_SCAN_ROWS, LRU_SCAN_ROWS)
        for g in range(group):
            rows_f = pl.ds(g * seq_len + base_f, LRU_SCAN_ROWS)
            rows_b = pl.ds(g * seq_len + base_b, LRU_SCAN_ROWS)
            af, bf, yf = af_scr.at[rows_f, :], bf_scr.at[rows_f, :], yf_scr.at[rows_f, :]
            ab, bb, yb = ab_scr.at[rows_b, :], bb_scr.at[rows_b, :], yb_scr.at[rows_b, :]
            hf, hb = out[2 * g], out[2 * g + 1]
            for r in range(LRU_SCAN_ROWS):
                rb = LRU_SCAN_ROWS - 1 - r
                hf = af[r:r + 1, :] * hf + bf[r:r + 1, :]
                hb = ab[rb:rb + 1, :] * hb + bb[rb:rb + 1, :]
                yf[r:r + 1, :] = hf
                yb[rb:rb + 1, :] = hb
            out[2 * g], out[2 * g + 1] = hf, hb
        return tuple(out)

    init = []
    for g in range(group):
        init += [h0f_ref[g], h0b_ref[g]]
    final = lax.fori_loop(0, n_tiles, tile_step, tuple(init))
    for g in range(group):
        hf_ref[g] = final[2 * g]
        hb_ref[g] = final[2 * g + 1]
    lg = l_ref[:, LRU_W:]
    gelu = 0.5 * lg * (1.0 + jnp.tanh(math.sqrt(2.0 / math.pi) * (lg + 0.044715 * (lg * lg * lg))))
    o_ref[...] = (gelu * (yf_scr[...] + yb_scr[...])).astype(o_ref.dtype)


def _lru(l_all, cw, cb, gw, gb, lam, h0f, h0b, seq_len, n_seq, row0, group):
    rows = seq_len * group
    blk0 = row0 // rows
    vec = lambda b: (0, 0)
    return pl.pallas_call(
        functools.partial(_lru_kernel, seq_len=seq_len),
        grid=(n_seq // group,),
        in_specs=[
            pl.BlockSpec((rows, 2 * LRU_W), lambda b: (blk0 + b, 0)),
            pl.BlockSpec((LRU_CONV, LRU_W), vec),
            pl.BlockSpec((1, LRU_W), vec),
            pl.BlockSpec((LRU_HEADS, LRU_BLK, 4 * LRU_BLK), lambda b: (0, 0, 0)),
            pl.BlockSpec((4, LRU_W), vec),
            pl.BlockSpec((2, LRU_W), vec),
            pl.BlockSpec((group, 1, LRU_W), lambda b: (b, 0, 0)),
            pl.BlockSpec((group, 1, LRU_W), lambda b: (b, 0, 0)),
        ],
        out_specs=[
            pl.BlockSpec((rows, LRU_W), lambda b: (b, 0)),
            pl.BlockSpec((group, 1, LRU_W), lambda b: (b, 0, 0)),
            pl.BlockSpec((group, 1, LRU_W), lambda b: (b, 0, 0)),
        ],
        out_shape=[
            jax.ShapeDtypeStruct((n_seq * seq_len, LRU_W), BF16),
            jax.ShapeDtypeStruct((n_seq, 1, LRU_W), F32),
            jax.ShapeDtypeStruct((n_seq, 1, LRU_W), F32),
        ],
        scratch_shapes=[pltpu.VMEM((rows, LRU_W), F32)] * 6,
        compiler_params=_params(("arbitrary",)),
        name="rg_lru",
    )(l_all, cw, cb, gw, gb, lam, h0f, h0b)


def _hy_filter_kernel(z_ref, t_ref, ckh_ref, ckl_ref, alt_ref, w1_ref, b1_ref, w2_ref, b2_ref, w3_ref, fr_ref,
                      dec_ref, skip_ref, kf_ref):
    z = jnp.sin(fr_ref[0:1, :] * (_dot_f32(z_ref[...], w1_ref[...]) + b1_ref[...]))
    z = jnp.sin(fr_ref[1:2, :] * (_dot_f32(z, w2_ref[...]) + b2_ref[...]))
    zh, zl = _split_bf16(z)
    filt = _dot3(zh, zl, w3_ref[...]) * jnp.exp(-t_ref[...] * jnp.abs(dec_ref[...]))
    filt = filt / jnp.sum(jnp.abs(filt), axis=0, keepdims=True)
    n = t_ref.shape[0]
    kf = _dot3(ckh_ref[...], ckl_ref[...], filt)
    kn = jnp.sum(alt_ref[...] * filt, axis=0, keepdims=True)
    for o in range(HY_ORDER):
        cols = slice(o * HY_W, (o + 1) * HY_W)
        skip = skip_ref[o:o + 1, :]
        kf_ref[o, 0:n, :] = kf[:, cols] + skip
        kf_ref[o, n:, :] = kf[:, cols] + skip
        kf_ref[o, n:n + 1, :] = kn[:, cols] + skip


def _hy_filter(consts, w1, b1, w2, b2, w3, freq, decay, skip, seq_len):
    ckh, ckl = _split_bf16(consts["ck"])
    return pl.pallas_call(
        _hy_filter_kernel,
        out_shape=jax.ShapeDtypeStruct((HY_ORDER, 2 * seq_len, HY_W), F32),
        compiler_params=pltpu.CompilerParams(vmem_limit_bytes=VMEM_LIMIT_BYTES),
        name="hyena_filter",
    )(consts["z"], consts["t"], ckh, ckl, consts["alt_w"], w1, b1, w2, b2, w3, freq, decay, skip)


def _longconv(u, fwd_ref, inv_ref, kf2):
    spec = _dot(fwd_ref[...], u.astype(BF16)) * kf2
    return _dot(inv_ref[...], spec.astype(BF16))


def _hyena_kernel(x1_ref, x2_ref, v_ref, cw_ref, cb_ref, kf_ref, fwd_ref, inv_ref, *rest, seq_len, chain_w):
    if len(rest) == 1:
        (o_ref,) = rest
    else:
        p_ref, pw_ref, ps_ref, o_ref, po_ref = rest
        _pool_kernel(p_ref, pw_ref, ps_ref, po_ref, seq_len=seq_len)
    shift = _RowShifter(seq_len)
    v = _dwconv(v_ref[...], cw_ref[2], cb_ref[2:3, :], HY_CONV_LEFT, shift)
    gates = [_dwconv(g_ref[...], cw_ref[o], cb_ref[o:o + 1, :], HY_CONV_LEFT, shift)
             for o, g_ref in enumerate((x1_ref, x2_ref))]
    for s in range(v.shape[0] // seq_len):
        rows = slice(s * seq_len, (s + 1) * seq_len)
        for c0 in range(0, v.shape[1], chain_w):
            cols = slice(c0, c0 + chain_w)
            z = v[rows, cols]
            for o in range(HY_ORDER):
                z = gates[o][rows, cols] * _longconv(z, fwd_ref, inv_ref, kf_ref[o, :, cols])
            o_ref[rows, cols] = z.astype(o_ref.dtype)


def _hyena(hy_all, cw, cb, kf, consts, seq_len, n_seq, row0, group, block_w, chain_w, pool=None):
    rows = seq_len * group
    blk0 = row0 // rows
    nc = HY_W // block_w
    assert pool is None or nc == 1
    const = dict(pipeline_mode=pl.Buffered(1))
    in_specs = [pl.BlockSpec((rows, block_w), lambda b, c, g=g: (blk0 + b, g * nc + c)) for g in range(3)]
    in_specs += [
        pl.BlockSpec((HY_ORDER + 1, HY_CONV, block_w), lambda b, c: (0, 0, c)),
        pl.BlockSpec((HY_ORDER + 1, block_w), lambda b, c: (0, c)),
        pl.BlockSpec((HY_ORDER, 2 * seq_len, block_w), lambda b, c: (0, 0, c)),
        pl.BlockSpec((2 * seq_len, seq_len), lambda b, c: (0, 0), **const),
        pl.BlockSpec((seq_len, 2 * seq_len), lambda b, c: (0, 0), **const),
    ]
    out_specs = pl.BlockSpec((rows, block_w), lambda b, c: (b, c))
    out_shape = jax.ShapeDtypeStruct((n_seq * seq_len, HY_W), BF16)
    args = [hy_all, hy_all, hy_all, cw.reshape(HY_CONV, HY_ORDER + 1, HY_W).transpose(1, 0, 2),
            cb.reshape(HY_ORDER + 1, HY_W), kf, consts["fwd"].astype(BF16), consts["inv"].astype(BF16)]
    if pool is not None:
        in_specs += [
            pl.BlockSpec((rows, POOL_W), lambda b, c: (blk0 + b, 0)),
            pl.BlockSpec((len(POOL_WINDOWS), POOL_GW, POOL_GW), lambda b, c: (0, 0, 0)),
            pl.BlockSpec((1, POOL_W), lambda b, c: (0, 0)),
        ]
        args += list(pool)
        out_specs = [out_specs, pl.BlockSpec((rows, POOL_W), lambda b, c: (b, 0))]
        out_shape = [out_shape, jax.ShapeDtypeStruct((n_seq * seq_len, POOL_W), BF16)]
    return pl.pallas_call(
        functools.partial(_hyena_kernel, seq_len=seq_len, chain_w=chain_w),
        grid=(n_seq // group, nc),
        in_specs=in_specs,
        out_specs=out_specs,
        out_shape=out_shape,
        compiler_params=_params(("arbitrary", "arbitrary")),
        name="hyena",
    )(*args)


def _pool_kernel(p_ref, w_ref, s_ref, o_ref, *, seq_len):
    shift = _RowShifter(seq_len)
    t = lax.broadcasted_iota(jnp.int32, (p_ref.shape[0], 1), 0) & (seq_len - 1)
    for gi, win in enumerate(POOL_WINDOWS):
        cols = slice(gi * POOL_GW, (gi + 1) * POOL_GW)
        x = p_ref[:, cols]
        half = win // 2
        back, fwd = x, x
        m = 1
        while m < half:
            back = back + shift(back, -m)
            fwd = fwd + shift(fwd, m)
            m *= 2
        s = shift(back, -1) + fwd
        cnt = (jnp.minimum(t + half, seq_len) - jnp.maximum(t - half, 0)).astype(F32)
        y = _dot((s / cnt - x).astype(BF16), w_ref[gi].astype(BF16))
        o_ref[:, cols] = (y * s_ref[:, cols]).astype(o_ref.dtype)


def _pool(p_all, w, scale, seq_len, n_seq, row0, group):
    rows = seq_len * group
    blk0 = row0 // rows
    return pl.pallas_call(
        functools.partial(_pool_kernel, seq_len=seq_len),
        grid=(n_seq // group,),
        in_specs=[
            pl.BlockSpec((rows, POOL_W), lambda b: (blk0 + b, 0)),
            pl.BlockSpec((len(POOL_WINDOWS), POOL_GW, POOL_GW), lambda b: (0, 0, 0)),
            pl.BlockSpec((1, POOL_W), lambda b: (0, 0)),
        ],
        out_specs=pl.BlockSpec((rows, POOL_W), lambda b: (b, 0)),
        out_shape=jax.ShapeDtypeStruct((n_seq * seq_len, POOL_W), BF16),
        compiler_params=_params(("arbitrary",)),
        name="multi_pool",
    )(p_all, w, scale)


def _rope_tables():
    rows = DEC_SEQ // GRID_W
    r = np.repeat(np.arange(rows), GRID_W).astype(np.float64)
    col = np.tile(np.arange(GRID_W), rows).astype(np.float64)
    half = HEAD_DIM // 2
    inv = ROPE_THETA ** (-np.arange(0, half, 2, dtype=np.float64) / half)
    ang = np.concatenate([r[:, None] * inv, col[:, None] * inv], axis=-1)
    cos = np.repeat(np.cos(ang), 2, axis=-1)
    sin = np.repeat(np.sin(ang), 2, axis=-1) * np.tile(np.array([-1.0, 1.0]), half)
    return jnp.asarray(cos, F32), jnp.asarray(sin, F32)


def _hyena_tables(n):
    idx = np.arange(n, dtype=np.float64)
    t = idx / max(n - 1, 1)
    bands = np.linspace(1e-4, HY_BANDS - 1, HY_BANDS)
    f = 2.0 * math.pi * idx[:, None] * bands[None, :] / n
    z = np.zeros((n, HY_EMB_PAD))
    z[:, 0] = t
    z[:, 1:1 + HY_BANDS] = np.cos(f)
    z[:, 1 + HY_BANDS:HY_EMB] = -np.sin(f)
    ang = math.pi * ((idx[:, None] * idx[None, :]) % (2 * n)) / n
    wgt_n = np.where(idx == 0, 1.0, 2.0)
    alt = np.where(idx % 2 == 0, 1.0, -1.0)
    fwd = np.concatenate([np.cos(ang), np.sin(ang)], axis=0)
    inv = np.concatenate([np.cos(ang) * wgt_n[None, :], np.sin(ang) * wgt_n[None, :]], axis=1) / (2 * n)
    fwd[n, :] = alt
    inv[:, n] = alt / (2 * n)
    return {
        "z": jnp.asarray(z, F32),
        "t": jnp.asarray(t[:, None], F32),
        "ck": jnp.asarray(np.cos(ang) * wgt_n[None, :], F32),
        "alt_w": jnp.asarray((alt * wgt_n)[:, None], F32),
        "fwd": jnp.asarray(fwd, F32),
        "inv": jnp.asarray(inv, F32),
    }


def kernel(x_prompt, x_sample, cache_k, cache_v, state_lru_fwd, state_lru_bwd, c, c_ctx, norm_g, w_mod, b_mod, ffn_w13, ffn_w2, ab_w_in, ab_q_norm, ab_k_norm, lru_conv_w, lru_conv_b, lru_gate_w, lru_gate_b, lru_lambda, ab_w_out, cd_w_in, hy_conv_w, hy_conv_b, hy_w1, hy_b1, hy_w2, hy_b2, hy_w3, hy_freq, hy_decay, hy_skip, pool_w, pool_scale, cd_w_out):
    xs = [x_prompt.reshape(N_CTX, D_MODEL), x_sample.reshape(N_LAT, D_MODEL)]
    cond = jnp.concatenate([c_ctx[None], c, jnp.zeros((COND_PAD - N_COND, D_MODEL), F32)], axis=0)
    mods = _modulation(cond, w_mod, b_mod)[:, :N_COND].reshape(DEPTH, N_COND, N_MOD, D_MODEL)
    rope_cos, rope_sin = _rope_tables()

    k_list, v_list, hf_list, hb_list = [], [], [], []
    for l in range(DEPTH):
        mod = mods[l]
        g = norm_g[l]
        (x,) = _ffn(xs, mod, g[0:1], ffn_w13, ffn_w2, l, 0, 0, [N_TOK])
        if l % 2 == 0:
            e = l // 2
            qkv, lxg = _inproj(x, mod, g[1:2], ab_w_in[e], (QKV_W, 2 * LRU_W))
            qn, kn = ab_q_norm[e][None], ab_k_norm[e][None]
            attn_c, kc, vc = _attn_ctx(qkv, qn, kn)
            attn_l = _attn_lat(qkv, cache_k[:, e].reshape(DEC_BATCH, PAST_LEN, KV_W),
                               cache_v[:, e].reshape(DEC_BATCH, PAST_LEN, KV_W), qn, kn, rope_cos, rope_sin)
            gw = 0.5 * jnp.transpose(lru_gate_w[e], (2, 3, 0, 1, 4)).reshape(LRU_HEADS, LRU_BLK, 4 * LRU_BLK)
            gb = lru_gate_b[e].reshape(4, LRU_W)
            lru_args = (lru_conv_w[e], lru_conv_b[e][None], gw, gb, lru_lambda[e])
            zeros = jnp.zeros((BATCH, 1, LRU_W), F32)
            rec_c, hf, hb = _lru(lxg, *lru_args, zeros, zeros, SEQ, BATCH, 0, LRU_CTX_GROUP)
            rec_l, _, _ = _lru(lxg, *lru_args, state_lru_fwd[:, e][:, None], state_lru_bwd[:, e][:, None],
                               DEC_SEQ, DEC_BATCH, N_CTX, 1)
            x = _outproj(x, mod, attn_c, attn_l, rec_c, rec_l, ab_w_out[e])
            k_list.append(kc.reshape(BATCH, SEQ, N_KV_HEADS, HEAD_DIM))
            v_list.append(vc.reshape(BATCH, SEQ, N_KV_HEADS, HEAD_DIM))
            hf_list.append(hf.reshape(BATCH, LRU_W))
            hb_list.append(hb.reshape(BATCH, LRU_W))
        else:
            o = l // 2
            hy, pw = _inproj(x, mod, g[1:2], cd_w_in[o], ((HY_ORDER + 1) * HY_W, POOL_W))
            w1 = jnp.zeros((HY_EMB_PAD, HY_FH), F32).at[:HY_EMB].set(hy_w1[o])
            z_out, p_out = [], []
            for seq_len, n_seq, row0, hy_cfg, pool_group in (
                    (SEQ, BATCH, 0, (HY_CTX_GROUP, HY_W, HY_W), POOL_CTX_GROUP),
                    (DEC_SEQ, DEC_BATCH, N_CTX, (1, HY_LAT_BLOCK_W, HY_CHAIN_W), 1)):
                consts = _hyena_tables(seq_len)
                kf = _hy_filter(consts, w1, hy_b1[o][None], hy_w2[o], hy_b2[o][None], hy_w3[o], hy_freq[o],
                                hy_decay[o][None], hy_skip[o], seq_len)
                pool_args = (pw, pool_w[o], pool_scale[o][None])
                if hy_cfg[1] == HY_W:
                    z, p = _hyena(hy, hy_conv_w[o], hy_conv_b[o], kf, consts, seq_len, n_seq, row0, *hy_cfg,
                                  pool=pool_args)
                else:
                    z = _hyena(hy, hy_conv_w[o], hy_conv_b[o], kf, consts, seq_len, n_seq, row0, *hy_cfg)
                    p = _pool(*pool_args, seq_len, n_seq, row0, pool_group)
                z_out.append(z)
                p_out.append(p)
            x = _outproj(x, mod, z_out[0], z_out[1], p_out[0], p_out[1], cd_w_out[o])
        xs = _ffn([x], mod, g[2:3], ffn_w13, ffn_w2, l, 1, 2, [N_TOK] if l + 1 < DEPTH else [N_CTX, N_LAT])

    y_prompt = xs[0].reshape(BATCH, SEQ, D_MODEL)
    y_sample = xs[1].reshape(DEC_BATCH, DEC_SEQ, D_MODEL)
    return (y_prompt, y_sample, jnp.stack(k_list, axis=1), jnp.stack(v_list, axis=1),
            jnp.stack(hf_list, axis=1), jnp.stack(hb_list, axis=1))
```
